```python
import jax, jax.numpy as jnp
from jax import lax
import numpy as np

D_MODEL = 1024
BATCH = 2
SEQ = 16384
DEPTH = 1
DEC_BATCH = 16
DEC_SEQ = 32
PAST_LEN = 1024

CHUNK = 64
Q_BLOCK = 128
SB_HEADS = 8
SB_HEAD_DIM = 64
SB_WIDTH = SB_HEADS * SB_HEAD_DIM
RW_HEADS = 8
RW_HEAD_DIM = 64
RW_WIDTH = RW_HEADS * RW_HEAD_DIM
RW_DECAY_LORA = 64
RW_AAA_LORA = 64
RW_GATE_LORA = 128
RW_PROJ = 3 * RW_WIDTH + RW_DECAY_LORA + RW_AAA_LORA + RW_GATE_LORA
RW_GN_EPS = 64e-5
RW_SPLITS = (RW_WIDTH, 2 * RW_WIDTH, 3 * RW_WIDTH, 3 * RW_WIDTH + RW_DECAY_LORA,
             3 * RW_WIDTH + RW_DECAY_LORA + RW_AAA_LORA)
IN_WIDTH = 3 * SB_WIDTH + RW_PROJ + 2 * D_MODEL
IN_SPLITS = (SB_WIDTH, 2 * SB_WIDTH, 3 * SB_WIDTH, 3 * SB_WIDTH + RW_PROJ,
             3 * SB_WIDTH + RW_PROJ + D_MODEL)
N_EXPERTS = 32
TOP_K = 4
D_EXPERT = D_MODEL
SWIGLU_LIMIT = 7.0
SWIGLU_ALPHA = 1.702
EXPERT_BLOCK = 128
RMS_EPS = 1e-6

kernel_name = "stickbreak_rwkv7_moe_stream_step"


def rmsnorm(x, g):
    xf = x.astype(jnp.float32)
    y = xf * lax.rsqrt(jnp.mean(xf * xf, axis=-1, keepdims=True) + RMS_EPS)
    return (y * g.astype(jnp.float32)).astype(x.dtype)


def stick_breaking(q, k, v, q_pos, k_pos):
    z = jnp.einsum('bhqd,bhsd->bhqs', q.astype(jnp.float32), k.astype(jnp.float32)) * (SB_HEAD_DIM ** -0.5)
    visible = k_pos[None, :] < q_pos[:, None]
    log_beta = jax.nn.log_sigmoid(z)
    log_keep = jnp.where(visible, log_beta - z, 0.0)
    later = lax.cumsum(log_keep, axis=3, reverse=True) - log_keep
    wts = jnp.where(visible, jnp.exp(log_beta + later), 0.0)
    return jnp.einsum('bhqs,bhsd->bhqd', wts, v.astype(jnp.float32)).astype(v.dtype)


def sb_prompt(q, k, v):
    B, H, S, dh = q.shape
    nb = S // Q_BLOCK
    qb = jnp.moveaxis(q.reshape(B, H, nb, Q_BLOCK, dh), 2, 0)
    pos = jnp.arange(S, dtype=jnp.int32)
    out = lax.map(lambda a: stick_breaking(a[0], k, v, a[1], pos), (qb, pos.reshape(nb, Q_BLOCK)))
    return jnp.moveaxis(out, 0, 2).reshape(B, H, S, dh)


def sb_sample(q, k, v, cache_k, cache_v):
    T = q.shape[2]
    P = cache_k.shape[2]
    k_all = jnp.concatenate([cache_k.astype(k.dtype), k], axis=2)
    v_all = jnp.concatenate([cache_v.astype(v.dtype), v], axis=2)
    pos = jnp.arange(P + T, dtype=jnp.int32)
    return stick_breaking(q, k_all, v_all, pos[P:], pos)


def wkv7_scan(r, decay, k, v, a, b, S0):
    def step(S, inp):
        rt, wt, kt, vt, at, bt = inp
        Sa = jnp.einsum('bhij,bhj->bhi', S, at)
        S = S * wt[:, :, None, :] + Sa[..., None] * bt[:, :, None, :] + vt[..., None] * kt[:, :, None, :]
        return S, jnp.einsum('bhij,bhj->bhi', S, rt)
    seq = tuple(jnp.moveaxis(t, 1, 0) for t in (r, decay, k, v, a, b))
    S, ys = lax.scan(step, S0, seq)
    return jnp.moveaxis(ys, 0, 1), S


def rwkv7_mix(p, p_prev, S0, lp):
    B, T, _ = p.shape
    f32 = jnp.float32
    p_shift = jnp.concatenate([p_prev.astype(p.dtype), p[:, :-1]], axis=1)
    xs = p + (p_shift - p) * lp['rw_mu']
    r, k, v, dw, da, dg = jnp.split(xs, RW_SPLITS, axis=-1)
    w = -jax.nn.softplus(-(lp['rw_w0'] + jnp.tanh(dw) @ lp['rw_w2']).astype(f32)) - 0.5
    decay = jnp.exp(-jnp.exp(w))
    a = jax.nn.sigmoid((lp['rw_a0'] + da @ lp['rw_a2']).astype(f32))
    g = jax.nn.sigmoid(dg) @ lp['rw_g2']
    hs = lambda t: t.astype(f32).reshape(B, T, RW_HEADS, RW_HEAD_DIM)
    r, k, v, decay, a = hs(r), hs(k), hs(v), hs(decay), hs(a)
    kk = k * lp['rw_k_k'].astype(f32).reshape(RW_HEADS, RW_HEAD_DIM)
    kk = kk / jnp.maximum(jnp.sqrt(jnp.sum(kk * kk, axis=-1, keepdims=True)), 1e-12)
    k = k * (1.0 + (a - 1.0) * lp['rw_k_a'].astype(f32).reshape(RW_HEADS, RW_HEAD_DIM))
    y, S = wkv7_scan(r, decay, k, v, -kk, kk * a, S0.astype(f32))
    mean = jnp.mean(y, axis=-1, keepdims=True)
    var = jnp.mean(jnp.square(y - mean), axis=-1, keepdims=True)
    y = ((y - mean) * lax.rsqrt(var + RW_GN_EPS)).reshape(B, T, RW_WIDTH)
    y = y * lp['rw_ln_g'].astype(f32) + lp['rw_ln_b'].astype(f32)
    bonus = jnp.sum(r * k * lp['rw_r_k'].astype(f32), axis=-1, keepdims=True) * v
    y = (y + bonus.reshape(B, T, RW_WIDTH)) * g.astype(f32)
    return y.astype(p.dtype), S, p[:, -1:]


def moe_ffn(h, w_router, b_router, w_gate_up, b_gate_up, w_down, b_down):
    B, T, _ = h.shape
    n_tok = B * T
    xs = h.reshape(n_tok, D_MODEL)
    logits = (xs @ w_router + b_router).astype(jnp.float32)
    top_logit, top_idx = lax.top_k(logits, TOP_K)
    gate = jax.nn.softmax(top_logit, axis=-1)
    n_assign = n_tok * TOP_K
    flat_e = top_idx.reshape(n_assign)
    flat_tok = jnp.broadcast_to(jnp.arange(n_tok, dtype=jnp.int32)[:, None], (n_tok, TOP_K)).reshape(n_assign)
    flat_g = gate.reshape(n_assign)
    order = jnp.argsort(flat_e)
    sorted_e = flat_e[order]
    counts = jnp.bincount(flat_e, length=N_EXPERTS)
    padded = (counts + EXPERT_BLOCK - 1) // EXPERT_BLOCK * EXPERT_BLOCK
    pad_end = jnp.cumsum(padded)
    dest = (pad_end - padded)[sorted_e] + jnp.arange(n_assign, dtype=jnp.int32) - (jnp.cumsum(counts) - counts)[sorted_e]
    n_blocks = -(-n_assign // EXPERT_BLOCK) + N_EXPERTS
    n_rows = n_blocks * EXPERT_BLOCK
    row_tok = jnp.zeros((n_rows,), jnp.int32).at[dest].set(flat_tok[order])
    row_gate = jnp.zeros((n_rows,), jnp.float32).at[dest].set(flat_g[order])
    block_expert = jnp.minimum(
        jnp.searchsorted(pad_end, jnp.arange(n_blocks, dtype=jnp.int32) * EXPERT_BLOCK, side='right'), N_EXPERTS - 1)

    def expert_block(args):
        xb, e = args
        gu = xb @ w_gate_up[e] + b_gate_up[e]
        g_, u_ = jnp.split(gu, 2, axis=-1)
        g_ = jnp.minimum(g_, SWIGLU_LIMIT)
        u_ = jnp.clip(u_, -SWIGLU_LIMIT, SWIGLU_LIMIT)
        act = (u_ + 1.0) * (g_ * jax.nn.sigmoid(g_ * SWIGLU_ALPHA))
        return act @ w_down[e] + b_down[e]

    y_rows = lax.map(expert_block, (xs[row_tok].reshape(n_blocks, EXPERT_BLOCK, D_MODEL), block_expert))
    out = jnp.zeros((n_tok, D_MODEL), jnp.float32).at[row_tok].add(
        y_rows.reshape(n_rows, D_MODEL).astype(jnp.float32) * row_gate[:, None])
    return out.astype(h.dtype).reshape(B, T, D_MODEL)


def trunk_layer(x, attend, S0, p_prev, lp):
    B, T, _ = x.shape
    h = rmsnorm(x, lp['norm1_g'])
    p = h @ lp['w_in']
    q, k, v, rw, ga, gb = jnp.split(p, IN_SPLITS, axis=-1)
    heads = lambda t: t.reshape(B, T, SB_HEADS, SB_HEAD_DIM).transpose(0, 2, 1, 3)
    q, k, v = heads(q), heads(k), heads(v)
    o_a = attend(q, k, v).transpose(0, 2, 1, 3).reshape(B, T, SB_WIDTH)
    o_b, S, p_last = rwkv7_mix(rw, p_prev, S0, lp)
    merged = jax.nn.sigmoid(ga) * (o_a @ lp['w_branch_a']) + jax.nn.sigmoid(gb) * (o_b @ lp['w_branch_b'])
    x = x + merged @ lp['w_out']
    x = x + moe_ffn(rmsnorm(x, lp['norm2_g']), lp['w_router'], lp['b_router'], lp['w_gate_up'],
                    lp['b_gate_up'], lp['w_down'], lp['b_down'])
    return x, k, v, S, p_last


def setup_inputs(seed: int = 0) -> dict:
    key = jax.random.key(seed)
    keys = list(jax.random.split(key, 40))
    f32 = jnp.float32
    nrm = lambda shape, scale: jax.random.normal(keys.pop(), shape, f32) * scale
    uni = lambda shape, lo, hi: jax.random.uniform(keys.pop(), shape, f32, minval=lo, maxval=hi)
    L = DEPTH
    return {
        "x_prompt": nrm((BATCH, SEQ, D_MODEL), 1.0),
        "x_sample": nrm((DEC_BATCH, DEC_SEQ, D_MODEL), 1.0),
        "cache_sb_k": nrm((L, DEC_BATCH, SB_HEADS, PAST_LEN, SB_HEAD_DIM), 1.0),
        "cache_sb_v": nrm((L, DEC_BATCH, SB_HEADS, PAST_LEN, SB_HEAD_DIM), 1.0),
        "state_rwkv": nrm((L, DEC_BATCH, RW_HEADS, RW_HEAD_DIM, RW_HEAD_DIM), 1.0),
        "state_shift": nrm((L, DEC_BATCH, 1, RW_PROJ), 1.0),
        "norm1_g": 1.0 + nrm((L, D_MODEL), 0.1),
        "w_in": nrm((L, D_MODEL, IN_WIDTH), D_MODEL ** -0.5),
        "rw_mu": uni((L, RW_PROJ), 0.0, 1.0),
        "rw_w0": uni((L, RW_WIDTH), -5.0, -1.0),
        "rw_w2": nrm((L, RW_DECAY_LORA, RW_WIDTH), 0.5 * RW_DECAY_LORA ** -0.5),
        "rw_a0": nrm((L, RW_WIDTH), 0.5),
        "rw_a2": nrm((L, RW_AAA_LORA, RW_WIDTH), RW_AAA_LORA ** -0.5),
        "rw_g2": nrm((L, RW_GATE_LORA, RW_WIDTH), RW_GATE_LORA ** -0.5),
        "rw_k_k": 0.85 + nrm((L, RW_WIDTH), 0.05),
        "rw_k_a": 1.0 + nrm((L, RW_WIDTH), 0.05),
        "rw_r_k": nrm((L, RW_HEADS, RW_HEAD_DIM), 0.1),
        "rw_ln_g": 1.0 + nrm((L, RW_WIDTH), 0.1),
        "rw_ln_b": nrm((L, RW_WIDTH), 0.01),
        "w_branch_a": nrm((L, SB_WIDTH, D_MODEL), SB_WIDTH ** -0.5),
        "w_branch_b": nrm((L, RW_WIDTH, D_MODEL), RW_WIDTH ** -0.5),
        "w_out": nrm((L, D_MODEL, D_MODEL), D_MODEL ** -0.5),
        "norm2_g": 1.0 + nrm((L, D_MODEL), 0.1),
        "w_router": nrm((L, D_MODEL, N_EXPERTS), D_MODEL ** -0.5),
        "b_router": nrm((L, N_EXPERTS), 0.01),
        "w_gate_up": nrm((L, N_EXPERTS, D_MODEL, 2 * D_EXPERT), D_MODEL ** -0.5),
        "b_gate_up": nrm((L, N_EXPERTS, 2 * D_EXPERT), 0.01),
        "w_down": nrm((L, N_EXPERTS, D_EXPERT, D_MODEL), D_EXPERT ** -0.5),
        "b_down": nrm((L, N_EXPERTS, D_MODEL), 0.01),
        "final_norm_g": 1.0 + nrm((D_MODEL,), 0.1),
    }


def reference(x_prompt, x_sample, cache_sb_k, cache_sb_v, state_rwkv, state_shift,
              norm1_g, w_in, rw_mu, rw_w0, rw_w2, rw_a0, rw_a2, rw_g2, rw_k_k, rw_k_a, rw_r_k,
              rw_ln_g, rw_ln_b, w_branch_a, w_branch_b, w_out, norm2_g, w_router, b_router,
              w_gate_up, b_gate_up, w_down, b_down, final_norm_g):
    yp, ys = x_prompt, x_sample
    kp_l, vp_l, Sp_l, shp_l, ks_l, vs_l, Ss_l, shs_l = [], [], [], [], [], [], [], []
    for l in range(DEPTH):
        lp = dict(norm1_g=norm1_g[l], w_in=w_in[l], rw_mu=rw_mu[l], rw_w0=rw_w0[l], rw_w2=rw_w2[l],
                  rw_a0=rw_a0[l], rw_a2=rw_a2[l], rw_g2=rw_g2[l], rw_k_k=rw_k_k[l], rw_k_a=rw_k_a[l],
                  rw_r_k=rw_r_k[l], rw_ln_g=rw_ln_g[l], rw_ln_b=rw_ln_b[l], w_branch_a=w_branch_a[l],
                  w_branch_b=w_branch_b[l], w_out=w_out[l], norm2_g=norm2_g[l], w_router=w_router[l],
                  b_router=b_router[l], w_gate_up=w_gate_up[l], b_gate_up=b_gate_up[l],
                  w_down=w_down[l], b_down=b_down[l])
        B = yp.shape[0]
        S0 = jnp.zeros((B, RW_HEADS, RW_HEAD_DIM, RW_HEAD_DIM), jnp.float32)
        prev0 = jnp.zeros((B, 1, RW_PROJ), yp.dtype)
        yp, kp, vp, Sp, shp = trunk_layer(yp, sb_prompt, S0, prev0, lp)
        ck, cv = cache_sb_k[l], cache_sb_v[l]
        ys, ks, vs, Ss, shs = trunk_layer(ys, lambda q, k, v: sb_sample(q, k, v, ck, cv),
                                          state_rwkv[l], state_shift[l], lp)
        kp_l.append(kp); vp_l.append(vp); Sp_l.append(Sp); shp_l.append(shp)
        ks_l.append(ks); vs_l.append(vs); Ss_l.append(Ss); shs_l.append(shs)
    y_prompt = rmsnorm(yp, final_norm_g)
    y_sample = rmsnorm(ys, final_norm_g)
    return (y_prompt, y_sample,
            jnp.stack(kp_l), jnp.stack(vp_l), jnp.stack(Sp_l), jnp.stack(shp_l),
            jnp.stack(ks_l), jnp.stack(vs_l), jnp.stack(Ss_l), jnp.stack(shs_l))
```

```python
import functools

import jax
import jax.numpy as jnp
from jax import lax
from jax.experimental import pallas as pl
from jax.experimental.pallas import tpu as pltpu

F32 = jnp.float32
BF16 = jnp.bfloat16
HIGHEST = lax.Precision.HIGHEST

SB_HEADS = 8
SB_HEAD_DIM = 64
RW_HEADS = 8
RW_HEAD_DIM = 64
RW_WIDTH = RW_HEADS * RW_HEAD_DIM
RW_DECAY_LORA = 64
RW_AAA_LORA = 64
RW_GATE_LORA = 128
RW_GN_EPS = 64e-5
N_EXPERTS = 32
TOP_K = 4
SWIGLU_LIMIT = 7.0
SWIGLU_ALPHA = 1.702
RMS_EPS = 1e-6

VMEM_LIMIT_BYTES = 56 * 1024 * 1024
ROW_TILE = 256
SB_BLOCK = 256
RW_CHUNK = 64
EXPERT_ROWS = 512
DISPATCH_TILE = 512
COMBINE_TILE = 128


def _params(semantics):
    return pltpu.CompilerParams(dimension_semantics=semantics, vmem_limit_bytes=VMEM_LIMIT_BYTES)


def _dot(a, b, precision=None):
    return jnp.dot(a, b, preferred_element_type=F32, precision=precision)


def _dot_nt(a, b, precision=None):
    return lax.dot_general(a, b, (((1,), (1,)), ((), ())), preferred_element_type=F32, precision=precision)


def _dot_tn(a, b, precision=None):
    return lax.dot_general(a, b, (((0,), (0,)), ((), ())), preferred_element_type=F32, precision=precision)


def _softplus_parts(z):
    l = jnp.log1p(jnp.exp(-jnp.abs(z)))
    sp = jnp.maximum(z, 0.0) + l
    return sp, sp - z


def _rmsnorm(x, g):
    return x * lax.rsqrt(jnp.mean(x * x, axis=-1, keepdims=True) + RMS_EPS) * g


def _inproj_body(x_ref, g_ref, w_ref, qkv_ref, rw_ref, gab_ref, *, col_chunk):
    h = _rmsnorm(x_ref[...], g_ref[...]).astype(BF16)
    off = 0
    for ref in (qkv_ref, rw_ref, gab_ref):
        width = ref.shape[-1]
        for c0 in range(0, width, col_chunk):
            cw = min(col_chunk, width - c0)
            ref[:, c0:c0 + cw] = _dot(h, w_ref[:, off + c0:off + c0 + cw])
        off += width


def _inproj(x, g, w_bf16, widths, tm):
    n, d = x.shape
    total = w_bf16.shape[1]
    return pl.pallas_call(
        functools.partial(_inproj_body, col_chunk=512),
        grid=(n // tm,),
        in_specs=[
            pl.BlockSpec((tm, d), lambda i: (i, 0)),
            pl.BlockSpec((1, d), lambda i: (0, 0)),
            pl.BlockSpec((d, total), lambda i: (0, 0), pipeline_mode=pl.Buffered(1)),
        ],
        out_specs=[pl.BlockSpec((tm, wd), lambda i: (i, 0)) for wd in widths],
        out_shape=[jax.ShapeDtypeStruct((n, wd), F32) for wd in widths],
        compiler_params=_params(("parallel",)),
        name="inproj",
    )(x, g.reshape(1, d), w_bf16)


def _sb_tile(q, kb, vb, tri, carry, mask):
    z = _dot_nt(q, kb)
    sp_pos, sp_neg = _softplus_parts(z)
    log_keep = -sp_pos
    if mask is not None:
        log_keep = jnp.where(mask, log_keep, 0.0)
    hi = log_keep.astype(BF16)
    lo = (log_keep - hi.astype(F32)).astype(BF16)
    later = _dot(hi, tri) + _dot(lo, tri) + carry
    wts = jnp.exp(later - sp_neg)
    if mask is not None:
        wts = jnp.where(mask, wts, 0.0)
    out = _dot(wts.astype(BF16), vb)
    return out, carry + jnp.sum(log_keep, axis=-1, keepdims=True)


def _sb_body(q_ref, k_ref, v_ref, triq_ref, trik_ref, o_ref, *, tq, tk, past):
    qi = pl.program_id(1)
    q = q_ref[0]
    d0 = pl.multiple_of(past + qi * tq, tq)
    row = lax.broadcasted_iota(jnp.int32, (tq, tq), 0)
    col = lax.broadcasted_iota(jnp.int32, (tq, tq), 1)
    acc, carry = _sb_tile(q, k_ref[0, pl.ds(d0, tq), :], v_ref[0, pl.ds(d0, tq), :], triq_ref[...],
                          jnp.zeros((tq, 1), F32), col < row)
    nb = (past + qi * tq) // tk

    def step(i, state):
        acc, carry = state
        s0 = pl.multiple_of((nb - 1 - i) * tk, tk)
        out, carry = _sb_tile(q, k_ref[0, pl.ds(s0, tk), :], v_ref[0, pl.ds(s0, tk), :], trik_ref[...],
                              carry, None)
        return acc + out, carry

    acc, _ = lax.fori_loop(0, nb, step, (acc, carry))
    o_ref[0] = acc.astype(o_ref.dtype)


def _later_matrix(n):
    r = lax.broadcasted_iota(jnp.int32, (n, n), 0)
    c = lax.broadcasted_iota(jnp.int32, (n, n), 1)
    return (r > c).astype(BF16)


def _stick_breaking(q, k, v, past, tq, tk):
    bh, t, dh = q.shape
    s = k.shape[1]
    return pl.pallas_call(
        functools.partial(_sb_body, tq=tq, tk=tk, past=past),
        grid=(bh, t // tq),
        in_specs=[
            pl.BlockSpec((1, tq, dh), lambda b, i: (b, i, 0)),
            pl.BlockSpec((1, s, dh), lambda b, i: (b, 0, 0)),
            pl.BlockSpec((1, s, dh), lambda b, i: (b, 0, 0)),
            pl.BlockSpec((tq, tq), lambda b, i: (0, 0)),
            pl.BlockSpec((tk, tk), lambda b, i: (0, 0)),
        ],
        out_specs=pl.BlockSpec((1, tq, dh), lambda b, i: (b, i, 0)),
        out_shape=jax.ShapeDtypeStruct((bh, t, dh), BF16),
        compiler_params=_params(("parallel", "arbitrary")),
        name="stick_breaking",
    )(q, k, v, _later_matrix(tq), _later_matrix(tk))


def _rwkv_body(p_ref, prev_ref, s0_ref, mu_ref, w0_ref, w2_ref, a0_ref, a2_ref, g2_ref, kk_ref, ka_ref,
               rk_ref, lng_ref, lnb_ref, tri_ref, bd_ref,
               o_ref, s_ref, shift_ref, prev_scr, y_scr, *, chunk, n_sq):
    c = pl.program_id(1)
    n = RW_HEAD_DIM
    w_ = RW_WIDTH

    @pl.when(c == 0)
    def _():
        prev_scr[...] = prev_ref[0]
        s_ref[...] = s0_ref[...]

    p = p_ref[...]
    rows = lax.broadcasted_iota(jnp.int32, p.shape, 0)
    p_shift = jnp.where(rows == 0, prev_scr[...], pltpu.roll(p, 1, 0))
    prev_scr[...] = p[chunk - 1:chunk, :]
    shift_ref[0] = p[chunk - 1:chunk, :]

    xs = p + (p_shift - p) * mu_ref[...]
    r = xs[:, 0:w_]
    k = xs[:, w_:2 * w_]
    v = xs[:, 2 * w_:3 * w_]
    o1 = 3 * w_
    dw = xs[:, o1:o1 + RW_DECAY_LORA]
    da = xs[:, o1 + RW_DECAY_LORA:o1 + RW_DECAY_LORA + RW_AAA_LORA]
    dg = xs[:, o1 + RW_DECAY_LORA + RW_AAA_LORA:]

    sp_neg = _softplus_parts(w0_ref[...] + _dot(jnp.tanh(dw), w2_ref[...], HIGHEST))[1]
    log_decay = -jnp.exp(-sp_neg - 0.5)
    a = jax.nn.sigmoid(a0_ref[...] + _dot(da, a2_ref[...], HIGHEST))
    g = _dot(jax.nn.sigmoid(dg), g2_ref[...], HIGHEST)

    bd = bd_ref[...]
    kk = k * kk_ref[...]
    kk = kk / jnp.maximum(jnp.sqrt(_dot(kk * kk, bd, HIGHEST)), 1e-12)
    k2 = k * (1.0 + (a - 1.0) * ka_ref[...])
    a_vec = -kk
    b_vec = kk * a
    bonus = _dot(r * k2 * rk_ref[...], bd, HIGHEST) * v

    cw = _dot(tri_ref[...], log_decay, HIGHEST)
    cw_last = cw[chunk - 1:chunk, :]
    r_t = r * jnp.exp(cw)
    a_t = a_vec * jnp.exp(cw - log_decay)
    e_neg = jnp.exp(-cw)
    b_t = b_vec * e_neg
    k_t = k2 * e_neg
    e_rel = jnp.exp(cw_last - cw)
    b_h = b_vec * e_rel
    k_h = k2 * e_rel
    w_c = jnp.exp(cw_last)

    rr = lax.broadcasted_iota(jnp.int32, (chunk, chunk), 0)
    cc = lax.broadcasted_iota(jnp.int32, (chunk, chunk), 1)
    strict = cc < rr
    incl = cc <= rr
    eye = (cc == rr).astype(F32)

    for h in range(RW_HEADS):
        sl = slice(h * n, (h + 1) * n)
        lhs2 = jnp.concatenate([a_t[:, sl], r_t[:, sl]], axis=0)
        rhs2 = jnp.concatenate([b_t[:, sl], k_t[:, sl]], axis=0)
        gram = _dot_nt(lhs2, rhs2, HIGHEST)
        l_ab = jnp.where(strict, gram[:chunk, :chunk], 0.0)
        l_ak = jnp.where(strict, gram[:chunk, chunk:], 0.0)
        m_rb = jnp.where(incl, gram[chunk:, :chunk], 0.0)
        m_rk = jnp.where(incl, gram[chunk:, chunk:], 0.0)
        s_h = s_ref[0, h]
        v_h = v[:, sl]
        a_s = _dot_nt(lhs2, s_h, HIGHEST)
        x0 = a_s[:chunk] + _dot(l_ak, v_h, HIGHEST)
        t_inv = eye + l_ab
        l_pow = _dot(l_ab, l_ab, HIGHEST)
        for i in range(n_sq):
            prod = _dot(l_pow, jnp.concatenate([t_inv, l_pow], axis=1), HIGHEST)
            t_inv = t_inv + prod[:, :chunk]
            if i + 1 < n_sq:
                l_pow = prod[:, chunk:]
        u = _dot(t_inv, x0, HIGHEST)
        uv = jnp.concatenate([u, v_h], axis=0)
        y_scr[:, sl] = a_s[chunk:] + _dot(jnp.concatenate([m_rb, m_rk], axis=1), uv, HIGHEST)
        bk = jnp.concatenate([b_h[:, sl], k_h[:, sl]], axis=0)
        s_ref[0, h] = s_h * w_c[:, sl] + _dot_tn(uv, bk, HIGHEST)

    y = y_scr[...]
    mean = _dot(y, bd, HIGHEST) * (1.0 / n)
    yc = y - mean
    var = _dot(yc * yc, bd, HIGHEST) * (1.0 / n)
    yn = yc * lax.rsqrt(var + RW_GN_EPS) * lng_ref[...] + lnb_ref[...]
    o_ref[...] = ((yn + bonus) * g).astype(o_ref.dtype)


def _rwkv(rw_all, row_off, batch, t, prev, s0, lp, o_prev):
    n_all, proj = rw_all.shape
    chunk = min(RW_CHUNK, t)
    n_sq = chunk.bit_length() - 2
    nc = t // chunk
    blk0 = row_off // chunk
    row = lambda x: x.reshape(1, -1).astype(F32)
    r = lax.broadcasted_iota(jnp.int32, (chunk, chunk), 0)
    cidx = lax.broadcasted_iota(jnp.int32, (chunk, chunk), 1)
    tri = (cidx <= r).astype(F32)
    hr = lax.broadcasted_iota(jnp.int32, (RW_WIDTH, RW_WIDTH), 0) // RW_HEAD_DIM
    hc = lax.broadcasted_iota(jnp.int32, (RW_WIDTH, RW_WIDTH), 1) // RW_HEAD_DIM
    bd = (hr == hc).astype(F32)
    const = lambda shape: pl.BlockSpec(shape, lambda b, c: (0,) * len(shape))
    args = [rw_all, prev.astype(F32), s0.astype(F32), row(lp['rw_mu']), row(lp['rw_w0']), lp['rw_w2'].astype(F32),
            row(lp['rw_a0']), lp['rw_a2'].astype(F32), lp['rw_g2'].astype(F32), row(lp['rw_k_k']),
            row(lp['rw_k_a']), row(lp['rw_r_k']), row(lp['rw_ln_g']), row(lp['rw_ln_b']), tri, bd]
    in_specs = [
        pl.BlockSpec((chunk, proj), lambda b, c: (blk0 + b * nc + c, 0)),
        pl.BlockSpec((1, 1, proj), lambda b, c: (b, 0, 0)),
        pl.BlockSpec((1, RW_HEADS, RW_HEAD_DIM, RW_HEAD_DIM), lambda b, c: (b, 0, 0, 0)),
    ] + [const(a.shape) for a in args[3:]]
    out_specs = [
        pl.BlockSpec((chunk, RW_WIDTH), lambda b, c: (blk0 + b * nc + c, 0)),
        pl.BlockSpec((1, RW_HEADS, RW_HEAD_DIM, RW_HEAD_DIM), lambda b, c: (b, 0, 0, 0)),
        pl.BlockSpec((1, 1, proj), lambda b, c: (b, 0, 0)),
    ]
    out_shape = [
        jax.ShapeDtypeStruct((n_all, RW_WIDTH), BF16),
        jax.ShapeDtypeStruct((batch, RW_HEADS, RW_HEAD_DIM, RW_HEAD_DIM), F32),
        jax.ShapeDtypeStruct((batch, 1, proj), F32),
    ]
    n_in = len(args)
    args.append(o_prev)
    in_specs.append(pl.BlockSpec(memory_space=pl.ANY))

    def body(*refs):
        _rwkv_body(*refs[:n_in], *refs[n_in + 1:], chunk=chunk, n_sq=n_sq)

    return pl.pallas_call(
        body,
        grid=(batch, nc),
        in_specs=in_specs,
        out_specs=out_specs,
        out_shape=out_shape,
        scratch_shapes=[pltpu.VMEM((1, proj), F32), pltpu.VMEM((chunk, RW_WIDTH), F32)],
        input_output_aliases={n_in: 0},
        compiler_params=_params(("parallel", "arbitrary")),
        name="rwkv7_chunked",
    )(*args)


def _merge_body(oa_ref, ob_ref, gab_ref, x_ref, wa_ref, wb_ref, wo_ref, g2_ref, wr_ref, br_ref, tri_ref,
                x2_ref, h2_ref, idx_ref, gate_ref, rank_ref, cnt_ref, carry_scr):
    i = pl.program_id(0)
    d = x_ref.shape[-1]

    @pl.when(i == 0)
    def _():
        carry_scr[...] = jnp.zeros_like(carry_scr)

    gab = gab_ref[...]
    merged = (jax.nn.sigmoid(gab[:, :d]) * _dot(oa_ref[...], wa_ref[...])
              + jax.nn.sigmoid(gab[:, d:]) * _dot(ob_ref[...], wb_ref[...]))
    x2 = x_ref[...] + _dot(merged.astype(BF16), wo_ref[...])
    x2_ref[...] = x2
    h2 = _rmsnorm(x2, g2_ref[...])
    h2_ref[...] = h2
    logits = _dot(h2, wr_ref[...], HIGHEST) + br_ref[...]

    tm, ne = logits.shape
    col = lax.broadcasted_iota(jnp.int32, (tm, ne), 1)
    c4 = lax.broadcasted_iota(jnp.int32, (tm, TOP_K), 1)
    work = logits
    tops, idxs = [], []
    for _ in range(TOP_K):
        m = jnp.max(work, axis=-1, keepdims=True)
        ix = jnp.min(jnp.where(work == m, col, ne), axis=-1, keepdims=True)
        tops.append(m)
        idxs.append(ix)
        work = jnp.where(col == ix, -jnp.inf, work)
    es = [jnp.exp(m - tops[0]) for m in tops]
    denom = es[0] + es[1] + es[2] + es[3]
    onehot = jnp.zeros((tm, ne), F32)
    for ix in idxs:
        onehot = onehot + (col == ix).astype(F32)
    before = carry_scr[...] + _dot(tri_ref[...], onehot.astype(BF16))
    idx_out = jnp.zeros((tm, TOP_K), jnp.int32)
    gate_out = jnp.zeros((tm, TOP_K), F32)
    rank_out = jnp.zeros((tm, TOP_K), jnp.int32)
    for kk in range(TOP_K):
        rk = jnp.sum(jnp.where(col == idxs[kk], before, 0.0), axis=-1, keepdims=True)
        idx_out = jnp.where(c4 == kk, idxs[kk], idx_out)
        gate_out = jnp.where(c4 == kk, es[kk] / denom, gate_out)
        rank_out = jnp.where(c4 == kk, rk.astype(jnp.int32), rank_out)
    idx_ref[...] = idx_out
    gate_ref[...] = gate_out
    rank_ref[...] = rank_out
    carry = carry_scr[...] + jnp.sum(onehot, axis=0, keepdims=True)
    carry_scr[...] = carry
    cnt_ref[...] = carry.astype(jnp.int32)


def _merge_route(oa, ob, gab, x, lp, tm):
    n, d = x.shape
    ne = N_EXPERTS
    r = lax.broadcasted_iota(jnp.int32, (tm, tm), 0)
    c = lax.broadcasted_iota(jnp.int32, (tm, tm), 1)
    tri = (c < r).astype(BF16)
    const = lambda shape: pl.BlockSpec(shape, lambda i: (0,) * len(shape))
    rowblk = lambda wd: pl.BlockSpec((tm, wd), lambda i: (i, 0))
    wa = lp['w_branch_a'].astype(BF16)
    wb = lp['w_branch_b'].astype(BF16)
    wo = lp['w_out'].astype(BF16)
    return pl.pallas_call(
        _merge_body,
        grid=(n // tm,),
        in_specs=[rowblk(oa.shape[1]), rowblk(ob.shape[1]), rowblk(2 * d), rowblk(d),
                  const(wa.shape), const(wb.shape), const(wo.shape), const((1, d)), const((d, ne)),
                  const((1, ne)), const((tm, tm))],
        out_specs=[rowblk(d), rowblk(d), rowblk(TOP_K), rowblk(TOP_K), rowblk(TOP_K), const((1, ne))],
        out_shape=[jax.ShapeDtypeStruct((n, d), F32), jax.ShapeDtypeStruct((n, d), F32),
                   jax.ShapeDtypeStruct((n, TOP_K), jnp.int32), jax.ShapeDtypeStruct((n, TOP_K), F32),
                   jax.ShapeDtypeStruct((n, TOP_K), jnp.int32), jax.ShapeDtypeStruct((1, ne), jnp.int32)],
        scratch_shapes=[pltpu.VMEM((1, ne), F32)],
        compiler_params=_params(("arbitrary",)),
        name="merge_route",
    )(oa, ob, gab, x, wa, wb, wo, lp['norm2_g'].reshape(1, d).astype(F32), lp['w_router'].astype(F32),
      lp['b_router'].reshape(1, ne).astype(F32), tri)


def _dispatch_body(dest_ref, h_ref, xs_in_ref, xs_ref, sem, *, tile):
    del xs_in_ref
    base = pl.program_id(0) * tile

    def copy(t, kk):
        return pltpu.make_async_copy(h_ref.at[pl.ds(base + t, 1)],
                                     xs_ref.at[pl.ds(dest_ref[0, 0, t * TOP_K + kk], 1)], sem)

    def issue(t, carry):
        for kk in range(TOP_K):
            copy(t, kk).start()
        return carry

    def drain(t, carry):
        for kk in range(TOP_K):
            copy(t, kk).wait()
        return carry

    lax.fori_loop(0, tile, issue, 0)
    lax.fori_loop(0, tile, drain, 0)


def _dispatch(h2, dest, n_rows, tile):
    n, d = h2.shape
    nt = n // tile
    xs0 = jnp.zeros((n_rows, d), F32)
    return pl.pallas_call(
        functools.partial(_dispatch_body, tile=tile),
        grid=(nt,),
        in_specs=[pl.BlockSpec((1, 1, tile * TOP_K), lambda i: (i, 0, 0), memory_space=pltpu.SMEM),
                  pl.BlockSpec(memory_space=pl.ANY), pl.BlockSpec(memory_space=pl.ANY)],
        out_specs=pl.BlockSpec(memory_space=pl.ANY),
        out_shape=jax.ShapeDtypeStruct((n_rows, d), F32),
        scratch_shapes=[pltpu.SemaphoreType.DMA],
        input_output_aliases={2: 0},
        compiler_params=_params(("arbitrary",)),
        name="moe_dispatch",
    )(dest.reshape(nt, 1, tile * TOP_K), h2, xs0)


def _expert_body(be_ref, used_ref, xs_ref, wgu_ref, bgu_ref, wd_ref, bd_ref, ys_ref):
    del be_ref
    i = pl.program_id(0)
    de = wd_ref.shape[1]

    @pl.when(i < used_ref[0])
    def _():
        gu = _dot(xs_ref[...].astype(BF16), wgu_ref[0]) + bgu_ref[0]
        g = jnp.minimum(gu[:, :de], SWIGLU_LIMIT)
        u = jnp.clip(gu[:, de:], -SWIGLU_LIMIT, SWIGLU_LIMIT)
        act = (u + 1.0) * (g * jax.nn.sigmoid(g * SWIGLU_ALPHA))
        ys_ref[...] = _dot(act.astype(BF16), wd_ref[0]) + bd_ref[0]

    @pl.when(i >= used_ref[0])
    def _():
        ys_ref[...] = jnp.zeros_like(ys_ref)


def _experts(xs, block_expert, n_used, wgu, bgu, wd, bd, bm):
    n_rows, d = xs.shape
    ne, _, de2 = wgu.shape
    nb = n_rows // bm
    grid_spec = pltpu.PrefetchScalarGridSpec(
        num_scalar_prefetch=2,
        grid=(nb,),
        in_specs=[
            pl.BlockSpec((bm, d), lambda i, be, nu: (i, 0)),
            pl.BlockSpec((1, d, de2), lambda i, be, nu: (be[i], 0, 0)),
            pl.BlockSpec((1, 1, de2), lambda i, be, nu: (be[i], 0, 0)),
            pl.BlockSpec((1, de2 // 2, d), lambda i, be, nu: (be[i], 0, 0)),
            pl.BlockSpec((1, 1, d), lambda i, be, nu: (be[i], 0, 0)),
        ],
        out_specs=pl.BlockSpec((bm, d), lambda i, be, nu: (i, 0)),
    )
    return pl.pallas_call(
        _expert_body,
        grid_spec=grid_spec,
        out_shape=jax.ShapeDtypeStruct((n_rows, d), F32),
        compiler_params=_params(("arbitrary",)),
        name="moe_experts",
    )(block_expert, n_used, xs, wgu, bgu.reshape(ne, 1, de2), wd, bd.reshape(ne, 1, d))


def _combine_body(dest_ref, ys_ref, gate_ref, x2_ref, g_ref, y_ref, buf, sem, *, tile):
    def copy(t, kk):
        return pltpu.make_async_copy(ys_ref.at[pl.ds(dest_ref[0, 0, t * TOP_K + kk], 1)],
                                     buf.at[kk, pl.ds(t, 1)], sem)

    def issue(t, carry):
        for kk in range(TOP_K):
            copy(t, kk).start()
        return carry

    def drain(t, carry):
        for kk in range(TOP_K):
            copy(t, kk).wait()
        return carry

    lax.fori_loop(0, tile, issue, 0)
    lax.fori_loop(0, tile, drain, 0)
    gate = gate_ref[...]
    moe = gate[:, 0:1] * buf[0]
    for kk in range(1, TOP_K):
        moe = moe + gate[:, kk:kk + 1] * buf[kk]
    y_ref[...] = _rmsnorm(x2_ref[...] + moe, g_ref[...])


def _combine(ys, dest, gate, x2, final_g, tile):
    n, d = x2.shape
    nt = n // tile
    return pl.pallas_call(
        functools.partial(_combine_body, tile=tile),
        grid=(nt,),
        in_specs=[pl.BlockSpec((1, 1, tile * TOP_K), lambda i: (i, 0, 0), memory_space=pltpu.SMEM),
                  pl.BlockSpec(memory_space=pl.ANY),
                  pl.BlockSpec((tile, TOP_K), lambda i: (i, 0)),
                  pl.BlockSpec((tile, d), lambda i: (i, 0)),
                  pl.BlockSpec((1, d), lambda i: (0, 0))],
        out_specs=pl.BlockSpec((tile, d), lambda i: (i, 0)),
        out_shape=jax.ShapeDtypeStruct((n, d), F32),
        scratch_shapes=[pltpu.VMEM((TOP_K, tile, d), F32), pltpu.SemaphoreType.DMA],
        compiler_params=_params(("arbitrary",)),
        name="moe_combine",
    )(dest.reshape(nt, 1, tile * TOP_K), ys, gate, x2, final_g.reshape(1, d).astype(F32))


def _moe(h2, x2, idx, gate, rank, counts, lp, final_g):
    n, d = h2.shape
    bm = EXPERT_ROWS
    counts = counts.reshape(N_EXPERTS)
    padded = (counts + bm - 1) // bm * bm
    pad_end = jnp.cumsum(padded)
    dest = ((pad_end - padded)[idx] + rank).reshape(n * TOP_K).astype(jnp.int32)
    n_blocks = -(-(n * TOP_K) // bm) + N_EXPERTS
    block_expert = jnp.minimum(
        jnp.searchsorted(pad_end, jnp.arange(n_blocks, dtype=jnp.int32) * bm, side='right'),
        N_EXPERTS - 1).astype(jnp.int32)
    n_used = (pad_end[-1:] // bm).astype(jnp.int32)
    xs = _dispatch(h2, dest, n_blocks * bm, DISPATCH_TILE)
    ys = _experts(xs, block_expert, n_used, lp['w_gate_up'].astype(BF16), lp['b_gate_up'].astype(F32),
                  lp['w_down'].astype(BF16), lp['b_down'].astype(F32), bm)
    return _combine(ys, dest, gate, x2, final_g, COMBINE_TILE)


def _heads(t, b, s):
    return t.reshape(b, s, SB_HEADS, SB_HEAD_DIM).transpose(0, 2, 1, 3).reshape(b * SB_HEADS, s, SB_HEAD_DIM)


def _unheads(t, b, s):
    return t.reshape(b, SB_HEADS, s, SB_HEAD_DIM).transpose(0, 2, 1, 3).reshape(b * s, SB_HEADS * SB_HEAD_DIM)


def _layer(x_all, bp, tp, bs, ts, cache_k, cache_v, state_rwkv, state_shift, lp, final_g):
    n, d = x_all.shape
    sbw = SB_HEADS * SB_HEAD_DIM
    rw_proj = lp['rw_mu'].shape[0]
    n_p = bp * tp
    qkv, rw, gab = _inproj(x_all, lp['norm1_g'].astype(F32), lp['w_in'].astype(BF16),
                           (3 * sbw, rw_proj, 2 * d), ROW_TILE)
    scale = SB_HEAD_DIM ** -0.5

    def split_heads(rows, b, s):
        q = _heads(rows[:, :sbw], b, s)
        k = _heads(rows[:, sbw:2 * sbw], b, s)
        v = _heads(rows[:, 2 * sbw:], b, s)
        return q, k, v

    qp, kp, vp = split_heads(qkv[:n_p], bp, tp)
    oa_p = _stick_breaking((qp * scale).astype(BF16), kp.astype(BF16), vp.astype(BF16), 0, SB_BLOCK, SB_BLOCK)
    qs, ks, vs = split_heads(qkv[n_p:], bs, ts)
    past = cache_k.shape[2]
    k_all = jnp.concatenate([cache_k.reshape(bs * SB_HEADS, past, SB_HEAD_DIM), ks], axis=1)
    v_all = jnp.concatenate([cache_v.reshape(bs * SB_HEADS, past, SB_HEAD_DIM), vs], axis=1)
    oa_s = _stick_breaking((qs * scale).astype(BF16), k_all.astype(BF16), v_all.astype(BF16), past, ts,
                           min(SB_BLOCK, past))
    oa = jnp.concatenate([_unheads(oa_p, bp, tp), _unheads(oa_s, bs, ts)], axis=0)

    s0_p = jnp.zeros((bp, RW_HEADS, RW_HEAD_DIM, RW_HEAD_DIM), F32)
    prev_p = jnp.zeros((bp, 1, rw_proj), F32)
    ob, st_p, sh_p = _rwkv(rw, 0, bp, tp, prev_p, s0_p, lp, jnp.zeros((n, RW_WIDTH), BF16))
    ob, st_s, sh_s = _rwkv(rw, n_p, bs, ts, state_shift, state_rwkv, lp, ob)

    x2, h2, idx, gate, rank, counts = _merge_route(oa, ob, gab, x_all, lp, ROW_TILE)
    y = _moe(h2, x2, idx, gate, rank, counts, lp, final_g)
    hm = lambda t, b, s: t.reshape(b, SB_HEADS, s, SB_HEAD_DIM)
    return (y, hm(kp, bp, tp), hm(vp, bp, tp), st_p, sh_p, hm(ks, bs, ts), hm(vs, bs, ts), st_s, sh_s)


def kernel(x_prompt, x_sample, cache_sb_k, cache_sb_v, state_rwkv, state_shift, norm1_g, w_in, rw_mu, rw_w0, rw_w2, rw_a0, rw_a2, rw_g2, rw_k_k, rw_k_a, rw_r_k, rw_ln_g, rw_ln_b, w_branch_a, w_branch_b, w_out, norm2_g, w_router, b_router, w_gate_up, b_gate_up, w_down, b_down, final_norm_g):
    depth = w_in.shape[0]
    assert depth == 1, "the final RMSNorm is fused into the last (only) layer"
    bp, tp, d = x_prompt.shape
    bs, ts, _ = x_sample.shape
    n_p = bp * tp
    x_all = jnp.concatenate([x_prompt.reshape(n_p, d), x_sample.reshape(bs * ts, d)], axis=0)
    lp = dict(norm1_g=norm1_g[0], w_in=w_in[0], rw_mu=rw_mu[0], rw_w0=rw_w0[0], rw_w2=rw_w2[0], rw_a0=rw_a0[0],
              rw_a2=rw_a2[0], rw_g2=rw_g2[0], rw_k_k=rw_k_k[0], rw_k_a=rw_k_a[0], rw_r_k=rw_r_k[0].reshape(-1),
              rw_ln_g=rw_ln_g[0], rw_ln_b=rw_ln_b[0], w_branch_a=w_branch_a[0], w_branch_b=w_branch_b[0],
              w_out=w_out[0], norm2_g=norm2_g[0], w_router=w_router[0], b_router=b_router[0],
              w_gate_up=w_gate_up[0], b_gate_up=b_gate_up[0], w_down=w_down[0], b_down=b_down[0])
    y, kp, vp, st_p, sh_p, ks, vs, st_s, sh_s = _layer(
        x_all, bp, tp, bs, ts, cache_sb_k[0], cache_sb_v[0], state_rwkv[0], state_shift[0], lp, final_norm_g)
    return (y[:n_p].reshape(bp, tp, d), y[n_p:].reshape(bs, ts, d),
            kp[None], vp[None], st_p[None], sh_p[None], ks[None], vs[None], st_s[None], sh_s[None])
```

```python
import functools

import jax
import jax.numpy as jnp
from jax import lax
from jax.experimental import pallas as pl
from jax.experimental.pallas import tpu as pltpu

F32 = jnp.float32
BF16 = jnp.bfloat16
HIGHEST = lax.Precision.HIGHEST

SB_HEADS = 8
SB_HEAD_DIM = 64
RW_HEADS = 8
RW_HEAD_DIM = 64
RW_WIDTH = RW_HEADS * RW_HEAD_DIM
RW_DECAY_LORA = 64
RW_AAA_LORA = 64
RW_GATE_LORA = 128
RW_GN_EPS = 64e-5
N_EXPERTS = 32
TOP_K = 4
SWIGLU_LIMIT = 7.0
SWIGLU_ALPHA = 1.702
RMS_EPS = 1e-6

VMEM_LIMIT_BYTES = 56 * 1024 * 1024
ROW_TILE = 256
SB_BLOCK = 512
SB_SUB = 256
LOG2E = 1.4426950408889634
SB_DEAD_LOG2 = -200.0
RW_CHUNK = 64
RW_PREP_CHUNKS = 2
RW_SCAN_CHUNKS = 4
EXPERT_ROWS = 512
DISPATCH_TILE = 512
COMBINE_TILE = 128


def _params(semantics):
    return pltpu.CompilerParams(dimension_semantics=semantics, vmem_limit_bytes=VMEM_LIMIT_BYTES)


def _dot(a, b, precision=None):
    return jnp.dot(a, b, preferred_element_type=F32, precision=precision)


def _dot_nt(a, b, precision=None):
    return lax.dot_general(a, b, (((1,), (1,)), ((), ())), preferred_element_type=F32, precision=precision)


def _dot_tn(a, b, precision=None):
    return lax.dot_general(a, b, (((0,), (0,)), ((), ())), preferred_element_type=F32, precision=precision)


def _softplus_parts(z):
    l = jnp.log1p(jnp.exp(-jnp.abs(z)))
    sp = jnp.maximum(z, 0.0) + l
    return sp, sp - z


def _rmsnorm(x, g):
    return x * lax.rsqrt(jnp.mean(x * x, axis=-1, keepdims=True) + RMS_EPS) * g


def _inproj_body(x_ref, g_ref, w_ref, qkv_ref, rw_ref, gab_ref, *, col_chunk):
    h = _rmsnorm(x_ref[...], g_ref[...]).astype(BF16)
    off = 0
    for ref in (qkv_ref, rw_ref, gab_ref):
        width = ref.shape[-1]
        for c0 in range(0, width, col_chunk):
            cw = min(col_chunk, width - c0)
            ref[:, c0:c0 + cw] = _dot(h, w_ref[:, off + c0:off + c0 + cw])
        off += width


def _inproj(x, g, w_bf16, widths, tm):
    n, d = x.shape
    total = w_bf16.shape[1]
    return pl.pallas_call(
        functools.partial(_inproj_body, col_chunk=512),
        grid=(n // tm,),
        in_specs=[
            pl.BlockSpec((tm, d), lambda i: (i, 0)),
            pl.BlockSpec((1, d), lambda i: (0, 0)),
            pl.BlockSpec((d, total), lambda i: (0, 0), pipeline_mode=pl.Buffered(1)),
        ],
        out_specs=[pl.BlockSpec((tm, wd), lambda i: (i, 0)) for wd in widths],
        out_shape=[jax.ShapeDtypeStruct((n, wd), F32) for wd in widths],
        compiler_params=_params(("parallel",)),
        name="inproj",
    )(x, g.reshape(1, d), w_bf16)


def _sb_tile(q, kb, vb, ntri, carry, mask, sb):
    tk = kb.shape[0]
    z = _dot_nt(q, kb)
    neg_abs = lax.bitcast_convert_type(lax.bitcast_convert_type(z, jnp.uint32) | jnp.uint32(0x80000000), F32)
    sp = jnp.maximum(z, 0.0) + jnp.log(1.0 + jnp.exp2(neg_abs)) * LOG2E
    if mask is not None:
        sp = jnp.where(mask, sp, 0.0)
    hi = sp.astype(BF16)
    lo = (sp - hi.astype(F32)).astype(BF16)
    log_beta = z - sp
    wts = []
    for j in range(tk // sb - 1, -1, -1):
        cols = slice(j * sb, (j + 1) * sb)
        later = _dot(hi[:, cols], ntri) + _dot(lo[:, cols], ntri) + carry
        w = jnp.exp2(log_beta[:, cols] + later)
        if mask is not None:
            w = jnp.where(mask[:, cols], w, 0.0)
        wts.append(w.astype(BF16))
        carry = carry - jnp.sum(sp[:, cols], axis=-1, keepdims=True)
    wts = wts[0] if len(wts) == 1 else jnp.concatenate(wts[::-1], axis=1)
    return _dot(wts, vb), carry


def _sb_body(q_ref, k_ref, v_ref, triq_ref, trik_ref, o_ref, *, tq, tk, past):
    qi = pl.program_id(2)
    q = q_ref[0]
    first = lax.broadcasted_iota(jnp.int32, q.shape, 1) < SB_HEAD_DIM
    zero = jnp.zeros_like(q)
    qs = (jnp.where(first, q, zero), jnp.where(first, zero, q))
    d0 = pl.multiple_of(past + qi * tq, tq)
    row = lax.broadcasted_iota(jnp.int32, (tq, tq), 0)
    col = lax.broadcasted_iota(jnp.int32, (tq, tq), 1)
    kd = k_ref[0, pl.ds(d0, tq), :]
    vd = v_ref[0, pl.ds(d0, tq), :]
    sbq = triq_ref.shape[0]
    state = []
    for qh in qs:
        state.extend(_sb_tile(qh, kd, vd, triq_ref[...], jnp.zeros((tq, 1), F32), col < row, sbq))
    nb = (past + qi * tq) // tk
    sbk = trik_ref.shape[0]

    def live(state):
        return jnp.max(jnp.maximum(state[1], state[3])) > SB_DEAD_LOG2

    def cond(loop):
        i, alive, _ = loop
        return jnp.logical_and(i < nb, alive)

    def step(loop):
        i, _, state = loop
        s0 = pl.multiple_of((nb - 1 - i) * tk, tk)
        kb = k_ref[0, pl.ds(s0, tk), :]
        vb = v_ref[0, pl.ds(s0, tk), :]
        new = []
        for h, qh in enumerate(qs):
            out, carry = _sb_tile(qh, kb, vb, trik_ref[...], state[2 * h + 1], None, sbk)
            new.extend((state[2 * h] + out, carry))
        return i + 1, live(new), tuple(new)

    _, _, state = lax.while_loop(cond, step, (jnp.int32(0), live(state), tuple(state)))
    o_ref[0] = jnp.where(first, state[0], state[2]).astype(o_ref.dtype)


def _later_matrix(n):
    r = lax.broadcasted_iota(jnp.int32, (n, n), 0)
    c = lax.broadcasted_iota(jnp.int32, (n, n), 1)
    return jnp.where(r > c, -1.0, 0.0).astype(BF16)


def _stick_breaking(q, k, v, past, tq, tk):
    b, t, w = q.shape
    s = k.shape[1]
    pair = 2 * SB_HEAD_DIM
    sbq = min(tq, SB_SUB)
    sbk = min(tk, SB_SUB)
    return pl.pallas_call(
        functools.partial(_sb_body, tq=tq, tk=tk, past=past),
        grid=(b, w // pair, t // tq),
        in_specs=[
            pl.BlockSpec((1, tq, pair), lambda b, h, i: (b, i, h)),
            pl.BlockSpec((1, s, pair), lambda b, h, i: (b, 0, h)),
            pl.BlockSpec((1, s, pair), lambda b, h, i: (b, 0, h)),
            pl.BlockSpec((sbq, sbq), lambda b, h, i: (0, 0)),
            pl.BlockSpec((sbk, sbk), lambda b, h, i: (0, 0)),
        ],
        out_specs=pl.BlockSpec((1, tq, pair), lambda b, h, i: (b, i, h)),
        out_shape=jax.ShapeDtypeStruct((b, t, w), BF16),
        compiler_params=_params(("parallel", "parallel", "arbitrary")),
        name="stick_breaking",
    )(q, k, v, _later_matrix(sbq), _later_matrix(sbk))


def _split_bf16(x):
    hi = x.astype(BF16)
    return hi, (x - hi.astype(F32)).astype(BF16)


def _dot3(a, b, dot=_dot):
    ah, al = _split_bf16(a)
    bh, bl = _split_bf16(b)
    return dot(ah, bh) + dot(ah, bl) + dot(al, bh)


def _dot_ones_rhs(x, ones_bf16):
    hi, lo = _split_bf16(x)
    return _dot(hi, ones_bf16) + _dot(lo, ones_bf16)


def _dot_ones_lhs(ones_bf16, x):
    hi, lo = _split_bf16(x)
    return _dot(ones_bf16, hi) + _dot(ones_bf16, lo)


def _rwkv_prep_body(p_ref, pb_ref, prev_ref, mu_ref, w0_ref, w2_ref, a0_ref, a2_ref, g2_ref, kk_ref, ka_ref,
                    rk_ref, tri_ref, bd_ref,
                    rhat_ref, y0_ref, a_ref, d_ref, bonus_ref, g_ref, *, chunk, groups, n_sq):
    n = RW_HEAD_DIM
    w_ = RW_WIDTH
    p = p_ref[...]
    rows = lax.broadcasted_iota(jnp.int32, p.shape, 0)
    prev_row = jnp.where(pl.program_id(1) == 0, prev_ref[0], pb_ref[7:8, :])
    p_shift = jnp.where(rows == 0, prev_row, pltpu.roll(p, 1, 0))

    xs = p + (p_shift - p) * mu_ref[...]
    r = xs[:, 0:w_]
    k = xs[:, w_:2 * w_]
    v = xs[:, 2 * w_:3 * w_]
    o1 = 3 * w_
    dw = xs[:, o1:o1 + RW_DECAY_LORA]
    da = xs[:, o1 + RW_DECAY_LORA:o1 + RW_DECAY_LORA + RW_AAA_LORA]
    dg = xs[:, o1 + RW_DECAY_LORA + RW_AAA_LORA:]

    sp_neg = _softplus_parts(w0_ref[...] + _dot3(jnp.tanh(dw), w2_ref[...]))[1]
    log_decay = -jnp.exp(-sp_neg - 0.5)
    a = jax.nn.sigmoid(a0_ref[...] + _dot3(da, a2_ref[...]))
    g_ref[...] = _dot(jax.nn.sigmoid(dg).astype(BF16), g2_ref[...])

    bd = bd_ref[...]
    kk = k * kk_ref[...]
    kk = kk / jnp.maximum(jnp.sqrt(_dot_ones_rhs(kk * kk, bd)), 1e-12)
    k2 = k * (1.0 + (a - 1.0) * ka_ref[...])
    a_vec = -kk
    b_vec = kk * a
    bonus_ref[...] = _dot_ones_rhs(r * k2 * rk_ref[...], bd) * v

    cw = _dot_ones_lhs(tri_ref[...], log_decay)
    cw_last = jnp.concatenate(
        [jnp.broadcast_to(cw[(j + 1) * chunk - 1:(j + 1) * chunk, :], (chunk, w_)) for j in range(groups)], axis=0)
    r_t = r * jnp.exp(cw)
    a_t = a_vec * jnp.exp(cw - log_decay)
    e_neg = jnp.exp(-cw)
    b_t = b_vec * e_neg
    k_t = k2 * e_neg
    e_rel = jnp.exp(cw_last - cw)
    b_h = b_vec * e_rel
    k_h = k2 * e_rel
    w_c = jnp.exp(cw_last)

    rr = lax.broadcasted_iota(jnp.int32, (chunk, chunk), 0)
    cc = lax.broadcasted_iota(jnp.int32, (chunk, chunk), 1)
    strict = cc < rr
    incl = cc <= rr
    eye_c = (cc == rr).astype(F32)
    rn = lax.broadcasted_iota(jnp.int32, (n, n), 0)
    cn = lax.broadcasted_iota(jnp.int32, (n, n), 1)
    eye_n = (rn == cn).astype(F32)
    zero = jnp.zeros((chunk, n), F32)

    for j in range(groups):
        rs = slice(j * chunk, (j + 1) * chunk)
        for h in range(RW_HEADS):
            sl = slice(h * n, (h + 1) * n)
            a_th = a_t[rs, sl]
            r_th = r_t[rs, sl]
            v_h = v[rs, sl]
            gram = _dot3(jnp.concatenate([a_th, r_th], axis=0),
                         jnp.concatenate([b_t[rs, sl], k_t[rs, sl]], axis=0), _dot_nt)
            l_ab = jnp.where(strict, gram[:chunk, :chunk], 0.0)
            l_ak = jnp.where(strict, gram[:chunk, chunk:], 0.0)
            m_rb = jnp.where(incl, gram[chunk:, :chunk], 0.0)
            m_rk = jnp.where(incl, gram[chunk:, chunk:], 0.0)
            t_inv = eye_c + l_ab
            l_pow = _dot3(l_ab, l_ab)
            for i in range(n_sq):
                prod = _dot3(l_pow, jnp.concatenate([t_inv, l_pow], axis=1))
                t_inv = t_inv + prod[:, :chunk]
                if i + 1 < n_sq:
                    l_pow = prod[:, chunk:]
            pq = _dot3(t_inv, jnp.concatenate([a_th, _dot3(l_ak, v_h)], axis=1))
            pqv = jnp.concatenate([pq, jnp.concatenate([zero, v_h], axis=1)], axis=0)
            ry = _dot3(jnp.concatenate([m_rb, m_rk], axis=1), pqv)
            rhat_ref[rs, sl] = r_th + ry[:, :n]
            y0_ref[rs, sl] = ry[:, n:]
            ad = _dot3(pqv.T, jnp.concatenate([b_h[rs, sl], k_h[rs, sl]], axis=0))
            ns = slice(j * n, (j + 1) * n)
            a_ref[ns, sl] = ad[:n] + eye_n * w_c[j * chunk:j * chunk + 1, sl]
            d_ref[ns, sl] = ad[n:]


def _rwkv_scan_body(rhat_ref, y0_ref, a_ref, d_ref, bonus_ref, g_ref, s0_ref, lng_ref, lnb_ref, bd_ref,
                    o_ref, s_ref, y_scr, *, chunk, groups):
    n = RW_HEAD_DIM

    @pl.when(pl.program_id(1) == 0)
    def _():
        s_ref[...] = s0_ref[...]

    for j in range(groups):
        rs = slice(j * chunk, (j + 1) * chunk)
        ns = slice(j * n, (j + 1) * n)
        for h in range(RW_HEADS):
            sl = slice(h * n, (h + 1) * n)
            s_h = s_ref[0, h]
            y_scr[rs, sl] = y0_ref[rs, sl] + _dot_nt(rhat_ref[rs, sl], s_h, HIGHEST)
            s_ref[0, h] = _dot(s_h, a_ref[ns, sl], HIGHEST) + d_ref[ns, sl]

    bd = bd_ref[...]
    y = y_scr[...]
    mean = _dot_ones_rhs(y, bd) * (1.0 / n)
    yc = y - mean
    var = _dot_ones_rhs(yc * yc, bd) * (1.0 / n)
    yn = yc * lax.rsqrt(var + RW_GN_EPS) * lng_ref[...] + lnb_ref[...]
    o_ref[...] = ((yn + bonus_ref[...]) * g_ref[...]).astype(o_ref.dtype)


def _rwkv(rw_all, row_off, batch, t, prev, s0, lp, o_prev):
    n_all, proj = rw_all.shape
    n = RW_HEAD_DIM
    chunk = min(RW_CHUNK, t)
    n_sq = chunk.bit_length() - 2
    nc = t // chunk
    gp = min(RW_PREP_CHUNKS, nc)
    gs = min(RW_SCAN_CHUNKS, nc)
    rows_p = gp * chunk
    rows_s = gs * chunk
    n_tok = batch * t
    row = lambda x: x.reshape(1, -1).astype(F32)
    r = lax.broadcasted_iota(jnp.int32, (rows_p, rows_p), 0)
    c = lax.broadcasted_iota(jnp.int32, (rows_p, rows_p), 1)
    tri = ((c <= r) & (c // chunk == r // chunk)).astype(BF16)
    hr = lax.broadcasted_iota(jnp.int32, (RW_WIDTH, RW_WIDTH), 0) // n
    hc = lax.broadcasted_iota(jnp.int32, (RW_WIDTH, RW_WIDTH), 1) // n
    bd = (hr == hc).astype(BF16)
    const = lambda shape: pl.BlockSpec(shape, lambda b, c: (0,) * len(shape))

    blk_p = row_off // rows_p
    prep_args = [rw_all, rw_all, prev.astype(F32), row(lp['rw_mu']), row(lp['rw_w0']), lp['rw_w2'].astype(F32),
                 row(lp['rw_a0']), lp['rw_a2'].astype(F32), lp['rw_g2'].astype(BF16), row(lp['rw_k_k']),
                 row(lp['rw_k_a']), row(lp['rw_r_k']), tri, bd]
    tok_spec = lambda rows: pl.BlockSpec((rows, RW_WIDTH), lambda b, c: (b * (t // rows) + c, 0))
    mat_spec = lambda g: pl.BlockSpec((g * n, RW_WIDTH), lambda b, c: (b * (nc // g) + c, 0))
    tok_shape = jax.ShapeDtypeStruct((n_tok, RW_WIDTH), F32)
    mat_shape = jax.ShapeDtypeStruct((batch * nc * n, RW_WIDTH), F32)
    rhat, y0, a_mat, d_mat, bonus, g = pl.pallas_call(
        functools.partial(_rwkv_prep_body, chunk=chunk, groups=gp, n_sq=n_sq),
        grid=(batch, nc // gp),
        in_specs=[
            pl.BlockSpec((rows_p, proj), lambda b, c: (blk_p + b * (nc // gp) + c, 0)),
            pl.BlockSpec((8, proj), lambda b, c: (jnp.maximum((row_off + b * t + c * rows_p) // 8 - 1, 0), 0)),
            pl.BlockSpec((1, 1, proj), lambda b, c: (b, 0, 0)),
        ] + [const(x.shape) for x in prep_args[3:]],
        out_specs=[tok_spec(rows_p), tok_spec(rows_p), mat_spec(gp), mat_spec(gp), tok_spec(rows_p),
                   tok_spec(rows_p)],
        out_shape=[tok_shape, tok_shape, mat_shape, mat_shape, tok_shape, tok_shape],
        compiler_params=_params(("parallel", "parallel")),
        name="rwkv7_prep",
    )(*prep_args)

    blk_s = row_off // rows_s
    scan_args = [rhat, y0, a_mat, d_mat, bonus, g, s0.astype(F32), row(lp['rw_ln_g']), row(lp['rw_ln_b']), bd,
                 o_prev]
    state_spec = pl.BlockSpec((1, RW_HEADS, n, n), lambda b, c: (b, 0, 0, 0))

    def scan_body(*refs):
        _rwkv_scan_body(*refs[:10], *refs[11:], chunk=chunk, groups=gs)

    o, state = pl.pallas_call(
        scan_body,
        grid=(batch, nc // gs),
        in_specs=[tok_spec(rows_s), tok_spec(rows_s), mat_spec(gs), mat_spec(gs), tok_spec(rows_s),
                  tok_spec(rows_s), state_spec, const((1, RW_WIDTH)), const((1, RW_WIDTH)), const(bd.shape),
                  pl.BlockSpec(memory_space=pl.ANY)],
        out_specs=[pl.BlockSpec((rows_s, RW_WIDTH), lambda b, c: (blk_s + b * (nc // gs) + c, 0)), state_spec],
        out_shape=[jax.ShapeDtypeStruct((n_all, RW_WIDTH), BF16),
                   jax.ShapeDtypeStruct((batch, RW_HEADS, n, n), F32)],
        scratch_shapes=[pltpu.VMEM((rows_s, RW_WIDTH), F32)],
        input_output_aliases={10: 0},
        compiler_params=_params(("parallel", "arbitrary")),
        name="rwkv7_scan",
    )(*scan_args)
    return o, state


def _merge_body(oa_ref, ob_ref, gab_ref, x_ref, wa_ref, wb_ref, wo_ref, g2_ref, wr_ref, br_ref, tri_ref,
                x2_ref, h2_ref, idx_ref, gate_ref, rank_ref, cnt_ref, carry_scr):
    i = pl.program_id(0)
    d = x_ref.shape[-1]

    @pl.when(i == 0)
    def _():
        carry_scr[...] = jnp.zeros_like(carry_scr)

    gab = gab_ref[...]
    merged = (jax.nn.sigmoid(gab[:, :d]) * _dot(oa_ref[...], wa_ref[...])
              + jax.nn.sigmoid(gab[:, d:]) * _dot(ob_ref[...], wb_ref[...]))
    x2 = x_ref[...] + _dot(merged.astype(BF16), wo_ref[...])
    x2_ref[...] = x2
    h2 = _rmsnorm(x2, g2_ref[...])
    h2_ref[...] = h2
    logits = _dot(h2, wr_ref[...], HIGHEST) + br_ref[...]

    tm, ne = logits.shape
    col = lax.broadcasted_iota(jnp.int32, (tm, ne), 1)
    c4 = lax.broadcasted_iota(jnp.int32, (tm, TOP_K), 1)
    work = logits
    tops, idxs = [], []
    for _ in range(TOP_K):
        m = jnp.max(work, axis=-1, keepdims=True)
        ix = jnp.min(jnp.where(work == m, col, ne), axis=-1, keepdims=True)
        tops.append(m)
        idxs.append(ix)
        work = jnp.where(col == ix, -jnp.inf, work)
    es = [jnp.exp(m - tops[0]) for m in tops]
    denom = es[0] + es[1] + es[2] + es[3]
    onehot = jnp.zeros((tm, ne), F32)
    for ix in idxs:
        onehot = onehot + (col == ix).astype(F32)
    before = carry_scr[...] + _dot(tri_ref[...], onehot.astype(BF16))
    idx_out = jnp.zeros((tm, TOP_K), jnp.int32)
    gate_out = jnp.zeros((tm, TOP_K), F32)
    rank_out = jnp.zeros((tm, TOP_K), jnp.int32)
    for kk in range(TOP_K):
        rk = jnp.sum(jnp.where(col == idxs[kk], before, 0.0), axis=-1, keepdims=True)
        idx_out = jnp.where(c4 == kk, idxs[kk], idx_out)
        gate_out = jnp.where(c4 == kk, es[kk] / denom, gate_out)
        rank_out = jnp.where(c4 == kk, rk.astype(jnp.int32), rank_out)
    idx_ref[...] = idx_out
    gate_ref[...] = gate_out
    rank_ref[...] = rank_out
    carry = carry_scr[...] + jnp.sum(onehot, axis=0, keepdims=True)
    carry_scr[...] = carry
    cnt_ref[...] = carry.astype(jnp.int32)


def _merge_route(oa, ob, gab, x, lp, tm):
    n, d = x.shape
    ne = N_EXPERTS
    r = lax.broadcasted_iota(jnp.int32, (tm, tm), 0)
    c = lax.broadcasted_iota(jnp.int32, (tm, tm), 1)
    tri = (c < r).astype(BF16)
    const = lambda shape: pl.BlockSpec(shape, lambda i: (0,) * len(shape))
    rowblk = lambda wd: pl.BlockSpec((tm, wd), lambda i: (i, 0))
    wa = lp['w_branch_a'].astype(BF16)
    wb = lp['w_branch_b'].astype(BF16)
    wo = lp['w_out'].astype(BF16)
    return pl.pallas_call(
        _merge_body,
        grid=(n // tm,),
        in_specs=[rowblk(oa.shape[1]), rowblk(ob.shape[1]), rowblk(2 * d), rowblk(d),
                  const(wa.shape), const(wb.shape), const(wo.shape), const((1, d)), const((d, ne)),
                  const((1, ne)), const((tm, tm))],
        out_specs=[rowblk(d), rowblk(d), rowblk(TOP_K), rowblk(TOP_K), rowblk(TOP_K), const((1, ne))],
        out_shape=[jax.ShapeDtypeStruct((n, d), F32), jax.ShapeDtypeStruct((n, d), F32),
                   jax.ShapeDtypeStruct((n, TOP_K), jnp.int32), jax.ShapeDtypeStruct((n, TOP_K), F32),
                   jax.ShapeDtypeStruct((n, TOP_K), jnp.int32), jax.ShapeDtypeStruct((1, ne), jnp.int32)],
        scratch_shapes=[pltpu.VMEM((1, ne), F32)],
        compiler_params=_params(("arbitrary",)),
        name="merge_route",
    )(oa, ob, gab, x, wa, wb, wo, lp['norm2_g'].reshape(1, d).astype(F32), lp['w_router'].astype(F32),
      lp['b_router'].reshape(1, ne).astype(F32), tri)


def _dispatch_body(dest_ref, h_ref, xs_in_ref, xs_ref, sem, *, tile):
    del xs_in_ref

    def copy(t, kk):
        return pltpu.make_async_copy(h_ref.at[pl.ds(t, 1)],
                                     xs_ref.at[pl.ds(dest_ref[0, 0, t * TOP_K + kk], 1)], sem)

    def issue(t, carry):
        for kk in range(TOP_K):
            copy(t, kk).start()
        return carry

    def drain(t, carry):
        for kk in range(TOP_K):
            copy(t, kk).wait()
        return carry

    lax.fori_loop(0, tile, issue, 0)
    lax.fori_loop(0, tile, drain, 0)


def _dispatch(h2, dest, n_rows, tile):
    n, d = h2.shape
    nt = n // tile
    xs0 = jnp.zeros((n_rows, d), F32)
    return pl.pallas_call(
        functools.partial(_dispatch_body, tile=tile),
        grid=(nt,),
        in_specs=[pl.BlockSpec((1, 1, tile * TOP_K), lambda i: (i, 0, 0), memory_space=pltpu.SMEM),
                  pl.BlockSpec((tile, d), lambda i: (i, 0)), pl.BlockSpec(memory_space=pl.ANY)],
        out_specs=pl.BlockSpec(memory_space=pl.ANY),
        out_shape=jax.ShapeDtypeStruct((n_rows, d), F32),
        scratch_shapes=[pltpu.SemaphoreType.DMA],
        input_output_aliases={2: 0},
        compiler_params=_params(("arbitrary",)),
        name="moe_dispatch",
    )(dest.reshape(nt, 1, tile * TOP_K), h2, xs0)


def _expert_body(be_ref, used_ref, xs_ref, wgu_ref, bgu_ref, wd_ref, bd_ref, ys_ref):
    del be_ref
    i = pl.program_id(0)
    de = wd_ref.shape[1]

    @pl.when(i < used_ref[0])
    def _():
        gu = _dot(xs_ref[...].astype(BF16), wgu_ref[0]) + bgu_ref[0]
        g = jnp.minimum(gu[:, :de], SWIGLU_LIMIT)
        u = jnp.clip(gu[:, de:], -SWIGLU_LIMIT, SWIGLU_LIMIT)
        act = (u + 1.0) * (g * jax.nn.sigmoid(g * SWIGLU_ALPHA))
        ys_ref[...] = _dot(act.astype(BF16), wd_ref[0]) + bd_ref[0]

    @pl.when(i >= used_ref[0])
    def _():
        ys_ref[...] = jnp.zeros_like(ys_ref)


def _experts(xs, block_expert, n_used, wgu, bgu, wd, bd, bm):
    n_rows, d = xs.shape
    ne, _, de2 = wgu.shape
    nb = n_rows // bm
    grid_spec = pltpu.PrefetchScalarGridSpec(
        num_scalar_prefetch=2,
        grid=(nb,),
        in_specs=[
            pl.BlockSpec((bm, d), lambda i, be, nu: (i, 0)),
            pl.BlockSpec((1, d, de2), lambda i, be, nu: (be[i], 0, 0)),
            pl.BlockSpec((1, 1, de2), lambda i, be, nu: (be[i], 0, 0)),
            pl.BlockSpec((1, de2 // 2, d), lambda i, be, nu: (be[i], 0, 0)),
            pl.BlockSpec((1, 1, d), lambda i, be, nu: (be[i], 0, 0)),
        ],
        out_specs=pl.BlockSpec((bm, d), lambda i, be, nu: (i, 0)),
    )
    return pl.pallas_call(
        _expert_body,
        grid_spec=grid_spec,
        out_shape=jax.ShapeDtypeStruct((n_rows, d), F32),
        compiler_params=_params(("arbitrary",)),
        name="moe_experts",
    )(block_expert, n_used, xs, wgu, bgu.reshape(ne, 1, de2), wd, bd.reshape(ne, 1, d))


def _combine_body(dest_ref, ys_ref, gate_ref, x2_ref, g_ref, y_ref, buf, sem, *, tile):
    def copy(t, kk):
        return pltpu.make_async_copy(ys_ref.at[pl.ds(dest_ref[0, 0, t * TOP_K + kk], 1)],
                                     buf.at[kk, pl.ds(t, 1)], sem)

    def issue(t, carry):
        for kk in range(TOP_K):
            copy(t, kk).start()
        return carry

    def drain(t, carry):
        for kk in range(TOP_K):
            copy(t, kk).wait()
        return carry

    lax.fori_loop(0, tile, issue, 0)
    lax.fori_loop(0, tile, drain, 0)
    gate = gate_ref[...]
    moe = gate[:, 0:1] * buf[0]
    for kk in range(1, TOP_K):
        moe = moe + gate[:, kk:kk + 1] * buf[kk]
    y_ref[...] = _rmsnorm(x2_ref[...] + moe, g_ref[...])


def _combine(ys, dest, gate, x2, final_g, tile):
    n, d = x2.shape
    nt = n // tile
    return pl.pallas_call(
        functools.partial(_combine_body, tile=tile),
        grid=(nt,),
        in_specs=[pl.BlockSpec((1, 1, tile * TOP_K), lambda i: (i, 0, 0), memory_space=pltpu.SMEM),
                  pl.BlockSpec(memory_space=pl.ANY),
                  pl.BlockSpec((tile, TOP_K), lambda i: (i, 0)),
                  pl.BlockSpec((tile, d), lambda i: (i, 0)),
                  pl.BlockSpec((1, d), lambda i: (0, 0))],
        out_specs=pl.BlockSpec((tile, d), lambda i: (i, 0)),
        out_shape=jax.ShapeDtypeStruct((n, d), F32),
        scratch_shapes=[pltpu.VMEM((TOP_K, tile, d), F32), pltpu.SemaphoreType.DMA],
        compiler_params=_params(("arbitrary",)),
        name="moe_combine",
    )(dest.reshape(nt, 1, tile * TOP_K), ys, gate, x2, final_g.reshape(1, d).astype(F32))


def _moe(h2, x2, idx, gate, rank, counts, lp, final_g):
    n, d = h2.shape
    bm = EXPERT_ROWS
    counts = counts.reshape(N_EXPERTS)
    padded = (counts + bm - 1) // bm * bm
    pad_end = jnp.cumsum(padded)
    dest = ((pad_end - padded)[idx] + rank).reshape(n * TOP_K).astype(jnp.int32)
    n_blocks = -(-(n * TOP_K) // bm) + N_EXPERTS
    block_expert = jnp.minimum(
        jnp.searchsorted(pad_end, jnp.arange(n_blocks, dtype=jnp.int32) * bm, side='right'),
        N_EXPERTS - 1).astype(jnp.int32)
    n_used = (pad_end[-1:] // bm).astype(jnp.int32)
    xs = _dispatch(h2, dest, n_blocks * bm, DISPATCH_TILE)
    ys = _experts(xs, block_expert, n_used, lp['w_gate_up'].astype(BF16), lp['b_gate_up'].astype(F32),
                  lp['w_down'].astype(BF16), lp['b_down'].astype(F32), bm)
    return _combine(ys, dest, gate, x2, final_g, COMBINE_TILE)


def _to_heads(t, b, s):
    return t.reshape(b, s, SB_HEADS, SB_HEAD_DIM).transpose(0, 2, 1, 3)


def _from_heads(t):
    b, h, s, dh = t.shape
    return t.transpose(0, 2, 1, 3).reshape(b, s, h * dh)


def _layer(x_all, bp, tp, bs, ts, cache_k, cache_v, state_rwkv, state_shift, lp, final_g):
    n, d = x_all.shape
    sbw = SB_HEADS * SB_HEAD_DIM
    rw_proj = lp['rw_mu'].shape[0]
    n_p = bp * tp
    qkv, rw, gab = _inproj(x_all, lp['norm1_g'].astype(F32), lp['w_in'].astype(BF16),
                           (3 * sbw, rw_proj, 2 * d), ROW_TILE)
    scale = LOG2E * SB_HEAD_DIM ** -0.5

    def split(rows, b, s):
        tok = lambda c: rows[:, c * sbw:(c + 1) * sbw].reshape(b, s, sbw)
        return (tok(0) * scale).astype(BF16), tok(1), tok(2)

    qp, kp, vp = split(qkv[:n_p], bp, tp)
    oa_p = _stick_breaking(qp, kp.astype(BF16), vp.astype(BF16), 0, SB_BLOCK, SB_BLOCK)
    qs, ks, vs = split(qkv[n_p:], bs, ts)
    past = cache_k.shape[2]
    k_all = jnp.concatenate([_from_heads(cache_k), ks], axis=1).astype(BF16)
    v_all = jnp.concatenate([_from_heads(cache_v), vs], axis=1).astype(BF16)
    oa_s = _stick_breaking(qs, k_all, v_all, past, ts, min(SB_BLOCK, past))
    oa = jnp.concatenate([oa_p.reshape(n_p, sbw), oa_s.reshape(bs * ts, sbw)], axis=0)

    s0_p = jnp.zeros((bp, RW_HEADS, RW_HEAD_DIM, RW_HEAD_DIM), F32)
    prev_p = jnp.zeros((bp, 1, rw_proj), F32)
    ob, st_p = _rwkv(rw, 0, bp, tp, prev_p, s0_p, lp, jnp.zeros((n, RW_WIDTH), BF16))
    ob, st_s = _rwkv(rw, n_p, bs, ts, state_shift, state_rwkv, lp, ob)
    sh_p = rw[:n_p].reshape(bp, tp, rw_proj)[:, tp - 1:]
    sh_s = rw[n_p:].reshape(bs, ts, rw_proj)[:, ts - 1:]

    x2, h2, idx, gate, rank, counts = _merge_route(oa, ob, gab, x_all, lp, ROW_TILE)
    y = _moe(h2, x2, idx, gate, rank, counts, lp, final_g)
    return (y, _to_heads(kp, bp, tp), _to_heads(vp, bp, tp), st_p, sh_p,
            _to_heads(ks, bs, ts), _to_heads(vs, bs, ts), st_s, sh_s)


def kernel(x_prompt, x_sample, cache_sb_k, cache_sb_v, state_rwkv, state_shift, norm1_g, w_in, rw_mu, rw_w0, rw_w2, rw_a0, rw_a2, rw_g2, rw_k_k, rw_k_a, rw_r_k, rw_ln_g, rw_ln_b, w_branch_a, w_branch_b, w_out, norm2_g, w_router, b_router, w_gate_up, b_gate_up, w_down, b_down, final_norm_g):
    depth = w_in.shape[0]
    assert depth == 1, "the final RMSNorm is fused into the last (only) layer"
    bp, tp, d = x_prompt.shape
    bs, ts, _ = x_sample.shape
    n_p = bp * tp
    x_all = jnp.concatenate([x_prompt.reshape(n_p, d), x_sample.reshape(bs * ts, d)], axis=0)
    lp = dict(norm1_g=norm1_g[0], w_in=w_in[0], rw_mu=rw_mu[0], rw_w0=rw_w0[0], rw_w2=rw_w2[0], rw_a0=rw_a0[0],
              rw_a2=rw_a2[0], rw_g2=rw_g2[0], rw_k_k=rw_k_k[0], rw_k_a=rw_k_a[0], rw_r_k=rw_r_k[0].reshape(-1),
              rw_ln_g=rw_ln_g[0], rw_ln_b=rw_ln_b[0], w_branch_a=w_branch_a[0], w_branch_b=w_branch_b[0],
              w_out=w_out[0], norm2_g=norm2_g[0], w_router=w_router[0], b_router=b_router[0],
              w_gate_up=w_gate_up[0], b_gate_up=b_gate_up[0], w_down=w_down[0], b_down=b_down[0])
    y, kp, vp, st_p, sh_p, ks, vs, st_s, sh_s = _layer(
        x_all, bp, tp, bs, ts, cache_sb_k[0], cache_sb_v[0], state_rwkv[0], state_shift[0], lp, final_norm_g)
    return (y[:n_p].reshape(bp, tp, d), y[n_p:].reshape(bs, ts, d),
            kp[None], vp[None], st_p[None], sh_p[None], ks[None], vs[None], st_s[None], sh_s[None])
```

```python
import functools

import jax
import jax.numpy as jnp
from jax import lax
from jax.experimental import pallas as pl
from jax.experimental.pallas import tpu as pltpu

F32 = jnp.float32
BF16 = jnp.bfloat16
HIGHEST = lax.Precision.HIGHEST

SB_HEADS = 8
SB_HEAD_DIM = 64
RW_HEADS = 8
RW_HEAD_DIM = 64
RW_WIDTH = RW_HEADS * RW_HEAD_DIM
RW_DECAY_LORA = 64
RW_AAA_LORA = 64
RW_GATE_LORA = 128
RW_GN_EPS = 64e-5
N_EXPERTS = 32
TOP_K = 4
SWIGLU_LIMIT = 7.0
SWIGLU_ALPHA = 1.702
RMS_EPS = 1e-6

VMEM_LIMIT_BYTES = 56 * 1024 * 1024
ROW_TILE = 256
SB_BLOCK = 512
SB_SUB = 256
LOG2E = 1.4426950408889634
SB_DEAD_LOG2 = -200.0
RW_CHUNK = 64
RW_PREP_CHUNKS = 2
RW_SCAN_CHUNKS = 4
EXPERT_ROWS = 512
DISPATCH_TILE = 512
COMBINE_TILE = 128
DMA_UNROLL = 8


def _params(semantics):
    return pltpu.CompilerParams(dimension_semantics=semantics, vmem_limit_bytes=VMEM_LIMIT_BYTES)


def _dot(a, b, precision=None):
    return jnp.dot(a, b, preferred_element_type=F32, precision=precision)


def _dot_nt(a, b, precision=None):
    return lax.dot_general(a, b, (((1,), (1,)), ((), ())), preferred_element_type=F32, precision=precision)


def _dot_tn(a, b, precision=None):
    return lax.dot_general(a, b, (((0,), (0,)), ((), ())), preferred_element_type=F32, precision=precision)


def _softplus_parts(z):
    l = jnp.log1p(jnp.exp(-jnp.abs(z)))
    sp = jnp.maximum(z, 0.0) + l
    return sp, sp - z


def _rmsnorm(x, g):
    return x * lax.rsqrt(jnp.mean(x * x, axis=-1, keepdims=True) + RMS_EPS) * g


def _inproj_body(x_ref, g_ref, w_ref, qkv_ref, rw_ref, gab_ref, *, col_chunk):
    h = _rmsnorm(x_ref[...], g_ref[...]).astype(BF16)
    off = 0
    for ref in (qkv_ref, rw_ref, gab_ref):
        width = ref.shape[-1]
        for c0 in range(0, width, col_chunk):
            cw = min(col_chunk, width - c0)
            ref[:, c0:c0 + cw] = _dot(h, w_ref[:, off + c0:off + c0 + cw])
        off += width


def _inproj(x, g, w_bf16, widths, tm):
    n, d = x.shape
    total = w_bf16.shape[1]
    return pl.pallas_call(
        functools.partial(_inproj_body, col_chunk=512),
        grid=(n // tm,),
        in_specs=[
            pl.BlockSpec((tm, d), lambda i: (i, 0)),
            pl.BlockSpec((1, d), lambda i: (0, 0)),
            pl.BlockSpec((d, total), lambda i: (0, 0), pipeline_mode=pl.Buffered(1)),
        ],
        out_specs=[pl.BlockSpec((tm, wd), lambda i: (i, 0)) for wd in widths],
        out_shape=[jax.ShapeDtypeStruct((n, wd), F32) for wd in widths],
        compiler_params=_params(("parallel",)),
        name="inproj",
    )(x, g.reshape(1, d), w_bf16)


def _sb_tile(q, kb, vb, ntri, carry, mask, sb):
    tk = kb.shape[0]
    z = _dot_nt(q, kb)
    neg_abs = lax.bitcast_convert_type(lax.bitcast_convert_type(z, jnp.uint32) | jnp.uint32(0x80000000), F32)
    sp = jnp.maximum(z, 0.0) + jnp.log(1.0 + jnp.exp2(neg_abs)) * LOG2E
    if mask is not None:
        sp = jnp.where(mask, sp, 0.0)
    hi = sp.astype(BF16)
    lo = (sp - hi.astype(F32)).astype(BF16)
    log_beta = z - sp
    wts = []
    for j in range(tk // sb - 1, -1, -1):
        cols = slice(j * sb, (j + 1) * sb)
        later = _dot(hi[:, cols], ntri) + _dot(lo[:, cols], ntri) + carry
        w = jnp.exp2(log_beta[:, cols] + later)
        if mask is not None:
            w = jnp.where(mask[:, cols], w, 0.0)
        wts.append(w.astype(BF16))
        carry = carry - jnp.sum(sp[:, cols], axis=-1, keepdims=True)
    wts = wts[0] if len(wts) == 1 else jnp.concatenate(wts[::-1], axis=1)
    return _dot(wts, vb), carry


def _sb_body(q_ref, k_ref, v_ref, triq_ref, trik_ref, o_ref, *, tq, tk, past):
    qi = pl.program_id(2)
    q = q_ref[0]
    first = lax.broadcasted_iota(jnp.int32, q.shape, 1) < SB_HEAD_DIM
    zero = jnp.zeros_like(q)
    qs = (jnp.where(first, q, zero), jnp.where(first, zero, q))
    d0 = pl.multiple_of(past + qi * tq, tq)
    row = lax.broadcasted_iota(jnp.int32, (tq, tq), 0)
    col = lax.broadcasted_iota(jnp.int32, (tq, tq), 1)
    kd = k_ref[0, pl.ds(d0, tq), :]
    vd = v_ref[0, pl.ds(d0, tq), :]
    sbq = triq_ref.shape[0]
    state = []
    for qh in qs:
        state.extend(_sb_tile(qh, kd, vd, triq_ref[...], jnp.zeros((tq, 1), F32), col < row, sbq))
    nb = (past + qi * tq) // tk
    sbk = trik_ref.shape[0]

    def live(state):
        return jnp.max(jnp.maximum(state[1], state[3])) > SB_DEAD_LOG2

    def cond(loop):
        i, alive, _ = loop
        return jnp.logical_and(i < nb, alive)

    def step(loop):
        i, _, state = loop
        s0 = pl.multiple_of((nb - 1 - i) * tk, tk)
        kb = k_ref[0, pl.ds(s0, tk), :]
        vb = v_ref[0, pl.ds(s0, tk), :]
        new = []
        for h, qh in enumerate(qs):
            out, carry = _sb_tile(qh, kb, vb, trik_ref[...], state[2 * h + 1], None, sbk)
            new.extend((state[2 * h] + out, carry))
        return i + 1, live(new), tuple(new)

    _, _, state = lax.while_loop(cond, step, (jnp.int32(0), live(state), tuple(state)))
    o_ref[0] = jnp.where(first, state[0], state[2]).astype(o_ref.dtype)


def _later_matrix(n):
    r = lax.broadcasted_iota(jnp.int32, (n, n), 0)
    c = lax.broadcasted_iota(jnp.int32, (n, n), 1)
    return jnp.where(r > c, -1.0, 0.0).astype(BF16)


def _stick_breaking(q, k, v, past, tq, tk):
    b, t, w = q.shape
    s = k.shape[1]
    pair = 2 * SB_HEAD_DIM
    sbq = min(tq, SB_SUB)
    sbk = min(tk, SB_SUB)
    return pl.pallas_call(
        functools.partial(_sb_body, tq=tq, tk=tk, past=past),
        grid=(b, w // pair, t // tq),
        in_specs=[
            pl.BlockSpec((1, tq, pair), lambda b, h, i: (b, i, h)),
            pl.BlockSpec((1, s, pair), lambda b, h, i: (b, 0, h)),
            pl.BlockSpec((1, s, pair), lambda b, h, i: (b, 0, h)),
            pl.BlockSpec((sbq, sbq), lambda b, h, i: (0, 0)),
            pl.BlockSpec((sbk, sbk), lambda b, h, i: (0, 0)),
        ],
        out_specs=pl.BlockSpec((1, tq, pair), lambda b, h, i: (b, i, h)),
        out_shape=jax.ShapeDtypeStruct((b, t, w), BF16),
        compiler_params=_params(("parallel", "parallel", "arbitrary")),
        name="stick_breaking",
    )(q, k, v, _later_matrix(sbq), _later_matrix(sbk))


def _split_bf16(x):
    hi = x.astype(BF16)
    return hi, (x - hi.astype(F32)).astype(BF16)


def _dot3(a, b, dot=_dot):
    ah, al = _split_bf16(a)
    bh, bl = _split_bf16(b)
    return dot(ah, bh) + dot(ah, bl) + dot(al, bh)


def _dot_ones_rhs(x, ones_bf16):
    hi, lo = _split_bf16(x)
    return _dot(hi, ones_bf16) + _dot(lo, ones_bf16)


def _dot_ones_lhs(ones_bf16, x):
    hi, lo = _split_bf16(x)
    return _dot(ones_bf16, hi) + _dot(ones_bf16, lo)


HEADS_PER_MATMUL = 4
GROUP_WIDTH = HEADS_PER_MATMUL * RW_HEAD_DIM


def _block_diag(y, mask):
    return jnp.concatenate([y] * HEADS_PER_MATMUL, axis=0) * mask


def _bdmm(xs, ys, mask, dot=_dot):
    xh, xl = xs
    yh, yl = ys
    outs = []
    for g in range(RW_WIDTH // GROUP_WIDTH):
        ls = slice(g * GROUP_WIDTH, (g + 1) * GROUP_WIDTH)
        wh = _block_diag(yh[:, ls], mask)
        wl = _block_diag(yl[:, ls], mask)
        outs.append(dot(xh[:, ls], wh) + dot(xl[:, ls], wh) + dot(xh[:, ls], wl))
    return jnp.concatenate(outs, axis=1)


def _bd_tn(x, y, mask):
    n = RW_HEAD_DIM
    outs = []
    for g in range(RW_WIDTH // GROUP_WIDTH):
        ls = slice(g * GROUP_WIDTH, (g + 1) * GROUP_WIDTH)
        z = _dot3(x[:, ls].T, y[:, ls]) * mask
        outs.append(z[0:n] + z[n:2 * n] + z[2 * n:3 * n] + z[3 * n:4 * n])
    return jnp.concatenate(outs, axis=1)


def _rwkv_prep_body(p_ref, pb_ref, prev_ref, mu_ref, w0_ref, w2_ref, a0_ref, a2_ref, g2_ref, kk_ref, ka_ref,
                    rk_ref, tri_ref, bd_ref, mbf_ref, mf32_ref,
                    rhat_ref, y0_ref, a_ref, d_ref, bonus_ref, g_ref, *, chunk, groups, n_sq, t_valid):
    n = RW_HEAD_DIM
    w_ = RW_WIDTH
    p = p_ref[...]
    rows = lax.broadcasted_iota(jnp.int32, p.shape, 0)
    prev_row = jnp.where(pl.program_id(1) == 0, prev_ref[0], pb_ref[7:8, :])
    p_shift = jnp.where(rows == 0, prev_row, pltpu.roll(p, 1, 0))

    xs = p + (p_shift - p) * mu_ref[...]
    r = xs[:, 0:w_]
    k = xs[:, w_:2 * w_]
    v = xs[:, 2 * w_:3 * w_]
    o1 = 3 * w_
    dw = xs[:, o1:o1 + RW_DECAY_LORA]
    da = xs[:, o1 + RW_DECAY_LORA:o1 + RW_DECAY_LORA + RW_AAA_LORA]
    dg = xs[:, o1 + RW_DECAY_LORA + RW_AAA_LORA:]

    sp_neg = _softplus_parts(w0_ref[...] + _dot3(jnp.tanh(dw), w2_ref[...]))[1]
    log_decay = -jnp.exp(-sp_neg - 0.5)
    a = jax.nn.sigmoid(a0_ref[...] + _dot3(da, a2_ref[...]))
    g_ref[...] = _dot(jax.nn.sigmoid(dg).astype(BF16), g2_ref[...])

    bd = bd_ref[...]
    kk = k * kk_ref[...]
    kk = kk / jnp.maximum(jnp.sqrt(_dot_ones_rhs(kk * kk, bd)), 1e-12)
    k2 = k * (1.0 + (a - 1.0) * ka_ref[...])
    a_vec = -kk
    b_vec = kk * a
    bonus_ref[...] = _dot_ones_rhs(r * k2 * rk_ref[...], bd) * v
    if t_valid < chunk:
        live = lax.broadcasted_iota(jnp.int32, r.shape, 0) < t_valid
        keep = lambda x: jnp.where(live, x, 0.0)
        log_decay, r, k2, v, a_vec, b_vec = (keep(x) for x in (log_decay, r, k2, v, a_vec, b_vec))

    cw = _dot_ones_lhs(tri_ref[...], log_decay)
    cw_last = jnp.concatenate(
        [jnp.broadcast_to(cw[(j + 1) * chunk - 1:(j + 1) * chunk, :], (chunk, w_)) for j in range(groups)], axis=0)
    r_t = r * jnp.exp(cw)
    a_t = a_vec * jnp.exp(cw - log_decay)
    e_neg = jnp.exp(-cw)
    b_t = b_vec * e_neg
    k_t = k2 * e_neg
    e_rel = jnp.exp(cw_last - cw)
    b_h = b_vec * e_rel
    k_h = k2 * e_rel
    w_c = jnp.exp(cw_last)

    rr = lax.broadcasted_iota(jnp.int32, (chunk, w_), 0)
    ss = lax.broadcasted_iota(jnp.int32, (chunk, w_), 1) % n
    strict = ss < rr
    incl = ss <= rr
    eye = (ss == rr).astype(F32)
    m_bf16 = mbf_ref[...]
    m_f32 = mf32_ref[...]

    for j in range(groups):
        rs = slice(j * chunk, (j + 1) * chunk)
        v_j = v[rs]
        v_s = _split_bf16(v_j)
        ar_s = _split_bf16(jnp.concatenate([a_t[rs], r_t[rs]], axis=0))
        gram_b = _bdmm(ar_s, _split_bf16(b_t[rs]), m_bf16, _dot_nt)
        gram_k = _bdmm(ar_s, _split_bf16(k_t[rs]), m_bf16, _dot_nt)
        l_ab = jnp.where(strict, gram_b[:chunk], 0.0)
        l_ak = jnp.where(strict, gram_k[:chunk], 0.0)
        m_rb_s = _split_bf16(jnp.where(incl, gram_b[chunk:], 0.0))
        m_rk_s = _split_bf16(jnp.where(incl, gram_k[chunk:], 0.0))
        l_s = _split_bf16(l_ab)
        t_inv = eye + l_ab
        l_pow = _bdmm(l_s, l_s, m_bf16)
        for i in range(n_sq):
            lp_s = _split_bf16(l_pow)
            t_inv = t_inv + _bdmm(lp_s, _split_bf16(t_inv), m_bf16)
            if i + 1 < n_sq:
                l_pow = _bdmm(lp_s, lp_s, m_bf16)
        t_s = _split_bf16(t_inv)
        p_mat = _bdmm(t_s, _split_bf16(a_t[rs]), m_bf16)
        q_mat = _bdmm(t_s, _split_bf16(_bdmm(_split_bf16(l_ak), v_s, m_bf16)), m_bf16)
        rhat_ref[rs, :] = r_t[rs] + _bdmm(m_rb_s, _split_bf16(p_mat), m_bf16)
        y0_ref[rs, :] = _bdmm(m_rb_s, _split_bf16(q_mat), m_bf16) + _bdmm(m_rk_s, v_s, m_bf16)
        ns = slice(j * n, (j + 1) * n)
        a_ref[ns, :] = _bd_tn(p_mat, b_h[rs], m_f32) + eye * w_c[j * chunk:j * chunk + 1, :]
        d_ref[ns, :] = _bd_tn(jnp.concatenate([q_mat, v_j], axis=0),
                              jnp.concatenate([b_h[rs], k_h[rs]], axis=0), m_f32)


def _rwkv_scan_body(rhat_ref, y0_ref, a_ref, d_ref, bonus_ref, g_ref, s0_ref, lng_ref, lnb_ref, bd_ref,
                    o_ref, s_ref, y_scr, *, chunk, groups):
    n = RW_HEAD_DIM

    @pl.when(pl.program_id(1) == 0)
    def _():
        s_ref[...] = s0_ref[...]

    for j in range(groups):
        rs = slice(j * chunk, (j + 1) * chunk)
        ns = slice(j * n, (j + 1) * n)
        for h in range(RW_HEADS):
            sl = slice(h * n, (h + 1) * n)
            s_h = s_ref[0, h]
            y_scr[rs, sl] = y0_ref[rs, sl] + _dot_nt(rhat_ref[rs, sl], s_h, HIGHEST)
            s_ref[0, h] = _dot(s_h, a_ref[ns, sl], HIGHEST) + d_ref[ns, sl]

    bd = bd_ref[...]
    y = y_scr[...]
    mean = _dot_ones_rhs(y, bd) * (1.0 / n)
    yc = y - mean
    var = _dot_ones_rhs(yc * yc, bd) * (1.0 / n)
    yn = yc * lax.rsqrt(var + RW_GN_EPS) * lng_ref[...] + lnb_ref[...]
    o_ref[...] = ((yn + bonus_ref[...]) * g_ref[...]).astype(o_ref.dtype)


def _rwkv(rw_all, batch, t, t_valid, prev, s0, lp, o_prev):
    proj = rw_all.shape[1]
    n_all = o_prev.shape[0]
    row_off = 0
    n = RW_HEAD_DIM
    chunk = RW_CHUNK
    assert chunk == n and t % chunk == 0 and (t_valid == t or t == chunk)
    n_sq = chunk.bit_length() - 2
    nc = t // chunk
    gp = min(RW_PREP_CHUNKS, nc)
    gs = min(RW_SCAN_CHUNKS, nc)
    rows_p = gp * chunk
    rows_s = gs * chunk
    n_tok = batch * t
    row = lambda x: x.reshape(1, -1).astype(F32)
    r = lax.broadcasted_iota(jnp.int32, (rows_p, rows_p), 0)
    c = lax.broadcasted_iota(jnp.int32, (rows_p, rows_p), 1)
    tri = ((c <= r) & (c // chunk == r // chunk)).astype(BF16)
    hr = lax.broadcasted_iota(jnp.int32, (RW_WIDTH, RW_WIDTH), 0) // n
    hc = lax.broadcasted_iota(jnp.int32, (RW_WIDTH, RW_WIDTH), 1) // n
    bd = (hr == hc).astype(BF16)
    group_mask = (hr == hc)[:GROUP_WIDTH, :GROUP_WIDTH]
    const = lambda shape: pl.BlockSpec(shape, lambda b, c: (0,) * len(shape))

    blk_p = row_off // rows_p
    prep_args = [rw_all, rw_all, prev.astype(F32), row(lp['rw_mu']), row(lp['rw_w0']), lp['rw_w2'].astype(F32),
                 row(lp['rw_a0']), lp['rw_a2'].astype(F32), lp['rw_g2'].astype(BF16), row(lp['rw_k_k']),
                 row(lp['rw_k_a']), row(lp['rw_r_k']), tri, bd, group_mask.astype(BF16), group_mask.astype(F32)]
    tok_spec = lambda rows: pl.BlockSpec((rows, RW_WIDTH), lambda b, c: (b * (t // rows) + c, 0))
    mat_spec = lambda g: pl.BlockSpec((g * n, RW_WIDTH), lambda b, c: (b * (nc // g) + c, 0))
    tok_shape = jax.ShapeDtypeStruct((n_tok, RW_WIDTH), F32)
    mat_shape = jax.ShapeDtypeStruct((batch * nc * n, RW_WIDTH), F32)
    rhat, y0, a_mat, d_mat, bonus, g = pl.pallas_call(
        functools.partial(_rwkv_prep_body, chunk=chunk, groups=gp, n_sq=n_sq, t_valid=t_valid),
        grid=(batch, nc // gp),
        in_specs=[
            pl.BlockSpec((rows_p, proj), lambda b, c: (blk_p + b * (nc // gp) + c, 0)),
            pl.BlockSpec((8, proj), lambda b, c: (jnp.maximum((row_off + b * t + c * rows_p) // 8 - 1, 0), 0)),
            pl.BlockSpec((1, 1, proj), lambda b, c: (b, 0, 0)),
        ] + [const(x.shape) for x in prep_args[3:]],
        out_specs=[tok_spec(rows_p), tok_spec(rows_p), mat_spec(gp), mat_spec(gp), tok_spec(rows_p),
                   tok_spec(rows_p)],
        out_shape=[tok_shape, tok_shape, mat_shape, mat_shape, tok_shape, tok_shape],
        compiler_params=_params(("parallel", "parallel")),
        name="rwkv7_prep",
    )(*prep_args)

    blk_s = row_off // rows_s
    scan_args = [rhat, y0, a_mat, d_mat, bonus, g, s0.astype(F32), row(lp['rw_ln_g']), row(lp['rw_ln_b']), bd,
                 o_prev]
    state_spec = pl.BlockSpec((1, RW_HEADS, n, n), lambda b, c: (b, 0, 0, 0))

    def scan_body(*refs):
        _rwkv_scan_body(*refs[:10], *refs[11:], chunk=chunk, groups=gs)

    o, state = pl.pallas_call(
        scan_body,
        grid=(batch, nc // gs),
        in_specs=[tok_spec(rows_s), tok_spec(rows_s), mat_spec(gs), mat_spec(gs), tok_spec(rows_s),
                  tok_spec(rows_s), state_spec, const((1, RW_WIDTH)), const((1, RW_WIDTH)), const(bd.shape),
                  pl.BlockSpec(memory_space=pl.ANY)],
        out_specs=[pl.BlockSpec((rows_s, RW_WIDTH), lambda b, c: (blk_s + b * (nc // gs) + c, 0)), state_spec],
        out_shape=[jax.ShapeDtypeStruct((n_all, RW_WIDTH), BF16),
                   jax.ShapeDtypeStruct((batch, RW_HEADS, n, n), F32)],
        scratch_shapes=[pltpu.VMEM((rows_s, RW_WIDTH), F32)],
        input_output_aliases={10: 0},
        compiler_params=_params(("parallel", "arbitrary")),
        name="rwkv7_scan",
    )(*scan_args)
    return o, state


def _merge_body(oa_ref, ob_ref, gab_ref, x_ref, wa_ref, wb_ref, wo_ref, g2_ref, wr_ref, br_ref, tri_ref,
                x2_ref, h2_ref, idx_ref, gate_ref, rank_ref, cnt_ref, carry_scr):
    i = pl.program_id(0)
    d = x_ref.shape[-1]

    @pl.when(i == 0)
    def _():
        carry_scr[...] = jnp.zeros_like(carry_scr)

    gab = gab_ref[...]
    merged = (jax.nn.sigmoid(gab[:, :d]) * _dot(oa_ref[...], wa_ref[...])
              + jax.nn.sigmoid(gab[:, d:]) * _dot(ob_ref[...], wb_ref[...]))
    x2 = x_ref[...] + _dot(merged.astype(BF16), wo_ref[...])
    x2_ref[...] = x2
    h2 = _rmsnorm(x2, g2_ref[...])
    h2_ref[...] = h2
    logits = _dot(h2, wr_ref[...], HIGHEST) + br_ref[...]

    tm, ne = logits.shape
    col = lax.broadcasted_iota(jnp.int32, (tm, ne), 1)
    c4 = lax.broadcasted_iota(jnp.int32, (tm, TOP_K), 1)
    work = logits
    tops, idxs = [], []
    for _ in range(TOP_K):
        m = jnp.max(work, axis=-1, keepdims=True)
        ix = jnp.min(jnp.where(work == m, col, ne), axis=-1, keepdims=True)
        tops.append(m)
        idxs.append(ix)
        work = jnp.where(col == ix, -jnp.inf, work)
    es = [jnp.exp(m - tops[0]) for m in tops]
    denom = es[0] + es[1] + es[2] + es[3]
    onehot = jnp.zeros((tm, ne), F32)
    for ix in idxs:
        onehot = onehot + (col == ix).astype(F32)
    before = carry_scr[...] + _dot(tri_ref[...], onehot.astype(BF16))
    idx_out = jnp.zeros((tm, TOP_K), jnp.int32)
    gate_out = jnp.zeros((tm, TOP_K), F32)
    rank_out = jnp.zeros((tm, TOP_K), jnp.int32)
    for kk in range(TOP_K):
        rk = jnp.sum(jnp.where(col == idxs[kk], before, 0.0), axis=-1, keepdims=True)
        idx_out = jnp.where(c4 == kk, idxs[kk], idx_out)
        gate_out = jnp.where(c4 == kk, es[kk] / denom, gate_out)
        rank_out = jnp.where(c4 == kk, rk.astype(jnp.int32), rank_out)
    idx_ref[...] = idx_out
    gate_ref[...] = gate_out
    rank_ref[...] = rank_out
    carry = carry_scr[...] + jnp.sum(onehot, axis=0, keepdims=True)
    carry_scr[...] = carry
    cnt_ref[...] = carry.astype(jnp.int32)


def _merge_route(oa, ob, gab, x, lp, tm):
    n, d = x.shape
    ne = N_EXPERTS
    r = lax.broadcasted_iota(jnp.int32, (tm, tm), 0)
    c = lax.broadcasted_iota(jnp.int32, (tm, tm), 1)
    tri = (c < r).astype(BF16)
    const = lambda shape: pl.BlockSpec(shape, lambda i: (0,) * len(shape))
    rowblk = lambda wd: pl.BlockSpec((tm, wd), lambda i: (i, 0))
    wa = lp['w_branch_a'].astype(BF16)
    wb = lp['w_branch_b'].astype(BF16)
    wo = lp['w_out'].astype(BF16)
    return pl.pallas_call(
        _merge_body,
        grid=(n // tm,),
        in_specs=[rowblk(oa.shape[1]), rowblk(ob.shape[1]), rowblk(2 * d), rowblk(d),
                  const(wa.shape), const(wb.shape), const(wo.shape), const((1, d)), const((d, ne)),
                  const((1, ne)), const((tm, tm))],
        out_specs=[rowblk(d), rowblk(d), rowblk(TOP_K), rowblk(TOP_K), rowblk(TOP_K), const((1, ne))],
        out_shape=[jax.ShapeDtypeStruct((n, d), F32), jax.ShapeDtypeStruct((n, d), F32),
                   jax.ShapeDtypeStruct((n, TOP_K), jnp.int32), jax.ShapeDtypeStruct((n, TOP_K), F32),
                   jax.ShapeDtypeStruct((n, TOP_K), jnp.int32), jax.ShapeDtypeStruct((1, ne), jnp.int32)],
        scratch_shapes=[pltpu.VMEM((1, ne), F32)],
        compiler_params=_params(("arbitrary",)),
        name="merge_route",
    )(oa, ob, gab, x, wa, wb, wo, lp['norm2_g'].reshape(1, d).astype(F32), lp['w_router'].astype(F32),
      lp['b_router'].reshape(1, ne).astype(F32), tri)


def _dispatch_body(dest_ref, h_ref, xs_in_ref, xs_ref, sem, *, tile):
    del xs_in_ref

    def copy(t, kk):
        return pltpu.make_async_copy(h_ref.at[pl.ds(t, 1)],
                                     xs_ref.at[pl.ds(dest_ref[0, 0, t * TOP_K + kk], 1)], sem)

    def issue(t, carry):
        for kk in range(TOP_K):
            copy(t, kk).start(priority=kk % 2)
        return carry

    def drain(t, carry):
        for kk in range(TOP_K):
            copy(t, kk).wait()
        return carry

    lax.fori_loop(0, tile, issue, 0, unroll=DMA_UNROLL)
    lax.fori_loop(0, tile, drain, 0, unroll=DMA_UNROLL)


def _dispatch(h2, dest, n_rows, tile):
    n, d = h2.shape
    nt = n // tile
    xs0 = jnp.zeros((n_rows, d), F32)
    return pl.pallas_call(
        functools.partial(_dispatch_body, tile=tile),
        grid=(nt,),
        in_specs=[pl.BlockSpec((1, 1, tile * TOP_K), lambda i: (i, 0, 0), memory_space=pltpu.SMEM),
                  pl.BlockSpec((tile, d), lambda i: (i, 0)), pl.BlockSpec(memory_space=pl.ANY)],
        out_specs=pl.BlockSpec(memory_space=pl.ANY),
        out_shape=jax.ShapeDtypeStruct((n_rows, d), F32),
        scratch_shapes=[pltpu.SemaphoreType.DMA],
        input_output_aliases={2: 0},
        compiler_params=_params(("arbitrary",)),
        name="moe_dispatch",
    )(dest.reshape(nt, 1, tile * TOP_K), h2, xs0)


def _expert_body(be_ref, used_ref, xs_ref, wgu_ref, bgu_ref, wd_ref, bd_ref, ys_ref):
    del be_ref
    i = pl.program_id(0)
    de = wd_ref.shape[1]

    @pl.when(i < used_ref[0])
    def _():
        gu = _dot(xs_ref[...].astype(BF16), wgu_ref[0]) + bgu_ref[0]
        g = jnp.minimum(gu[:, :de], SWIGLU_LIMIT)
        u = jnp.clip(gu[:, de:], -SWIGLU_LIMIT, SWIGLU_LIMIT)
        act = (u + 1.0) * (g * jax.nn.sigmoid(g * SWIGLU_ALPHA))
        ys_ref[...] = _dot(act.astype(BF16), wd_ref[0]) + bd_ref[0]

    @pl.when(i >= used_ref[0])
    def _():
        ys_ref[...] = jnp.zeros_like(ys_ref)


def _experts(xs, block_expert, n_used, wgu, bgu, wd, bd, bm):
    n_rows, d = xs.shape
    ne, _, de2 = wgu.shape
    nb = n_rows // bm
    grid_spec = pltpu.PrefetchScalarGridSpec(
        num_scalar_prefetch=2,
        grid=(nb,),
        in_specs=[
            pl.BlockSpec((bm, d), lambda i, be, nu: (i, 0)),
            pl.BlockSpec((1, d, de2), lambda i, be, nu: (be[i], 0, 0)),
            pl.BlockSpec((1, 1, de2), lambda i, be, nu: (be[i], 0, 0)),
            pl.BlockSpec((1, de2 // 2, d), lambda i, be, nu: (be[i], 0, 0)),
            pl.BlockSpec((1, 1, d), lambda i, be, nu: (be[i], 0, 0)),
        ],
        out_specs=pl.BlockSpec((bm, d), lambda i, be, nu: (i, 0)),
    )
    return pl.pallas_call(
        _expert_body,
        grid_spec=grid_spec,
        out_shape=jax.ShapeDtypeStruct((n_rows, d), F32),
        compiler_params=_params(("arbitrary",)),
        name="moe_experts",
    )(block_expert, n_used, xs, wgu, bgu.reshape(ne, 1, de2), wd, bd.reshape(ne, 1, d))


def _combine_body(dest_ref, ys_ref, gate_ref, x2_ref, g_ref, ya_ref, yb_ref, buf, sem, *, tile, tiles_a):
    def copy(t, kk):
        return pltpu.make_async_copy(ys_ref.at[pl.ds(dest_ref[0, 0, t * TOP_K + kk], 1)],
                                     buf.at[kk, pl.ds(t, 1)], sem)

    def issue(t, carry):
        for kk in range(TOP_K):
            copy(t, kk).start(priority=kk % 2)
        return carry

    def drain(t, carry):
        for kk in range(TOP_K):
            copy(t, kk).wait()
        return carry

    lax.fori_loop(0, tile, issue, 0, unroll=DMA_UNROLL)
    lax.fori_loop(0, tile, drain, 0, unroll=DMA_UNROLL)
    gate = gate_ref[...]
    moe = gate[:, 0:1] * buf[0]
    for kk in range(1, TOP_K):
        moe = moe + gate[:, kk:kk + 1] * buf[kk]
    y = _rmsnorm(x2_ref[...] + moe, g_ref[...])
    i = pl.program_id(0)

    @pl.when(i < tiles_a)
    def _():
        ya_ref[...] = y

    @pl.when(i >= tiles_a)
    def _():
        yb_ref[...] = y


def _combine(ys, dest, gate, x2, final_g, tile, n_a):
    n, d = x2.shape
    nt = n // tile
    tiles_a = n_a // tile
    return pl.pallas_call(
        functools.partial(_combine_body, tile=tile, tiles_a=tiles_a),
        grid=(nt,),
        in_specs=[pl.BlockSpec((1, 1, tile * TOP_K), lambda i: (i, 0, 0), memory_space=pltpu.SMEM),
                  pl.BlockSpec(memory_space=pl.ANY),
                  pl.BlockSpec((tile, TOP_K), lambda i: (i, 0)),
                  pl.BlockSpec((tile, d), lambda i: (i, 0)),
                  pl.BlockSpec((1, d), lambda i: (0, 0))],
        out_specs=[pl.BlockSpec((tile, d), lambda i: (jnp.minimum(i, tiles_a - 1), 0)),
                   pl.BlockSpec((tile, d), lambda i: (jnp.maximum(i - tiles_a, 0), 0))],
        out_shape=[jax.ShapeDtypeStruct((n_a, d), F32), jax.ShapeDtypeStruct((n - n_a, d), F32)],
        scratch_shapes=[pltpu.VMEM((TOP_K, tile, d), F32), pltpu.SemaphoreType.DMA],
        compiler_params=_params(("arbitrary",)),
        name="moe_combine",
    )(dest.reshape(nt, 1, tile * TOP_K), ys, gate, x2, final_g.reshape(1, d).astype(F32))


def _moe(h2, x2, idx, gate, rank, counts, lp, final_g, n_a):
    n, d = h2.shape
    bm = EXPERT_ROWS
    counts = counts.reshape(N_EXPERTS)
    padded = (counts + bm - 1) // bm * bm
    pad_end = jnp.cumsum(padded)
    dest = ((pad_end - padded)[idx] + rank).reshape(n * TOP_K).astype(jnp.int32)
    n_blocks = -(-(n * TOP_K) // bm) + N_EXPERTS
    block_start = jnp.arange(n_blocks, dtype=jnp.int32) * bm
    block_expert = jnp.minimum(jnp.sum(pad_end[None, :] <= block_start[:, None], axis=1),
                               N_EXPERTS - 1).astype(jnp.int32)
    n_used = (pad_end[-1:] // bm).astype(jnp.int32)
    xs = _dispatch(h2, dest, n_blocks * bm, DISPATCH_TILE)
    ys = _experts(xs, block_expert, n_used, lp['w_gate_up'].astype(BF16), lp['b_gate_up'].astype(F32),
                  lp['w_down'].astype(BF16), lp['b_down'].astype(F32), bm)
    return _combine(ys, dest, gate, x2, final_g, COMBINE_TILE, n_a)


def _to_heads(t, b, s):
    return t.reshape(b, s, SB_HEADS, SB_HEAD_DIM).transpose(0, 2, 1, 3)


def _from_heads(t):
    b, h, s, dh = t.shape
    return t.transpose(0, 2, 1, 3).reshape(b, s, h * dh)


def _layer(x_all, bp, tp, bs, ts, cache_k, cache_v, state_rwkv, state_shift, lp, final_g):
    n, d = x_all.shape
    sbw = SB_HEADS * SB_HEAD_DIM
    rw_proj = lp['rw_mu'].shape[0]
    n_p = bp * tp
    qkv, rw, gab = _inproj(x_all, lp['norm1_g'].astype(F32), lp['w_in'].astype(BF16),
                           (3 * sbw, rw_proj, 2 * d), ROW_TILE)
    scale = LOG2E * SB_HEAD_DIM ** -0.5

    def split(rows, b, s):
        tok = lambda c: rows[:, c * sbw:(c + 1) * sbw].reshape(b, s, sbw)
        return (tok(0) * scale).astype(BF16), tok(1), tok(2)

    qp, kp, vp = split(qkv[:n_p], bp, tp)
    oa_p = _stick_breaking(qp, kp.astype(BF16), vp.astype(BF16), 0, SB_BLOCK, SB_BLOCK)
    qs, ks, vs = split(qkv[n_p:], bs, ts)
    past = cache_k.shape[2]
    k_all = jnp.concatenate([_from_heads(cache_k), ks], axis=1).astype(BF16)
    v_all = jnp.concatenate([_from_heads(cache_v), vs], axis=1).astype(BF16)
    oa_s = _stick_breaking(qs, k_all, v_all, past, ts, min(SB_BLOCK, past))
    oa = jnp.concatenate([oa_p.reshape(n_p, sbw), oa_s.reshape(bs * ts, sbw)], axis=0)

    s0_p = jnp.zeros((bp, RW_HEADS, RW_HEAD_DIM, RW_HEAD_DIM), F32)
    prev_p = jnp.zeros((bp, 1, rw_proj), F32)
    ob, st_p = _rwkv(rw, bp, tp, tp, prev_p, s0_p, lp, jnp.zeros((n, RW_WIDTH), BF16))
    rw_s = rw[n_p:].reshape(bs, ts, rw_proj)
    ts_pad = -(-ts // RW_CHUNK) * RW_CHUNK
    rw_s_pad = jnp.pad(rw_s, ((0, 0), (0, ts_pad - ts), (0, 0))).reshape(bs * ts_pad, rw_proj)
    ob_s, st_s = _rwkv(rw_s_pad, bs, ts_pad, ts, state_shift, state_rwkv, lp,
                       jnp.zeros((bs * ts_pad, RW_WIDTH), BF16))
    ob = lax.dynamic_update_slice(
        ob, ob_s.reshape(bs, ts_pad, RW_WIDTH)[:, :ts].reshape(bs * ts, RW_WIDTH), (n_p, 0))
    sh_p = rw[:n_p].reshape(bp, tp, rw_proj)[:, tp - 1:]
    sh_s = rw_s[:, ts - 1:]

    x2, h2, idx, gate, rank, counts = _merge_route(oa, ob, gab, x_all, lp, ROW_TILE)
    y = _moe(h2, x2, idx, gate, rank, counts, lp, final_g, n_p)
    return (y, _to_heads(kp, bp, tp), _to_heads(vp, bp, tp), st_p, sh_p,
            _to_heads(ks, bs, ts), _to_heads(vs, bs, ts), st_s, sh_s)


def kernel(x_prompt, x_sample, cache_sb_k, cache_sb_v, state_rwkv, state_shift, norm1_g, w_in, rw_mu, rw_w0, rw_w2, rw_a0, rw_a2, rw_g2, rw_k_k, rw_k_a, rw_r_k, rw_ln_g, rw_ln_b, w_branch_a, w_branch_b, w_out, norm2_g, w_router, b_router, w_gate_up, b_gate_up, w_down, b_down, final_norm_g):
    depth = w_in.shape[0]
    assert depth == 1, "the final RMSNorm is fused into the last (only) layer"
    bp, tp, d = x_prompt.shape
    bs, ts, _ = x_sample.shape
    n_p = bp * tp
    x_all = jnp.concatenate([x_prompt.reshape(n_p, d), x_sample.reshape(bs * ts, d)], axis=0)
    lp = dict(norm1_g=norm1_g[0], w_in=w_in[0], rw_mu=rw_mu[0], rw_w0=rw_w0[0], rw_w2=rw_w2[0], rw_a0=rw_a0[0],
              rw_a2=rw_a2[0], rw_g2=rw_g2[0], rw_k_k=rw_k_k[0], rw_k_a=rw_k_a[0], rw_r_k=rw_r_k[0].reshape(-1),
              rw_ln_g=rw_ln_g[0], rw_ln_b=rw_ln_b[0], w_branch_a=w_branch_a[0], w_branch_b=w_branch_b[0],
              w_out=w_out[0], norm2_g=norm2_g[0], w_router=w_router[0], b_router=b_router[0],
              w_gate_up=w_gate_up[0], b_gate_up=b_gate_up[0], w_down=w_down[0], b_down=b_down[0])
    y, kp, vp, st_p, sh_p, ks, vs, st_s, sh_s = _layer(
        x_all, bp, tp, bs, ts, cache_sb_k[0], cache_sb_v[0], state_rwkv[0], state_shift[0], lp, final_norm_g)
    return (y[0].reshape(bp, tp, d), y[1].reshape(bs, ts, d),
            kp[None], vp[None], st_p[None], sh_p[None], ks[None], vs[None], st_s[None], sh_s[None])
```

```python
import functools

import jax
import jax.numpy as jnp
from jax import lax
from jax.experimental import pallas as pl
from jax.experimental.pallas import tpu as pltpu

F32 = jnp.float32
BF16 = jnp.bfloat16
HIGHEST = lax.Precision.HIGHEST

SB_HEADS = 8
SB_HEAD_DIM = 64
RW_HEADS = 8
RW_HEAD_DIM = 64
RW_WIDTH = RW_HEADS * RW_HEAD_DIM
RW_DECAY_LORA = 64
RW_AAA_LORA = 64
RW_GATE_LORA = 128
RW_GN_EPS = 64e-5
N_EXPERTS = 32
TOP_K = 4
SWIGLU_LIMIT = 7.0
SWIGLU_ALPHA = 1.702
RMS_EPS = 1e-6

VMEM_LIMIT_BYTES = 56 * 1024 * 1024
ROW_TILE = 256
MERGE_TILE = 512
SB_BLOCK = 512
SB_SUB = 256
LOG2E = 1.4426950408889634
SB_DEAD_LOG2 = -200.0
RW_CHUNK = 64
RW_PREP_CHUNKS = 4
RW_SCAN_CHUNKS = 4
EXPERT_ROWS = 512
DISPATCH_TILE = 512
COMBINE_TILE = 128
DMA_UNROLL = 8


def _params(semantics):
    return pltpu.CompilerParams(dimension_semantics=semantics, vmem_limit_bytes=VMEM_LIMIT_BYTES)


def _dot(a, b, precision=None):
    return jnp.dot(a, b, preferred_element_type=F32, precision=precision)


def _dot_nt(a, b, precision=None):
    return lax.dot_general(a, b, (((1,), (1,)), ((), ())), preferred_element_type=F32, precision=precision)


def _dot_tn(a, b, precision=None):
    return lax.dot_general(a, b, (((0,), (0,)), ((), ())), preferred_element_type=F32, precision=precision)


def _softplus_parts(z):
    l = jnp.log1p(jnp.exp(-jnp.abs(z)))
    sp = jnp.maximum(z, 0.0) + l
    return sp, sp - z


def _rmsnorm(x, g):
    return x * lax.rsqrt(jnp.mean(x * x, axis=-1, keepdims=True) + RMS_EPS) * g


def _inproj_body(x_ref, g_ref, w_ref, qkv16_ref, kv_ref, rw_ref, gab_ref, *, q_scale):
    h = _rmsnorm(x_ref[...], g_ref[...]).astype(BF16)
    sbw = kv_ref.shape[-1] // 2
    q = _dot(h, w_ref[:, 0:sbw])
    qkv16_ref[:, 0:sbw] = (q * q_scale).astype(BF16)
    for c in (1, 2):
        kv = _dot(h, w_ref[:, c * sbw:(c + 1) * sbw])
        kv_ref[:, (c - 1) * sbw:c * sbw] = kv
        qkv16_ref[:, c * sbw:(c + 1) * sbw] = kv.astype(BF16)
    off = 3 * sbw
    for ref in (rw_ref, gab_ref):
        width = ref.shape[-1]
        for c0 in range(0, width, sbw):
            cw = min(sbw, width - c0)
            ref[:, c0:c0 + cw] = _dot(h, w_ref[:, off + c0:off + c0 + cw])
        off += width


def _inproj(x, g, w_bf16, sbw, rw_proj, q_scale, tm):
    n, d = x.shape
    total = w_bf16.shape[1]
    widths = (3 * sbw, 2 * sbw, rw_proj, 2 * d)
    dtypes = (BF16, F32, F32, F32)
    assert sum(widths) - 2 * sbw == total
    return pl.pallas_call(
        functools.partial(_inproj_body, q_scale=q_scale),
        grid=(n // tm,),
        in_specs=[
            pl.BlockSpec((tm, d), lambda i: (i, 0)),
            pl.BlockSpec((1, d), lambda i: (0, 0)),
            pl.BlockSpec((d, total), lambda i: (0, 0), pipeline_mode=pl.Buffered(1)),
        ],
        out_specs=[pl.BlockSpec((tm, wd), lambda i: (i, 0)) for wd in widths],
        out_shape=[jax.ShapeDtypeStruct((n, wd), dt) for wd, dt in zip(widths, dtypes)],
        compiler_params=_params(("parallel",)),
        name="inproj",
    )(x, g.reshape(1, d), w_bf16)


def _sb_tile(q, kb, vb, ntri, carry, mask, sb):
    tk = kb.shape[0]
    z = _dot_nt(q, kb)
    neg_abs = lax.bitcast_convert_type(lax.bitcast_convert_type(z, jnp.uint32) | jnp.uint32(0x80000000), F32)
    sp = jnp.maximum(z, 0.0) + jnp.log(1.0 + jnp.exp2(neg_abs)) * LOG2E
    if mask is not None:
        sp = jnp.where(mask, sp, 0.0)
    hi = sp.astype(BF16)
    lo = (sp - hi.astype(F32)).astype(BF16)
    log_beta = z - sp
    wts = []
    for j in range(tk // sb - 1, -1, -1):
        cols = slice(j * sb, (j + 1) * sb)
        later = _dot(hi[:, cols], ntri) + _dot(lo[:, cols], ntri) + carry
        w = jnp.exp2(log_beta[:, cols] + later)
        if mask is not None:
            w = jnp.where(mask[:, cols], w, 0.0)
        wts.append(w.astype(BF16))
        carry = carry - jnp.sum(sp[:, cols], axis=-1, keepdims=True)
    wts = wts[0] if len(wts) == 1 else jnp.concatenate(wts[::-1], axis=1)
    return _dot(wts, vb), carry


def _sb_body(q_ref, k_ref, v_ref, triq_ref, trik_ref, o_in_ref, o_ref, *, tq, tk, past):
    del o_in_ref
    qi = pl.program_id(2)
    q = q_ref[...]
    first = lax.broadcasted_iota(jnp.int32, q.shape, 1) < SB_HEAD_DIM
    zero = jnp.zeros_like(q)
    qs = (jnp.where(first, q, zero), jnp.where(first, zero, q))
    d0 = pl.multiple_of(past + qi * tq, tq)
    row = lax.broadcasted_iota(jnp.int32, (tq, tq), 0)
    col = lax.broadcasted_iota(jnp.int32, (tq, tq), 1)
    kd = k_ref[pl.ds(d0, tq), :]
    vd = v_ref[pl.ds(d0, tq), :]
    sbq = triq_ref.shape[0]
    state = []
    for qh in qs:
        state.extend(_sb_tile(qh, kd, vd, triq_ref[...], jnp.zeros((tq, 1), F32), col < row, sbq))
    nb = (past + qi * tq) // tk
    sbk = trik_ref.shape[0]

    def live(state):
        return jnp.max(jnp.maximum(state[1], state[3])) > SB_DEAD_LOG2

    def cond(loop):
        i, alive, _ = loop
        return jnp.logical_and(i < nb, alive)

    def step(loop):
        i, _, state = loop
        s0 = pl.multiple_of((nb - 1 - i) * tk, tk)
        kb = k_ref[pl.ds(s0, tk), :]
        vb = v_ref[pl.ds(s0, tk), :]
        new = []
        for h, qh in enumerate(qs):
            out, carry = _sb_tile(qh, kb, vb, trik_ref[...], state[2 * h + 1], None, sbk)
            new.extend((state[2 * h] + out, carry))
        return i + 1, live(new), tuple(new)

    _, _, state = lax.while_loop(cond, step, (jnp.int32(0), live(state), tuple(state)))
    o_ref[...] = jnp.where(first, state[0], state[2]).astype(o_ref.dtype)


def _later_matrix(n):
    r = lax.broadcasted_iota(jnp.int32, (n, n), 0)
    c = lax.broadcasted_iota(jnp.int32, (n, n), 1)
    return jnp.where(r > c, -1.0, 0.0).astype(BF16)


def _stick_breaking(q, q_row0, q_lane0, kv, k_lane0, v_lane0, batch, t, past, tq, tk, o_prev):
    s = past + t
    pair = 2 * SB_HEAD_DIM
    sbq = min(tq, SB_SUB)
    sbk = min(tk, SB_SUB)
    nq = t // tq
    qblk = q_row0 // tq
    q_spec = lambda lane0: pl.BlockSpec((tq, pair), lambda b, h, i: (qblk + b * nq + i, lane0 + h))
    kv_spec = lambda lane0: pl.BlockSpec((s, pair), lambda b, h, i: (b, lane0 + h))
    return pl.pallas_call(
        functools.partial(_sb_body, tq=tq, tk=tk, past=past),
        grid=(batch, SB_HEADS // 2, nq),
        in_specs=[
            q_spec(q_lane0), kv_spec(k_lane0), kv_spec(v_lane0),
            pl.BlockSpec((sbq, sbq), lambda b, h, i: (0, 0)),
            pl.BlockSpec((sbk, sbk), lambda b, h, i: (0, 0)),
            pl.BlockSpec(memory_space=pl.ANY),
        ],
        out_specs=q_spec(0),
        out_shape=jax.ShapeDtypeStruct(o_prev.shape, BF16),
        input_output_aliases={5: 0},
        compiler_params=_params(("parallel", "parallel", "arbitrary")),
        name="stick_breaking",
    )(q, kv, kv, _later_matrix(sbq), _later_matrix(sbk), o_prev)


def _split_bf16(x):
    hi = x.astype(BF16)
    return hi, (x - hi.astype(F32)).astype(BF16)


def _dot3(a, b, dot=_dot):
    ah, al = _split_bf16(a)
    bh, bl = _split_bf16(b)
    return dot(ah, bh) + dot(ah, bl) + dot(al, bh)


def _dot_ones_rhs(x, ones_bf16):
    hi, lo = _split_bf16(x)
    return _dot(hi, ones_bf16) + _dot(lo, ones_bf16)


def _dot_ones_lhs(ones_bf16, x):
    hi, lo = _split_bf16(x)
    return _dot(ones_bf16, hi) + _dot(ones_bf16, lo)


HEADS_PER_MATMUL = 4
GROUP_WIDTH = HEADS_PER_MATMUL * RW_HEAD_DIM


def _block_diag(y, mask):
    return jnp.concatenate([y] * HEADS_PER_MATMUL, axis=0) * mask


def _bdmm(xs, ys, mask, dot=_dot):
    xh, xl = xs
    yh, yl = ys
    outs = []
    for g in range(RW_WIDTH // GROUP_WIDTH):
        ls = slice(g * GROUP_WIDTH, (g + 1) * GROUP_WIDTH)
        wh = _block_diag(yh[:, ls], mask)
        wl = _block_diag(yl[:, ls], mask)
        outs.append(dot(xh[:, ls], wh) + dot(xl[:, ls], wh) + dot(xh[:, ls], wl))
    return jnp.concatenate(outs, axis=1)


def _bd_tn(x, y, mask):
    n = RW_HEAD_DIM
    outs = []
    for g in range(RW_WIDTH // GROUP_WIDTH):
        ls = slice(g * GROUP_WIDTH, (g + 1) * GROUP_WIDTH)
        z = _dot3(x[:, ls].T, y[:, ls]) * mask
        outs.append(z[0:n] + z[n:2 * n] + z[2 * n:3 * n] + z[3 * n:4 * n])
    return jnp.concatenate(outs, axis=1)


def _rwkv_prep_body(p_ref, pb_ref, prev_ref, mu_ref, w0_ref, w2_ref, a0_ref, a2_ref, g2_ref, kk_ref, ka_ref,
                    rk_ref, tri_ref, bd_ref, mbf_ref, mf32_ref,
                    rhat_ref, y0_ref, a_ref, d_ref, bonus_ref, g_ref, *, chunk, groups, n_sq, t_valid):
    n = RW_HEAD_DIM
    w_ = RW_WIDTH
    p = p_ref[...]
    rows = lax.broadcasted_iota(jnp.int32, p.shape, 0)
    prev_row = jnp.where(pl.program_id(1) == 0, prev_ref[0], pb_ref[7:8, :])
    p_shift = jnp.where(rows == 0, prev_row, pltpu.roll(p, 1, 0))

    xs = p + (p_shift - p) * mu_ref[...]
    r = xs[:, 0:w_]
    k = xs[:, w_:2 * w_]
    v = xs[:, 2 * w_:3 * w_]
    o1 = 3 * w_
    dw = xs[:, o1:o1 + RW_DECAY_LORA]
    da = xs[:, o1 + RW_DECAY_LORA:o1 + RW_DECAY_LORA + RW_AAA_LORA]
    dg = xs[:, o1 + RW_DECAY_LORA + RW_AAA_LORA:]

    sp_neg = _softplus_parts(w0_ref[...] + _dot3(jnp.tanh(dw), w2_ref[...]))[1]
    log_decay = -jnp.exp(-sp_neg - 0.5)
    a = jax.nn.sigmoid(a0_ref[...] + _dot3(da, a2_ref[...]))
    g_ref[...] = _dot(jax.nn.sigmoid(dg).astype(BF16), g2_ref[...])

    bd = bd_ref[...]
    kk = k * kk_ref[...]
    kk = kk / jnp.maximum(jnp.sqrt(_dot_ones_rhs(kk * kk, bd)), 1e-12)
    k2 = k * (1.0 + (a - 1.0) * ka_ref[...])
    a_vec = -kk
    b_vec = kk * a
    bonus_ref[...] = _dot_ones_rhs(r * k2 * rk_ref[...], bd) * v
    if t_valid < chunk:
        live = lax.broadcasted_iota(jnp.int32, r.shape, 0) < t_valid
        keep = lambda x: jnp.where(live, x, 0.0)
        log_decay, r, k2, v, a_vec, b_vec = (keep(x) for x in (log_decay, r, k2, v, a_vec, b_vec))

    cw = _dot_ones_lhs(tri_ref[...], log_decay)
    cw_last = jnp.concatenate(
        [jnp.broadcast_to(cw[(j + 1) * chunk - 1:(j + 1) * chunk, :], (chunk, w_)) for j in range(groups)], axis=0)
    r_t = r * jnp.exp(cw)
    a_t = a_vec * jnp.exp(cw - log_decay)
    e_neg = jnp.exp(-cw)
    b_t = b_vec * e_neg
    k_t = k2 * e_neg
    e_rel = jnp.exp(cw_last - cw)
    b_h = b_vec * e_rel
    k_h = k2 * e_rel
    w_c = jnp.exp(cw_last)

    rr = lax.broadcasted_iota(jnp.int32, (chunk, w_), 0)
    ss = lax.broadcasted_iota(jnp.int32, (chunk, w_), 1) % n
    strict = ss < rr
    incl = ss <= rr
    eye = (ss == rr).astype(F32)
    m_bf16 = mbf_ref[...]
    m_f32 = mf32_ref[...]

    for j in range(groups):
        rs = slice(j * chunk, (j + 1) * chunk)
        v_j = v[rs]
        v_s = _split_bf16(v_j)
        ar_s = _split_bf16(jnp.concatenate([a_t[rs], r_t[rs]], axis=0))
        gram_b = _bdmm(ar_s, _split_bf16(b_t[rs]), m_bf16, _dot_nt)
        gram_k = _bdmm(ar_s, _split_bf16(k_t[rs]), m_bf16, _dot_nt)
        l_ab = jnp.where(strict, gram_b[:chunk], 0.0)
        l_ak = jnp.where(strict, gram_k[:chunk], 0.0)
        m_rb_s = _split_bf16(jnp.where(incl, gram_b[chunk:], 0.0))
        m_rk_s = _split_bf16(jnp.where(incl, gram_k[chunk:], 0.0))
        l_s = _split_bf16(l_ab)
        t_inv = eye + l_ab
        l_pow = _bdmm(l_s, l_s, m_bf16)
        for i in range(n_sq):
            lp_s = _split_bf16(l_pow)
            t_inv = t_inv + _bdmm(lp_s, _split_bf16(t_inv), m_bf16)
            if i + 1 < n_sq:
                l_pow = _bdmm(lp_s, lp_s, m_bf16)
        t_s = _split_bf16(t_inv)
        p_mat = _bdmm(t_s, _split_bf16(a_t[rs]), m_bf16)
        q_mat = _bdmm(t_s, _split_bf16(_bdmm(_split_bf16(l_ak), v_s, m_bf16)), m_bf16)
        rhat_ref[rs, :] = r_t[rs] + _bdmm(m_rb_s, _split_bf16(p_mat), m_bf16)
        y0_ref[rs, :] = _bdmm(m_rb_s, _split_bf16(q_mat), m_bf16) + _bdmm(m_rk_s, v_s, m_bf16)
        ns = slice(j * n, (j + 1) * n)
        a_ref[ns, :] = _bd_tn(p_mat, b_h[rs], m_f32) + eye * w_c[j * chunk:j * chunk + 1, :]
        d_ref[ns, :] = _bd_tn(jnp.concatenate([q_mat, v_j], axis=0),
                              jnp.concatenate([b_h[rs], k_h[rs]], axis=0), m_f32)


def _rwkv_scan_body(rhat_ref, y0_ref, a_ref, d_ref, bonus_ref, g_ref, s0_ref, lng_ref, lnb_ref, bd_ref,
                    o_ref, s_ref, y_scr, *, chunk, groups):
    n = RW_HEAD_DIM

    @pl.when(pl.program_id(1) == 0)
    def _():
        s_ref[...] = s0_ref[...]

    for j in range(groups):
        rs = slice(j * chunk, (j + 1) * chunk)
        ns = slice(j * n, (j + 1) * n)
        for h in range(RW_HEADS):
            sl = slice(h * n, (h + 1) * n)
            s_h = s_ref[0, h]
            y_scr[rs, sl] = y0_ref[rs, sl] + _dot_nt(rhat_ref[rs, sl], s_h, HIGHEST)
            s_ref[0, h] = _dot(s_h, a_ref[ns, sl], HIGHEST) + d_ref[ns, sl]

    bd = bd_ref[...]
    y = y_scr[...]
    mean = _dot_ones_rhs(y, bd) * (1.0 / n)
    yc = y - mean
    var = _dot_ones_rhs(yc * yc, bd) * (1.0 / n)
    yn = yc * lax.rsqrt(var + RW_GN_EPS) * lng_ref[...] + lnb_ref[...]
    o_ref[...] = ((yn + bonus_ref[...]) * g_ref[...]).astype(o_ref.dtype)


def _rwkv(rw_all, batch, t, t_valid, prev, s0, lp, o_prev):
    proj = rw_all.shape[1]
    n_all = o_prev.shape[0]
    row_off = 0
    n = RW_HEAD_DIM
    chunk = RW_CHUNK
    assert chunk == n and t % chunk == 0 and (t_valid == t or t == chunk)
    n_sq = chunk.bit_length() - 2
    nc = t // chunk
    gp = min(RW_PREP_CHUNKS, nc)
    gs = min(RW_SCAN_CHUNKS, nc)
    rows_p = gp * chunk
    rows_s = gs * chunk
    n_tok = batch * t
    row = lambda x: x.reshape(1, -1).astype(F32)
    r = lax.broadcasted_iota(jnp.int32, (rows_p, rows_p), 0)
    c = lax.broadcasted_iota(jnp.int32, (rows_p, rows_p), 1)
    tri = ((c <= r) & (c // chunk == r // chunk)).astype(BF16)
    hr = lax.broadcasted_iota(jnp.int32, (RW_WIDTH, RW_WIDTH), 0) // n
    hc = lax.broadcasted_iota(jnp.int32, (RW_WIDTH, RW_WIDTH), 1) // n
    bd = (hr == hc).astype(BF16)
    group_mask = (hr == hc)[:GROUP_WIDTH, :GROUP_WIDTH]
    const = lambda shape: pl.BlockSpec(shape, lambda b, c: (0,) * len(shape))

    blk_p = row_off // rows_p
    prep_args = [rw_all, rw_all, prev.astype(F32), row(lp['rw_mu']), row(lp['rw_w0']), lp['rw_w2'].astype(F32),
                 row(lp['rw_a0']), lp['rw_a2'].astype(F32), lp['rw_g2'].astype(BF16), row(lp['rw_k_k']),
                 row(lp['rw_k_a']), row(lp['rw_r_k']), tri, bd, group_mask.astype(BF16), group_mask.astype(F32)]
    tok_spec = lambda rows: pl.BlockSpec((rows, RW_WIDTH), lambda b, c: (b * (t // rows) + c, 0))
    mat_spec = lambda g: pl.BlockSpec((g * n, RW_WIDTH), lambda b, c: (b * (nc // g) + c, 0))
    tok_shape = jax.ShapeDtypeStruct((n_tok, RW_WIDTH), F32)
    mat_shape = jax.ShapeDtypeStruct((batch * nc * n, RW_WIDTH), F32)
    rhat, y0, a_mat, d_mat, bonus, g = pl.pallas_call(
        functools.partial(_rwkv_prep_body, chunk=chunk, groups=gp, n_sq=n_sq, t_valid=t_valid),
        grid=(batch, nc // gp),
        in_specs=[
            pl.BlockSpec((rows_p, proj), lambda b, c: (blk_p + b * (nc // gp) + c, 0)),
            pl.BlockSpec((8, proj), lambda b, c: (jnp.maximum((row_off + b * t + c * rows_p) // 8 - 1, 0), 0)),
            pl.BlockSpec((1, 1, proj), lambda b, c: (b, 0, 0)),
        ] + [const(x.shape) for x in prep_args[3:]],
        out_specs=[tok_spec(rows_p), tok_spec(rows_p), mat_spec(gp), mat_spec(gp), tok_spec(rows_p),
                   tok_spec(rows_p)],
        out_shape=[tok_shape, tok_shape, mat_shape, mat_shape, tok_shape, tok_shape],
        compiler_params=_params(("parallel", "parallel")),
        name="rwkv7_prep",
    )(*prep_args)

    blk_s = row_off // rows_s
    scan_args = [rhat, y0, a_mat, d_mat, bonus, g, s0.astype(F32), row(lp['rw_ln_g']), row(lp['rw_ln_b']), bd,
                 o_prev]
    state_spec = pl.BlockSpec((1, RW_HEADS, n, n), lambda b, c: (b, 0, 0, 0))

    def scan_body(*refs):
        _rwkv_scan_body(*refs[:10], *refs[11:], chunk=chunk, groups=gs)

    o, state = pl.pallas_call(
        scan_body,
        grid=(batch, nc // gs),
        in_specs=[tok_spec(rows_s), tok_spec(rows_s), mat_spec(gs), mat_spec(gs), tok_spec(rows_s),
                  tok_spec(rows_s), state_spec, const((1, RW_WIDTH)), const((1, RW_WIDTH)), const(bd.shape),
                  pl.BlockSpec(memory_space=pl.ANY)],
        out_specs=[pl.BlockSpec((rows_s, RW_WIDTH), lambda b, c: (blk_s + b * (nc // gs) + c, 0)), state_spec],
        out_shape=[jax.ShapeDtypeStruct((n_all, RW_WIDTH), BF16),
                   jax.ShapeDtypeStruct((batch, RW_HEADS, n, n), F32)],
        scratch_shapes=[pltpu.VMEM((rows_s, RW_WIDTH), F32)],
        input_output_aliases={10: 0},
        compiler_params=_params(("parallel", "arbitrary")),
        name="rwkv7_scan",
    )(*scan_args)
    return o, state


def _merge_body(oa_ref, ob_ref, gab_ref, x_ref, wa_ref, wb_ref, wo_ref, g2_ref, wr_ref, br_ref, tri_ref,
                x2_ref, h2_ref, idx_ref, gate_ref, rank_ref, cnt_ref, carry_scr):
    i = pl.program_id(0)
    d = x_ref.shape[-1]

    @pl.when(i == 0)
    def _():
        carry_scr[...] = jnp.zeros_like(carry_scr)

    gab = gab_ref[...]
    merged = (jax.nn.sigmoid(gab[:, :d]) * _dot(oa_ref[...], wa_ref[...])
              + jax.nn.sigmoid(gab[:, d:]) * _dot(ob_ref[...], wb_ref[...]))
    x2 = x_ref[...] + _dot(merged.astype(BF16), wo_ref[...])
    x2_ref[...] = x2
    h2 = _rmsnorm(x2, g2_ref[...])
    h2_ref[...] = h2
    logits = _dot3(h2, wr_ref[...]) + br_ref[...]

    tm, ne = logits.shape
    col = lax.broadcasted_iota(jnp.int32, (tm, ne), 1)
    c4 = lax.broadcasted_iota(jnp.int32, (tm, TOP_K), 1)
    work = logits
    tops, idxs = [], []
    for _ in range(TOP_K):
        m = jnp.max(work, axis=-1, keepdims=True)
        ix = jnp.min(jnp.where(work == m, col, ne), axis=-1, keepdims=True)
        tops.append(m)
        idxs.append(ix)
        work = jnp.where(col == ix, -jnp.inf, work)
    es = [jnp.exp(m - tops[0]) for m in tops]
    denom = es[0] + es[1] + es[2] + es[3]
    onehot = jnp.zeros((tm, ne), F32)
    for ix in idxs:
        onehot = onehot + (col == ix).astype(F32)
    before = carry_scr[...] + _dot(tri_ref[...], onehot.astype(BF16))
    idx_out = jnp.zeros((tm, TOP_K), jnp.int32)
    gate_out = jnp.zeros((tm, TOP_K), F32)
    rank_out = jnp.zeros((tm, TOP_K), jnp.int32)
    for kk in range(TOP_K):
        rk = jnp.sum(jnp.where(col == idxs[kk], before, 0.0), axis=-1, keepdims=True)
        idx_out = jnp.where(c4 == kk, idxs[kk], idx_out)
        gate_out = jnp.where(c4 == kk, es[kk] / denom, gate_out)
        rank_out = jnp.where(c4 == kk, rk.astype(jnp.int32), rank_out)
    idx_ref[...] = idx_out
    gate_ref[...] = gate_out
    rank_ref[...] = rank_out
    carry = carry_scr[...] + jnp.sum(onehot, axis=0, keepdims=True)
    carry_scr[...] = carry
    cnt_ref[...] = carry.astype(jnp.int32)


def _merge_route(oa, ob, gab, x, lp, tm):
    n, d = x.shape
    ne = N_EXPERTS
    r = lax.broadcasted_iota(jnp.int32, (tm, tm), 0)
    c = lax.broadcasted_iota(jnp.int32, (tm, tm), 1)
    tri = (c < r).astype(BF16)
    const = lambda shape: pl.BlockSpec(shape, lambda i: (0,) * len(shape))
    rowblk = lambda wd: pl.BlockSpec((tm, wd), lambda i: (i, 0))
    wa = lp['w_branch_a'].astype(BF16)
    wb = lp['w_branch_b'].astype(BF16)
    wo = lp['w_out'].astype(BF16)
    return pl.pallas_call(
        _merge_body,
        grid=(n // tm,),
        in_specs=[rowblk(oa.shape[1]), rowblk(ob.shape[1]), rowblk(2 * d), rowblk(d),
                  const(wa.shape), const(wb.shape), const(wo.shape), const((1, d)), const((d, ne)),
                  const((1, ne)), const((tm, tm))],
        out_specs=[rowblk(d), rowblk(d), rowblk(TOP_K), rowblk(TOP_K), rowblk(TOP_K), const((1, ne))],
        out_shape=[jax.ShapeDtypeStruct((n, d), F32), jax.ShapeDtypeStruct((n, d), F32),
                   jax.ShapeDtypeStruct((n, TOP_K), jnp.int32), jax.ShapeDtypeStruct((n, TOP_K), F32),
                   jax.ShapeDtypeStruct((n, TOP_K), jnp.int32), jax.ShapeDtypeStruct((1, ne), jnp.int32)],
        scratch_shapes=[pltpu.VMEM((1, ne), F32)],
        compiler_params=_params(("arbitrary",)),
        name="merge_route",
    )(oa, ob, gab, x, wa, wb, wo, lp['norm2_g'].reshape(1, d).astype(F32), lp['w_router'].astype(F32),
      lp['b_router'].reshape(1, ne).astype(F32), tri)


def _dispatch_body(dest_ref, end_ref, padded_ref, h_ref, zeros_ref, xs_ref, sem, zsem, *, tile, bm):
    @pl.when(pl.program_id(0) == 0)
    def _():
        def clear(e):
            start = pl.multiple_of(end_ref[e] - bm, bm)
            return pltpu.make_async_copy(zeros_ref, xs_ref.at[pl.ds(start, bm)], zsem)

        for e in range(N_EXPERTS):
            @pl.when(padded_ref[e] > 0)
            def _():
                clear(e).start()

        for e in range(N_EXPERTS):
            @pl.when(padded_ref[e] > 0)
            def _():
                clear(e).wait()

    def copy(t, kk):
        return pltpu.make_async_copy(h_ref.at[pl.ds(t, 1)],
                                     xs_ref.at[pl.ds(dest_ref[0, 0, t * TOP_K + kk], 1)], sem)

    def issue(t, carry):
        for kk in range(TOP_K):
            copy(t, kk).start(priority=kk % 2)
        return carry

    def drain(t, carry):
        for kk in range(TOP_K):
            copy(t, kk).wait()
        return carry

    lax.fori_loop(0, tile, issue, 0, unroll=DMA_UNROLL)
    lax.fori_loop(0, tile, drain, 0, unroll=DMA_UNROLL)


def _dispatch(h2, dest, pad_end, padded, n_rows, tile, bm):
    n, d = h2.shape
    nt = n // tile
    smem = pl.BlockSpec(memory_space=pltpu.SMEM)
    return pl.pallas_call(
        functools.partial(_dispatch_body, tile=tile, bm=bm),
        grid=(nt,),
        in_specs=[pl.BlockSpec((1, 1, tile * TOP_K), lambda i: (i, 0, 0), memory_space=pltpu.SMEM), smem, smem,
                  pl.BlockSpec((tile, d), lambda i: (i, 0)), pl.BlockSpec(memory_space=pl.ANY)],
        out_specs=pl.BlockSpec(memory_space=pl.ANY),
        out_shape=jax.ShapeDtypeStruct((n_rows, d), F32),
        scratch_shapes=[pltpu.SemaphoreType.DMA, pltpu.SemaphoreType.DMA],
        compiler_params=_params(("arbitrary",)),
        name="moe_dispatch",
    )(dest.reshape(nt, 1, tile * TOP_K), pad_end.astype(jnp.int32), padded.astype(jnp.int32), h2,
      jnp.zeros((bm, d), F32))


def _expert_body(be_ref, used_ref, xs_ref, wgu_ref, bgu_ref, wd_ref, bd_ref, ys_ref, wgu16, wd16):
    i = pl.program_id(0)
    de = wd_ref.shape[1]

    @pl.when(jnp.logical_or(i == 0, be_ref[i] != be_ref[jnp.maximum(i - 1, 0)]))
    def _():
        wgu16[...] = wgu_ref[0].astype(BF16)
        wd16[...] = wd_ref[0].astype(BF16)

    @pl.when(i < used_ref[0])
    def _():
        gu = _dot(xs_ref[...].astype(BF16), wgu16[...]) + bgu_ref[0]
        g = jnp.minimum(gu[:, :de], SWIGLU_LIMIT)
        u = jnp.clip(gu[:, de:], -SWIGLU_LIMIT, SWIGLU_LIMIT)
        act = (u + 1.0) * (g * jax.nn.sigmoid(g * SWIGLU_ALPHA))
        ys_ref[...] = _dot(act.astype(BF16), wd16[...]) + bd_ref[0]

    @pl.when(i >= used_ref[0])
    def _():
        ys_ref[...] = jnp.zeros_like(ys_ref)


def _experts(xs, block_expert, n_used, wgu, bgu, wd, bd, bm):
    n_rows, d = xs.shape
    ne, _, de2 = wgu.shape
    nb = n_rows // bm
    grid_spec = pltpu.PrefetchScalarGridSpec(
        num_scalar_prefetch=2,
        grid=(nb,),
        in_specs=[
            pl.BlockSpec((bm, d), lambda i, be, nu: (i, 0)),
            pl.BlockSpec((1, d, de2), lambda i, be, nu: (be[i], 0, 0)),
            pl.BlockSpec((1, 1, de2), lambda i, be, nu: (be[i], 0, 0)),
            pl.BlockSpec((1, de2 // 2, d), lambda i, be, nu: (be[i], 0, 0)),
            pl.BlockSpec((1, 1, d), lambda i, be, nu: (be[i], 0, 0)),
        ],
        out_specs=pl.BlockSpec((bm, d), lambda i, be, nu: (i, 0)),
        scratch_shapes=[pltpu.VMEM((d, de2), BF16), pltpu.VMEM((de2 // 2, d), BF16)],
    )
    return pl.pallas_call(
        _expert_body,
        grid_spec=grid_spec,
        out_shape=jax.ShapeDtypeStruct((n_rows, d), F32),
        compiler_params=_params(("arbitrary",)),
        name="moe_experts",
    )(block_expert, n_used, xs, wgu, bgu.reshape(ne, 1, de2), wd, bd.reshape(ne, 1, d))


def _combine_body(dest_ref, ys_ref, gate_ref, x2_ref, g_ref, ya_ref, yb_ref, buf, sem, *, tile, tiles_a):
    def copy(t, kk):
        return pltpu.make_async_copy(ys_ref.at[pl.ds(dest_ref[0, 0, t * TOP_K + kk], 1)],
                                     buf.at[kk, pl.ds(t, 1)], sem)

    def issue(t, carry):
        for kk in range(TOP_K):
            copy(t, kk).start(priority=kk % 2)
        return carry

    def drain(t, carry):
        for kk in range(TOP_K):
            copy(t, kk).wait()
        return carry

    lax.fori_loop(0, tile, issue, 0, unroll=DMA_UNROLL)
    lax.fori_loop(0, tile, drain, 0, unroll=DMA_UNROLL)
    gate = gate_ref[...]
    moe = gate[:, 0:1] * buf[0]
    for kk in range(1, TOP_K):
        moe = moe + gate[:, kk:kk + 1] * buf[kk]
    y = _rmsnorm(x2_ref[...] + moe, g_ref[...])
    i = pl.program_id(0)

    @pl.when(i < tiles_a)
    def _():
        ya_ref[...] = y

    @pl.when(i >= tiles_a)
    def _():
        yb_ref[...] = y


def _combine(ys, dest, gate, x2, final_g, tile, n_a):
    n, d = x2.shape
    nt = n // tile
    tiles_a = n_a // tile
    return pl.pallas_call(
        functools.partial(_combine_body, tile=tile, tiles_a=tiles_a),
        grid=(nt,),
        in_specs=[pl.BlockSpec((1, 1, tile * TOP_K), lambda i: (i, 0, 0), memory_space=pltpu.SMEM),
                  pl.BlockSpec(memory_space=pl.ANY),
                  pl.BlockSpec((tile, TOP_K), lambda i: (i, 0)),
                  pl.BlockSpec((tile, d), lambda i: (i, 0)),
                  pl.BlockSpec((1, d), lambda i: (0, 0))],
        out_specs=[pl.BlockSpec((tile, d), lambda i: (jnp.minimum(i, tiles_a - 1), 0)),
                   pl.BlockSpec((tile, d), lambda i: (jnp.maximum(i - tiles_a, 0), 0))],
        out_shape=[jax.ShapeDtypeStruct((n_a, d), F32), jax.ShapeDtypeStruct((n - n_a, d), F32)],
        scratch_shapes=[pltpu.VMEM((TOP_K, tile, d), F32), pltpu.SemaphoreType.DMA],
        compiler_params=_params(("arbitrary",)),
        name="moe_combine",
    )(dest.reshape(nt, 1, tile * TOP_K), ys, gate, x2, final_g.reshape(1, d).astype(F32))


def _moe(h2, x2, idx, gate, rank, counts, lp, final_g, n_a):
    n, d = h2.shape
    bm = EXPERT_ROWS
    counts = counts.reshape(N_EXPERTS)
    padded = (counts + bm - 1) // bm * bm
    pad_end = jnp.cumsum(padded)
    dest = ((pad_end - padded)[idx] + rank).reshape(n * TOP_K).astype(jnp.int32)
    n_blocks = -(-(n * TOP_K) // bm) + N_EXPERTS
    block_start = jnp.arange(n_blocks, dtype=jnp.int32) * bm
    block_expert = jnp.minimum(jnp.sum(pad_end[None, :] <= block_start[:, None], axis=1),
                               N_EXPERTS - 1).astype(jnp.int32)
    n_used = (pad_end[-1:] // bm).astype(jnp.int32)
    xs = _dispatch(h2, dest, pad_end, padded, n_blocks * bm, DISPATCH_TILE, bm)
    ys = _experts(xs, block_expert, n_used, lp['w_gate_up'].astype(F32), lp['b_gate_up'].astype(F32),
                  lp['w_down'].astype(F32), lp['b_down'].astype(F32), bm)
    return _combine(ys, dest, gate, x2, final_g, COMBINE_TILE, n_a)


def _to_heads(t, b, s):
    return t.reshape(b, s, SB_HEADS, SB_HEAD_DIM).transpose(0, 2, 1, 3)


def _from_heads(t):
    b, h, s, dh = t.shape
    return t.transpose(0, 2, 1, 3).reshape(b, s, h * dh)


def _layer(x_all, bp, tp, bs, ts, cache_k, cache_v, state_rwkv, state_shift, lp, final_g):
    n, d = x_all.shape
    sbw = SB_HEADS * SB_HEAD_DIM
    rw_proj = lp['rw_mu'].shape[0]
    n_p = bp * tp
    scale = LOG2E * SB_HEAD_DIM ** -0.5
    qkv16, kv, rw, gab = _inproj(x_all, lp['norm1_g'].astype(F32), lp['w_in'].astype(BF16), sbw, rw_proj,
                                 scale, ROW_TILE)
    pairs = SB_HEADS // 2
    oa = _stick_breaking(qkv16, 0, 0, qkv16, pairs, 2 * pairs, bp, tp, 0, SB_BLOCK, SB_BLOCK,
                         jnp.zeros((n, sbw), BF16))
    past = cache_k.shape[2]
    new16 = qkv16[n_p:].reshape(bs, ts, 3 * sbw)
    kv_s = jnp.concatenate([
        jnp.concatenate([_from_heads(cache_k).astype(BF16), new16[:, :, sbw:2 * sbw]], axis=1),
        jnp.concatenate([_from_heads(cache_v).astype(BF16), new16[:, :, 2 * sbw:]], axis=1)], axis=2)
    oa = _stick_breaking(qkv16, n_p, 0, kv_s.reshape(bs * (past + ts), 2 * sbw), 0, pairs, bs, ts, past, ts,
                         min(SB_BLOCK, past), oa)
    kp, vp = kv[:n_p, :sbw], kv[:n_p, sbw:]
    ks, vs = kv[n_p:, :sbw], kv[n_p:, sbw:]

    s0_p = jnp.zeros((bp, RW_HEADS, RW_HEAD_DIM, RW_HEAD_DIM), F32)
    prev_p = jnp.zeros((bp, 1, rw_proj), F32)
    ob, st_p = _rwkv(rw, bp, tp, tp, prev_p, s0_p, lp, jnp.zeros((n, RW_WIDTH), BF16))
    rw_s = rw[n_p:].reshape(bs, ts, rw_proj)
    ts_pad = -(-ts // RW_CHUNK) * RW_CHUNK
    rw_s_pad = jnp.pad(rw_s, ((0, 0), (0, ts_pad - ts), (0, 0))).reshape(bs * ts_pad, rw_proj)
    ob_s, st_s = _rwkv(rw_s_pad, bs, ts_pad, ts, state_shift, state_rwkv, lp,
                       jnp.zeros((bs * ts_pad, RW_WIDTH), BF16))
    ob = lax.dynamic_update_slice(
        ob, ob_s.reshape(bs, ts_pad, RW_WIDTH)[:, :ts].reshape(bs * ts, RW_WIDTH), (n_p, 0))
    sh_p = jnp.stack([rw[(b + 1) * tp - 1:(b + 1) * tp] for b in range(bp)])
    sh_s = rw_s[:, ts - 1:]

    x2, h2, idx, gate, rank, counts = _merge_route(oa, ob, gab, x_all, lp, MERGE_TILE)
    y = _moe(h2, x2, idx, gate, rank, counts, lp, final_g, n_p)
    return (y, _to_heads(kp, bp, tp), _to_heads(vp, bp, tp), st_p, sh_p,
            _to_heads(ks, bs, ts), _to_heads(vs, bs, ts), st_s, sh_s)


def kernel(x_prompt, x_sample, cache_sb_k, cache_sb_v, state_rwkv, state_shift, norm1_g, w_in, rw_mu, rw_w0, rw_w2, rw_a0, rw_a2, rw_g2, rw_k_k, rw_k_a, rw_r_k, rw_ln_g, rw_ln_b, w_branch_a, w_branch_b, w_out, norm2_g, w_router, b_router, w_gate_up, b_gate_up, w_down, b_down, final_norm_g):
    depth = w_in.shape[0]
    assert depth == 1, "the final RMSNorm is fused into the last (only) layer"
    bp, tp, d = x_prompt.shape
    bs, ts, _ = x_sample.shape
    n_p = bp * tp
    x_all = jnp.concatenate([x_prompt.reshape(n_p, d), x_sample.reshape(bs * ts, d)], axis=0)
    lp = dict(norm1_g=norm1_g[0], w_in=w_in[0], rw_mu=rw_mu[0], rw_w0=rw_w0[0], rw_w2=rw_w2[0], rw_a0=rw_a0[0],
              rw_a2=rw_a2[0], rw_g2=rw_g2[0], rw_k_k=rw_k_k[0], rw_k_a=rw_k_a[0], rw_r_k=rw_r_k[0].reshape(-1),
              rw_ln_g=rw_ln_g[0], rw_ln_b=rw_ln_b[0], w_branch_a=w_branch_a[0], w_branch_b=w_branch_b[0],
              w_out=w_out[0], norm2_g=norm2_g[0], w_router=w_router[0], b_router=b_router[0],
              w_gate_up=w_gate_up[0], b_gate_up=b_gate_up[0], w_down=w_down[0], b_down=b_down[0])
    y, kp, vp, st_p, sh_p, ks, vs, st_s, sh_s = _layer(
        x_all, bp, tp, bs, ts, cache_sb_k[0], cache_sb_v[0], state_rwkv[0], state_shift[0], lp, final_norm_g)
    return (y[0].reshape(bp, tp, d), y[1].reshape(bs, ts, d),
            kp[None], vp[None], st_p[None], sh_p[None], ks[None], vs[None], st_s[None], sh_s[None])
```

```python
import functools

import jax
import jax.numpy as jnp
from jax import lax
from jax.experimental import pallas as pl
from jax.experimental.pallas import tpu as pltpu

F32 = jnp.float32
BF16 = jnp.bfloat16
HIGHEST = lax.Precision.HIGHEST

SB_HEADS = 8
SB_HEAD_DIM = 64
RW_HEADS = 8
RW_HEAD_DIM = 64
RW_WIDTH = RW_HEADS * RW_HEAD_DIM
RW_DECAY_LORA = 64
RW_AAA_LORA = 64
RW_GATE_LORA = 128
RW_GN_EPS = 64e-5
N_EXPERTS = 32
TOP_K = 4
SWIGLU_LIMIT = 7.0
SWIGLU_ALPHA = 1.702
RMS_EPS = 1e-6

VMEM_LIMIT_BYTES = 56 * 1024 * 1024
ROW_TILE = 256
MERGE_TILE = 512
SB_BLOCK = 512
SB_SUB = 256
LOG2E = 1.4426950408889634
SB_DEAD_LOG2 = -200.0
RW_CHUNK = 64
RW_PREP_CHUNKS = 4
RW_SCAN_CHUNKS = 4
EXPERT_ROWS = 512
DISPATCH_TILE = 512
COMBINE_TILE = 128
DMA_UNROLL = 8


def _params(semantics):
    return pltpu.CompilerParams(dimension_semantics=semantics, vmem_limit_bytes=VMEM_LIMIT_BYTES)


def _dot(a, b, precision=None):
    return jnp.dot(a, b, preferred_element_type=F32, precision=precision)


def _dot_nt(a, b, precision=None):
    return lax.dot_general(a, b, (((1,), (1,)), ((), ())), preferred_element_type=F32, precision=precision)


def _dot_tn(a, b, precision=None):
    return lax.dot_general(a, b, (((0,), (0,)), ((), ())), preferred_element_type=F32, precision=precision)


def _softplus_parts(z):
    l = jnp.log1p(jnp.exp(-jnp.abs(z)))
    sp = jnp.maximum(z, 0.0) + l
    return sp, sp - z


def _rmsnorm(x, g):
    return x * lax.rsqrt(jnp.mean(x * x, axis=-1, keepdims=True) + RMS_EPS) * g


def _inproj_body(x_ref, g_ref, w_ref, qkv16_ref, kv_ref, rw_ref, gab_ref, *, q_scale):
    h = _rmsnorm(x_ref[...], g_ref[...]).astype(BF16)
    sbw = kv_ref.shape[-1] // 2
    q = _dot(h, w_ref[:, 0:sbw])
    qkv16_ref[:, 0:sbw] = (q * q_scale).astype(BF16)
    for c in (1, 2):
        kv = _dot(h, w_ref[:, c * sbw:(c + 1) * sbw])
        kv_ref[:, (c - 1) * sbw:c * sbw] = kv
        qkv16_ref[:, c * sbw:(c + 1) * sbw] = kv.astype(BF16)
    off = 3 * sbw
    for ref in (rw_ref, gab_ref):
        width = ref.shape[-1]
        for c0 in range(0, width, sbw):
            cw = min(sbw, width - c0)
            ref[:, c0:c0 + cw] = _dot(h, w_ref[:, off + c0:off + c0 + cw])
        off += width


def _inproj(x, g, w_bf16, sbw, rw_proj, q_scale, tm):
    n, d = x.shape
    total = w_bf16.shape[1]
    widths = (3 * sbw, 2 * sbw, rw_proj, 2 * d)
    dtypes = (BF16, F32, F32, F32)
    assert sum(widths) - 2 * sbw == total
    return pl.pallas_call(
        functools.partial(_inproj_body, q_scale=q_scale),
        grid=(n // tm,),
        in_specs=[
            pl.BlockSpec((tm, d), lambda i: (i, 0)),
            pl.BlockSpec((1, d), lambda i: (0, 0)),
            pl.BlockSpec((d, total), lambda i: (0, 0), pipeline_mode=pl.Buffered(1)),
        ],
        out_specs=[pl.BlockSpec((tm, wd), lambda i: (i, 0)) for wd in widths],
        out_shape=[jax.ShapeDtypeStruct((n, wd), dt) for wd, dt in zip(widths, dtypes)],
        compiler_params=_params(("parallel",)),
        name="inproj",
    )(x, g.reshape(1, d), w_bf16)


def _sb_tile(q, kb, vb, ntri, carry, mask, sb):
    tk = kb.shape[0]
    z = _dot_nt(q, kb)
    neg_abs = lax.bitcast_convert_type(lax.bitcast_convert_type(z, jnp.uint32) | jnp.uint32(0x80000000), F32)
    sp = jnp.maximum(z, 0.0) + jnp.log(1.0 + jnp.exp2(neg_abs)) * LOG2E
    if mask is not None:
        sp = jnp.where(mask, sp, 0.0)
    hi = sp.astype(BF16)
    lo = (sp - hi.astype(F32)).astype(BF16)
    log_beta = z - sp
    wts = []
    for j in range(tk // sb - 1, -1, -1):
        cols = slice(j * sb, (j + 1) * sb)
        later = _dot(hi[:, cols], ntri) + _dot(lo[:, cols], ntri) + carry
        w = jnp.exp2(log_beta[:, cols] + later)
        if mask is not None:
            w = jnp.where(mask[:, cols], w, 0.0)
        wts.append(w.astype(BF16))
        carry = carry - jnp.sum(sp[:, cols], axis=-1, keepdims=True)
    wts = wts[0] if len(wts) == 1 else jnp.concatenate(wts[::-1], axis=1)
    return _dot(wts, vb), carry


def _sb_body(q_ref, k_ref, v_ref, triq_ref, trik_ref, o_in_ref, o_ref, *, tq, tk, past):
    del o_in_ref
    qi = pl.program_id(2)
    q = q_ref[...]
    first = lax.broadcasted_iota(jnp.int32, q.shape, 1) < SB_HEAD_DIM
    zero = jnp.zeros_like(q)
    qs = (jnp.where(first, q, zero), jnp.where(first, zero, q))
    d0 = pl.multiple_of(past + qi * tq, tq)
    row = lax.broadcasted_iota(jnp.int32, (tq, tq), 0)
    col = lax.broadcasted_iota(jnp.int32, (tq, tq), 1)
    kd = k_ref[pl.ds(d0, tq), :]
    vd = v_ref[pl.ds(d0, tq), :]
    sbq = triq_ref.shape[0]
    state = []
    for qh in qs:
        state.extend(_sb_tile(qh, kd, vd, triq_ref[...], jnp.zeros((tq, 1), F32), col < row, sbq))
    nb = (past + qi * tq) // tk
    sbk = trik_ref.shape[0]

    def live(state):
        return jnp.max(jnp.maximum(state[1], state[3])) > SB_DEAD_LOG2

    def cond(loop):
        i, alive, _ = loop
        return jnp.logical_and(i < nb, alive)

    def step(loop):
        i, _, state = loop
        s0 = pl.multiple_of((nb - 1 - i) * tk, tk)
        kb = k_ref[pl.ds(s0, tk), :]
        vb = v_ref[pl.ds(s0, tk), :]
        new = []
        for h, qh in enumerate(qs):
            out, carry = _sb_tile(qh, kb, vb, trik_ref[...], state[2 * h + 1], None, sbk)
            new.extend((state[2 * h] + out, carry))
        return i + 1, live(new), tuple(new)

    _, _, state = lax.while_loop(cond, step, (jnp.int32(0), live(state), tuple(state)))
    o_ref[...] = jnp.where(first, state[0], state[2]).astype(o_ref.dtype)


def _later_matrix(n):
    r = lax.broadcasted_iota(jnp.int32, (n, n), 0)
    c = lax.broadcasted_iota(jnp.int32, (n, n), 1)
    return jnp.where(r > c, -1.0, 0.0).astype(BF16)


def _stick_breaking(q, q_row0, q_lane0, kv, k_lane0, v_lane0, batch, t, past, tq, tk, o_prev):
    s = past + t
    pair = 2 * SB_HEAD_DIM
    sbq = min(tq, SB_SUB)
    sbk = min(tk, SB_SUB)
    nq = t // tq
    qblk = q_row0 // tq
    q_spec = lambda lane0: pl.BlockSpec((tq, pair), lambda b, h, i: (qblk + b * nq + i, lane0 + h))
    kv_spec = lambda lane0: pl.BlockSpec((s, pair), lambda b, h, i: (b, lane0 + h))
    return pl.pallas_call(
        functools.partial(_sb_body, tq=tq, tk=tk, past=past),
        grid=(batch, SB_HEADS // 2, nq),
        in_specs=[
            q_spec(q_lane0), kv_spec(k_lane0), kv_spec(v_lane0),
            pl.BlockSpec((sbq, sbq), lambda b, h, i: (0, 0)),
            pl.BlockSpec((sbk, sbk), lambda b, h, i: (0, 0)),
            pl.BlockSpec(memory_space=pl.ANY),
        ],
        out_specs=q_spec(0),
        out_shape=jax.ShapeDtypeStruct(o_prev.shape, BF16),
        input_output_aliases={5: 0},
        compiler_params=_params(("parallel", "parallel", "arbitrary")),
        name="stick_breaking",
    )(q, kv, kv, _later_matrix(sbq), _later_matrix(sbk), o_prev)


def _split_bf16(x):
    hi = x.astype(BF16)
    return hi, (x - hi.astype(F32)).astype(BF16)


def _dot3(a, b, dot=_dot):
    ah, al = _split_bf16(a)
    bh, bl = _split_bf16(b)
    return dot(ah, bh) + dot(ah, bl) + dot(al, bh)


def _dot_ones_rhs(x, ones_bf16):
    hi, lo = _split_bf16(x)
    return _dot(hi, ones_bf16) + _dot(lo, ones_bf16)


def _dot_ones_lhs(ones_bf16, x):
    hi, lo = _split_bf16(x)
    return _dot(ones_bf16, hi) + _dot(ones_bf16, lo)


HEADS_PER_MATMUL = 4
GROUP_WIDTH = HEADS_PER_MATMUL * RW_HEAD_DIM


def _block_diag(y, mask):
    return jnp.concatenate([y] * HEADS_PER_MATMUL, axis=0) * mask


def _bdmm(xs, ys, mask, dot=_dot, extended=False):
    xh, xl = xs
    yh, yl = ys
    m = xh.shape[0]
    x2 = jnp.concatenate([xh, xl], axis=0) if extended else xh
    outs = []
    for g in range(RW_WIDTH // GROUP_WIDTH):
        ls = slice(g * GROUP_WIDTH, (g + 1) * GROUP_WIDTH)
        out = dot(x2[:, ls], _block_diag(yh[:, ls], mask))
        if extended:
            out = out[:m] + out[m:] + dot(xh[:, ls], _block_diag(yl[:, ls], mask))
        outs.append(out)
    return jnp.concatenate(outs, axis=1)


def _bd_tn(x, y, mask):
    n = RW_HEAD_DIM
    outs = []
    for g in range(RW_WIDTH // GROUP_WIDTH):
        ls = slice(g * GROUP_WIDTH, (g + 1) * GROUP_WIDTH)
        z = _dot3(x[:, ls].T, y[:, ls]) * mask
        outs.append(z[0:n] + z[n:2 * n] + z[2 * n:3 * n] + z[3 * n:4 * n])
    return jnp.concatenate(outs, axis=1)


def _rwkv_prep_body(p_ref, pb_ref, prev_ref, mu_ref, w0_ref, w2_ref, a0_ref, a2_ref, g2_ref, kk_ref, ka_ref,
                    rk_ref, tri_ref, bd_ref, mbf_ref, mf32_ref,
                    rhat_ref, y0_ref, a_ref, d_ref, bonus_ref, g_ref, *, chunk, groups, n_sq, t_valid):
    n = RW_HEAD_DIM
    w_ = RW_WIDTH
    p = p_ref[...]
    rows = lax.broadcasted_iota(jnp.int32, p.shape, 0)
    prev_row = jnp.where(pl.program_id(1) == 0, prev_ref[0], pb_ref[7:8, :])
    p_shift = jnp.where(rows == 0, prev_row, pltpu.roll(p, 1, 0))

    xs = p + (p_shift - p) * mu_ref[...]
    r = xs[:, 0:w_]
    k = xs[:, w_:2 * w_]
    v = xs[:, 2 * w_:3 * w_]
    o1 = 3 * w_
    dw = xs[:, o1:o1 + RW_DECAY_LORA]
    da = xs[:, o1 + RW_DECAY_LORA:o1 + RW_DECAY_LORA + RW_AAA_LORA]
    dg = xs[:, o1 + RW_DECAY_LORA + RW_AAA_LORA:]

    sp_neg = _softplus_parts(w0_ref[...] + _dot3(jnp.tanh(dw), w2_ref[...]))[1]
    log_decay = -jnp.exp(-sp_neg - 0.5)
    a = jax.nn.sigmoid(a0_ref[...] + _dot3(da, a2_ref[...]))
    g_ref[...] = _dot(jax.nn.sigmoid(dg).astype(BF16), g2_ref[...])

    bd = bd_ref[...]
    kk = k * kk_ref[...]
    kk = kk / jnp.maximum(jnp.sqrt(_dot_ones_rhs(kk * kk, bd)), 1e-12)
    k2 = k * (1.0 + (a - 1.0) * ka_ref[...])
    a_vec = -kk
    b_vec = kk * a
    bonus_ref[...] = _dot_ones_rhs(r * k2 * rk_ref[...], bd) * v
    if t_valid < chunk:
        live = lax.broadcasted_iota(jnp.int32, r.shape, 0) < t_valid
        keep = lambda x: jnp.where(live, x, 0.0)
        log_decay, r, k2, v, a_vec, b_vec = (keep(x) for x in (log_decay, r, k2, v, a_vec, b_vec))

    cw = _dot_ones_lhs(tri_ref[...], log_decay)
    cw_last = jnp.concatenate(
        [jnp.broadcast_to(cw[(j + 1) * chunk - 1:(j + 1) * chunk, :], (chunk, w_)) for j in range(groups)], axis=0)
    r_t = r * jnp.exp(cw)
    a_t = a_vec * jnp.exp(cw - log_decay)
    e_neg = jnp.exp(-cw)
    b_t = b_vec * e_neg
    k_t = k2 * e_neg
    e_rel = jnp.exp(cw_last - cw)
    b_h = b_vec * e_rel
    k_h = k2 * e_rel
    w_c = jnp.exp(cw_last)

    rr = lax.broadcasted_iota(jnp.int32, (chunk, w_), 0)
    ss = lax.broadcasted_iota(jnp.int32, (chunk, w_), 1) % n
    strict = ss < rr
    incl = ss <= rr
    eye = (ss == rr).astype(F32)
    m_bf16 = mbf_ref[...]
    m_f32 = mf32_ref[...]

    for j in range(groups):
        rs = slice(j * chunk, (j + 1) * chunk)
        v_j = v[rs]
        v_s = _split_bf16(v_j)
        ar_s = _split_bf16(jnp.concatenate([a_t[rs], r_t[rs]], axis=0))
        gram_b = _bdmm(ar_s, _split_bf16(b_t[rs]), m_bf16, _dot_nt)
        gram_k = _bdmm(ar_s, _split_bf16(k_t[rs]), m_bf16, _dot_nt)
        l_ab = jnp.where(strict, gram_b[:chunk], 0.0)
        l_ak = jnp.where(strict, gram_k[:chunk], 0.0)
        m_rb_s = _split_bf16(jnp.where(incl, gram_b[chunk:], 0.0))
        m_rk_s = _split_bf16(jnp.where(incl, gram_k[chunk:], 0.0))
        l_s = _split_bf16(l_ab)
        t_inv = eye + l_ab
        l_pow = _bdmm(l_s, l_s, m_bf16)
        for i in range(n_sq):
            lp_s = _split_bf16(l_pow)
            t_inv = t_inv + _bdmm(lp_s, _split_bf16(t_inv), m_bf16)
            if i + 1 < n_sq:
                l_pow = _bdmm(lp_s, lp_s, m_bf16)
        t_s = _split_bf16(t_inv)
        p_mat = _bdmm(t_s, _split_bf16(a_t[rs]), m_bf16)
        q_mat = _bdmm(t_s, _split_bf16(_bdmm(_split_bf16(l_ak), v_s, m_bf16)), m_bf16)
        rhat_ref[rs, :] = r_t[rs] + _bdmm(m_rb_s, _split_bf16(p_mat), m_bf16)
        y0_ref[rs, :] = _bdmm(m_rb_s, _split_bf16(q_mat), m_bf16) + _bdmm(m_rk_s, v_s, m_bf16)
        ns = slice(j * n, (j + 1) * n)
        a_ref[ns, :] = _bd_tn(p_mat, b_h[rs], m_f32) + eye * w_c[j * chunk:j * chunk + 1, :]
        d_ref[ns, :] = _bd_tn(jnp.concatenate([q_mat, v_j], axis=0),
                              jnp.concatenate([b_h[rs], k_h[rs]], axis=0), m_f32)


def _rwkv_scan_body(rhat_ref, y0_ref, a_ref, d_ref, bonus_ref, g_ref, s0_ref, lng_ref, lnb_ref, bd_ref, mbf_ref,
                    o_ref, s_ref, s_scr, y_scr, *, chunk, groups, steps):
    n = RW_HEAD_DIM
    c = pl.program_id(1)

    @pl.when(c == 0)
    def _():
        for h in range(RW_HEADS):
            s_scr[:, h * n:(h + 1) * n] = s0_ref[0, h]

    mask = mbf_ref[...]
    s = s_scr[...]
    for j in range(groups):
        rs = slice(j * chunk, (j + 1) * chunk)
        ns = slice(j * n, (j + 1) * n)
        s_s = _split_bf16(s)
        y_scr[rs, :] = y0_ref[rs, :] + _bdmm(_split_bf16(rhat_ref[rs, :]), s_s, mask, _dot_nt, extended=True)
        s = _bdmm(s_s, _split_bf16(a_ref[ns, :]), mask, extended=True) + d_ref[ns, :]
    s_scr[...] = s

    @pl.when(c == steps - 1)
    def _():
        for h in range(RW_HEADS):
            s_ref[0, h] = s[:, h * n:(h + 1) * n]

    bd = bd_ref[...]
    y = y_scr[...]
    mean = _dot_ones_rhs(y, bd) * (1.0 / n)
    yc = y - mean
    var = _dot_ones_rhs(yc * yc, bd) * (1.0 / n)
    yn = yc * lax.rsqrt(var + RW_GN_EPS) * lng_ref[...] + lnb_ref[...]
    o_ref[...] = ((yn + bonus_ref[...]) * g_ref[...]).astype(o_ref.dtype)


def _rwkv(rw_all, batch, t, t_valid, prev, s0, lp, o_prev):
    proj = rw_all.shape[1]
    n_all = o_prev.shape[0]
    row_off = 0
    n = RW_HEAD_DIM
    chunk = RW_CHUNK
    assert chunk == n and t % chunk == 0 and (t_valid == t or t == chunk)
    n_sq = chunk.bit_length() - 2
    nc = t // chunk
    gp = min(RW_PREP_CHUNKS, nc)
    gs = min(RW_SCAN_CHUNKS, nc)
    rows_p = gp * chunk
    rows_s = gs * chunk
    n_tok = batch * t
    row = lambda x: x.reshape(1, -1).astype(F32)
    r = lax.broadcasted_iota(jnp.int32, (rows_p, rows_p), 0)
    c = lax.broadcasted_iota(jnp.int32, (rows_p, rows_p), 1)
    tri = ((c <= r) & (c // chunk == r // chunk)).astype(BF16)
    hr = lax.broadcasted_iota(jnp.int32, (RW_WIDTH, RW_WIDTH), 0) // n
    hc = lax.broadcasted_iota(jnp.int32, (RW_WIDTH, RW_WIDTH), 1) // n
    bd = (hr == hc).astype(BF16)
    group_mask = (hr == hc)[:GROUP_WIDTH, :GROUP_WIDTH]
    const = lambda shape: pl.BlockSpec(shape, lambda b, c: (0,) * len(shape))

    blk_p = row_off // rows_p
    prep_args = [rw_all, rw_all, prev.astype(F32), row(lp['rw_mu']), row(lp['rw_w0']), lp['rw_w2'].astype(F32),
                 row(lp['rw_a0']), lp['rw_a2'].astype(F32), lp['rw_g2'].astype(BF16), row(lp['rw_k_k']),
                 row(lp['rw_k_a']), row(lp['rw_r_k']), tri, bd, group_mask.astype(BF16), group_mask.astype(F32)]
    tok_spec = lambda rows: pl.BlockSpec((rows, RW_WIDTH), lambda b, c: (b * (t // rows) + c, 0))
    mat_spec = lambda g: pl.BlockSpec((g * n, RW_WIDTH), lambda b, c: (b * (nc // g) + c, 0))
    tok_shape = jax.ShapeDtypeStruct((n_tok, RW_WIDTH), F32)
    mat_shape = jax.ShapeDtypeStruct((batch * nc * n, RW_WIDTH), F32)
    rhat, y0, a_mat, d_mat, bonus, g = pl.pallas_call(
        functools.partial(_rwkv_prep_body, chunk=chunk, groups=gp, n_sq=n_sq, t_valid=t_valid),
        grid=(batch, nc // gp),
        in_specs=[
            pl.BlockSpec((rows_p, proj), lambda b, c: (blk_p + b * (nc // gp) + c, 0)),
            pl.BlockSpec((8, proj), lambda b, c: (jnp.maximum((row_off + b * t + c * rows_p) // 8 - 1, 0), 0)),
            pl.BlockSpec((1, 1, proj), lambda b, c: (b, 0, 0)),
        ] + [const(x.shape) for x in prep_args[3:]],
        out_specs=[tok_spec(rows_p), tok_spec(rows_p), mat_spec(gp), mat_spec(gp), tok_spec(rows_p),
                   tok_spec(rows_p)],
        out_shape=[tok_shape, tok_shape, mat_shape, mat_shape, tok_shape, tok_shape],
        compiler_params=_params(("parallel", "parallel")),
        name="rwkv7_prep",
    )(*prep_args)

    blk_s = row_off // rows_s
    scan_args = [rhat, y0, a_mat, d_mat, bonus, g, s0.astype(F32), row(lp['rw_ln_g']), row(lp['rw_ln_b']), bd,
                 group_mask.astype(BF16), o_prev]
    state_spec = pl.BlockSpec((1, RW_HEADS, n, n), lambda b, c: (b, 0, 0, 0))
    n_in = len(scan_args) - 1

    def scan_body(*refs):
        _rwkv_scan_body(*refs[:n_in], *refs[n_in + 1:], chunk=chunk, groups=gs, steps=nc // gs)

    o, state = pl.pallas_call(
        scan_body,
        grid=(batch, nc // gs),
        in_specs=[tok_spec(rows_s), tok_spec(rows_s), mat_spec(gs), mat_spec(gs), tok_spec(rows_s),
                  tok_spec(rows_s), state_spec, const((1, RW_WIDTH)), const((1, RW_WIDTH)), const(bd.shape),
                  const(group_mask.shape), pl.BlockSpec(memory_space=pl.ANY)],
        out_specs=[pl.BlockSpec((rows_s, RW_WIDTH), lambda b, c: (blk_s + b * (nc // gs) + c, 0)), state_spec],
        out_shape=[jax.ShapeDtypeStruct((n_all, RW_WIDTH), BF16),
                   jax.ShapeDtypeStruct((batch, RW_HEADS, n, n), F32)],
        scratch_shapes=[pltpu.VMEM((n, RW_WIDTH), F32), pltpu.VMEM((rows_s, RW_WIDTH), F32)],
        input_output_aliases={n_in: 0},
        compiler_params=_params(("parallel", "arbitrary")),
        name="rwkv7_scan",
    )(*scan_args)
    return o, state


def _merge_body(oa_ref, ob_ref, gab_ref, x_ref, wa_ref, wb_ref, wo_ref, g2_ref, wr_ref, br_ref, tri_ref,
                x2_ref, h2_ref, idx_ref, gate_ref, rank_ref, cnt_ref, carry_scr):
    i = pl.program_id(0)
    d = x_ref.shape[-1]

    @pl.when(i == 0)
    def _():
        carry_scr[...] = jnp.zeros_like(carry_scr)

    gab = gab_ref[...]
    merged = (jax.nn.sigmoid(gab[:, :d]) * _dot(oa_ref[...], wa_ref[...])
              + jax.nn.sigmoid(gab[:, d:]) * _dot(ob_ref[...], wb_ref[...]))
    x2 = x_ref[...] + _dot(merged.astype(BF16), wo_ref[...])
    x2_ref[...] = x2
    h2 = _rmsnorm(x2, g2_ref[...])
    h2_ref[...] = h2
    logits = _dot3(h2, wr_ref[...]) + br_ref[...]

    tm, ne = logits.shape
    col = lax.broadcasted_iota(jnp.int32, (tm, ne), 1)
    c4 = lax.broadcasted_iota(jnp.int32, (tm, TOP_K), 1)
    work = logits
    tops, idxs = [], []
    for _ in range(TOP_K):
        m = jnp.max(work, axis=-1, keepdims=True)
        ix = jnp.min(jnp.where(work == m, col, ne), axis=-1, keepdims=True)
        tops.append(m)
        idxs.append(ix)
        work = jnp.where(col == ix, -jnp.inf, work)
    es = [jnp.exp(m - tops[0]) for m in tops]
    denom = es[0] + es[1] + es[2] + es[3]
    onehot = jnp.zeros((tm, ne), F32)
    for ix in idxs:
        onehot = onehot + (col == ix).astype(F32)
    before = carry_scr[...] + _dot(tri_ref[...], onehot.astype(BF16))
    idx_out = jnp.zeros((tm, TOP_K), jnp.int32)
    gate_out = jnp.zeros((tm, TOP_K), F32)
    rank_out = jnp.zeros((tm, TOP_K), jnp.int32)
    for kk in range(TOP_K):
        rk = jnp.sum(jnp.where(col == idxs[kk], before, 0.0), axis=-1, keepdims=True)
        idx_out = jnp.where(c4 == kk, idxs[kk], idx_out)
        gate_out = jnp.where(c4 == kk, es[kk] / denom, gate_out)
        rank_out = jnp.where(c4 == kk, rk.astype(jnp.int32), rank_out)
    idx_ref[...] = idx_out
    gate_ref[...] = gate_out
    rank_ref[...] = rank_out
    carry = carry_scr[...] + jnp.sum(onehot, axis=0, keepdims=True)
    carry_scr[...] = carry
    cnt_ref[...] = carry.astype(jnp.int32)


def _merge_route(oa, ob, gab, x, lp, tm):
    n, d = x.shape
    ne = N_EXPERTS
    r = lax.broadcasted_iota(jnp.int32, (tm, tm), 0)
    c = lax.broadcasted_iota(jnp.int32, (tm, tm), 1)
    tri = (c < r).astype(BF16)
    const = lambda shape: pl.BlockSpec(shape, lambda i: (0,) * len(shape))
    rowblk = lambda wd: pl.BlockSpec((tm, wd), lambda i: (i, 0))
    wa = lp['w_branch_a'].astype(BF16)
    wb = lp['w_branch_b'].astype(BF16)
    wo = lp['w_out'].astype(BF16)
    return pl.pallas_call(
        _merge_body,
        grid=(n // tm,),
        in_specs=[rowblk(oa.shape[1]), rowblk(ob.shape[1]), rowblk(2 * d), rowblk(d),
                  const(wa.shape), const(wb.shape), const(wo.shape), const((1, d)), const((d, ne)),
                  const((1, ne)), const((tm, tm))],
        out_specs=[rowblk(d), rowblk(d), rowblk(TOP_K), rowblk(TOP_K), rowblk(TOP_K), const((1, ne))],
        out_shape=[jax.ShapeDtypeStruct((n, d), F32), jax.ShapeDtypeStruct((n, d), F32),
                   jax.ShapeDtypeStruct((n, TOP_K), jnp.int32), jax.ShapeDtypeStruct((n, TOP_K), F32),
                   jax.ShapeDtypeStruct((n, TOP_K), jnp.int32), jax.ShapeDtypeStruct((1, ne), jnp.int32)],
        scratch_shapes=[pltpu.VMEM((1, ne), F32)],
        compiler_params=_params(("arbitrary",)),
        name="merge_route",
    )(oa, ob, gab, x, wa, wb, wo, lp['norm2_g'].reshape(1, d).astype(F32), lp['w_router'].astype(F32),
      lp['b_router'].reshape(1, ne).astype(F32), tri)


def _dispatch_body(dest_ref, end_ref, padded_ref, h_ref, xs_ref, zeros_ref, sem, zsem, *, tile, bm):
    @pl.when(pl.program_id(0) == 0)
    def _():
        zeros_ref[...] = jnp.zeros_like(zeros_ref)

        def clear(e):
            start = pl.multiple_of(end_ref[e] - bm, bm)
            return pltpu.make_async_copy(zeros_ref, xs_ref.at[pl.ds(start, bm)], zsem)

        for e in range(N_EXPERTS):
            @pl.when(padded_ref[e] > 0)
            def _():
                clear(e).start()

        for e in range(N_EXPERTS):
            @pl.when(padded_ref[e] > 0)
            def _():
                clear(e).wait()

    def copy(t, kk):
        return pltpu.make_async_copy(h_ref.at[pl.ds(t, 1)],
                                     xs_ref.at[pl.ds(dest_ref[0, 0, t * TOP_K + kk], 1)], sem)

    def issue(t, carry):
        for kk in range(TOP_K):
            copy(t, kk).start(priority=kk % 2)
        return carry

    def drain(t, carry):
        for kk in range(TOP_K):
            copy(t, kk).wait()
        return carry

    lax.fori_loop(0, tile, issue, 0, unroll=DMA_UNROLL)
    lax.fori_loop(0, tile, drain, 0, unroll=DMA_UNROLL)


def _dispatch(h2, dest, pad_end, padded, n_rows, tile, bm):
    n, d = h2.shape
    nt = n // tile
    smem = pl.BlockSpec(memory_space=pltpu.SMEM)
    return pl.pallas_call(
        functools.partial(_dispatch_body, tile=tile, bm=bm),
        grid=(nt,),
        in_specs=[pl.BlockSpec((1, 1, tile * TOP_K), lambda i: (i, 0, 0), memory_space=pltpu.SMEM), smem, smem,
                  pl.BlockSpec((tile, d), lambda i: (i, 0))],
        out_specs=pl.BlockSpec(memory_space=pl.ANY),
        out_shape=jax.ShapeDtypeStruct((n_rows, d), F32),
        scratch_shapes=[pltpu.VMEM((bm, d), F32), pltpu.SemaphoreType.DMA, pltpu.SemaphoreType.DMA],
        compiler_params=_params(("arbitrary",)),
        name="moe_dispatch",
    )(dest.reshape(nt, 1, tile * TOP_K), pad_end.astype(jnp.int32), padded.astype(jnp.int32), h2)


def _expert_body(be_ref, used_ref, xs_ref, wgu_ref, bgu_ref, wd_ref, bd_ref, ys_ref, wgu16, wd16):
    i = pl.program_id(0)
    de = wd_ref.shape[1]

    @pl.when(jnp.logical_or(i == 0, be_ref[i] != be_ref[jnp.maximum(i - 1, 0)]))
    def _():
        wgu16[...] = wgu_ref[0].astype(BF16)
        wd16[...] = wd_ref[0].astype(BF16)

    @pl.when(i < used_ref[0])
    def _():
        gu = _dot(xs_ref[...].astype(BF16), wgu16[...]) + bgu_ref[0]
        g = jnp.minimum(gu[:, :de], SWIGLU_LIMIT)
        u = jnp.clip(gu[:, de:], -SWIGLU_LIMIT, SWIGLU_LIMIT)
        act = (u + 1.0) * (g * jax.nn.sigmoid(g * SWIGLU_ALPHA))
        ys_ref[...] = _dot(act.astype(BF16), wd16[...]) + bd_ref[0]

    @pl.when(i >= used_ref[0])
    def _():
        ys_ref[...] = jnp.zeros_like(ys_ref)


def _experts(xs, block_expert, n_used, wgu, bgu, wd, bd, bm):
    n_rows, d = xs.shape
    ne, _, de2 = wgu.shape
    nb = n_rows // bm
    grid_spec = pltpu.PrefetchScalarGridSpec(
        num_scalar_prefetch=2,
        grid=(nb,),
        in_specs=[
            pl.BlockSpec((bm, d), lambda i, be, nu: (i, 0)),
            pl.BlockSpec((1, d, de2), lambda i, be, nu: (be[i], 0, 0)),
            pl.BlockSpec((1, 1, de2), lambda i, be, nu: (be[i], 0, 0)),
            pl.BlockSpec((1, de2 // 2, d), lambda i, be, nu: (be[i], 0, 0)),
            pl.BlockSpec((1, 1, d), lambda i, be, nu: (be[i], 0, 0)),
        ],
        out_specs=pl.BlockSpec((bm, d), lambda i, be, nu: (i, 0)),
        scratch_shapes=[pltpu.VMEM((d, de2), BF16), pltpu.VMEM((de2 // 2, d), BF16)],
    )
    return pl.pallas_call(
        _expert_body,
        grid_spec=grid_spec,
        out_shape=jax.ShapeDtypeStruct((n_rows, d), F32),
        compiler_params=_params(("arbitrary",)),
        name="moe_experts",
    )(block_expert, n_used, xs, wgu, bgu.reshape(ne, 1, de2), wd, bd.reshape(ne, 1, d))


def _combine_body(dest_ref, ys_ref, gate_ref, x2_ref, g_ref, ya_ref, yb_ref, buf, sem, *, tile, tiles_a):
    def copy(t, kk):
        return pltpu.make_async_copy(ys_ref.at[pl.ds(dest_ref[0, 0, t * TOP_K + kk], 1)],
                                     buf.at[kk, pl.ds(t, 1)], sem)

    def issue(t, carry):
        for kk in range(TOP_K):
            copy(t, kk).start(priority=kk % 2)
        return carry

    def drain(t, carry):
        for kk in range(TOP_K):
            copy(t, kk).wait()
        return carry

    lax.fori_loop(0, tile, issue, 0, unroll=DMA_UNROLL)
    lax.fori_loop(0, tile, drain, 0, unroll=DMA_UNROLL)
    gate = gate_ref[...]
    moe = gate[:, 0:1] * buf[0]
    for kk in range(1, TOP_K):
        moe = moe + gate[:, kk:kk + 1] * buf[kk]
    y = _rmsnorm(x2_ref[...] + moe, g_ref[...])
    i = pl.program_id(0)

    @pl.when(i < tiles_a)
    def _():
        ya_ref[...] = y

    @pl.when(i >= tiles_a)
    def _():
        yb_ref[...] = y


def _combine(ys, dest, gate, x2, final_g, tile, n_a):
    n, d = x2.shape
    nt = n // tile
    tiles_a = n_a // tile
    return pl.pallas_call(
        functools.partial(_combine_body, tile=tile, tiles_a=tiles_a),
        grid=(nt,),
        in_specs=[pl.BlockSpec((1, 1, tile * TOP_K), lambda i: (i, 0, 0), memory_space=pltpu.SMEM),
                  pl.BlockSpec(memory_space=pl.ANY),
                  pl.BlockSpec((tile, TOP_K), lambda i: (i, 0)),
                  pl.BlockSpec((tile, d), lambda i: (i, 0)),
                  pl.BlockSpec((1, d), lambda i: (0, 0))],
        out_specs=[pl.BlockSpec((tile, d), lambda i: (jnp.minimum(i, tiles_a - 1), 0)),
                   pl.BlockSpec((tile, d), lambda i: (jnp.maximum(i - tiles_a, 0), 0))],
        out_shape=[jax.ShapeDtypeStruct((n_a, d), F32), jax.ShapeDtypeStruct((n - n_a, d), F32)],
        scratch_shapes=[pltpu.VMEM((TOP_K, tile, d), F32), pltpu.SemaphoreType.DMA],
        compiler_params=_params(("arbitrary",)),
        name="moe_combine",
    )(dest.reshape(nt, 1, tile * TOP_K), ys, gate, x2, final_g.reshape(1, d).astype(F32))


def _moe(h2, x2, idx, gate, rank, counts, lp, final_g, n_a):
    n, d = h2.shape
    bm = EXPERT_ROWS
    counts = counts.reshape(N_EXPERTS)
    padded = (counts + bm - 1) // bm * bm
    pad_end = jnp.cumsum(padded)
    dest = ((pad_end - padded)[idx] + rank).reshape(n * TOP_K).astype(jnp.int32)
    n_blocks = -(-(n * TOP_K) // bm) + N_EXPERTS
    block_start = jnp.arange(n_blocks, dtype=jnp.int32) * bm
    block_expert = jnp.minimum(jnp.sum(pad_end[None, :] <= block_start[:, None], axis=1),
                               N_EXPERTS - 1).astype(jnp.int32)
    n_used = (pad_end[-1:] // bm).astype(jnp.int32)
    xs = _dispatch(h2, dest, pad_end, padded, n_blocks * bm, DISPATCH_TILE, bm)
    ys = _experts(xs, block_expert, n_used, lp['w_gate_up'].astype(F32), lp['b_gate_up'].astype(F32),
                  lp['w_down'].astype(F32), lp['b_down'].astype(F32), bm)
    return _combine(ys, dest, gate, x2, final_g, COMBINE_TILE, n_a)


def _to_heads(t, b, s):
    return t.reshape(b, s, SB_HEADS, SB_HEAD_DIM).transpose(0, 2, 1, 3)


def _from_heads(t):
    b, h, s, dh = t.shape
    return t.transpose(0, 2, 1, 3).reshape(b, s, h * dh)


def _layer(x_all, bp, tp, bs, ts, cache_k, cache_v, state_rwkv, state_shift, lp, final_g):
    n, d = x_all.shape
    sbw = SB_HEADS * SB_HEAD_DIM
    rw_proj = lp['rw_mu'].shape[0]
    n_p = bp * tp
    scale = LOG2E * SB_HEAD_DIM ** -0.5
    qkv16, kv, rw, gab = _inproj(x_all, lp['norm1_g'].astype(F32), lp['w_in'].astype(BF16), sbw, rw_proj,
                                 scale, ROW_TILE)
    pairs = SB_HEADS // 2
    oa = _stick_breaking(qkv16, 0, 0, qkv16, pairs, 2 * pairs, bp, tp, 0, SB_BLOCK, SB_SUB,
                         jnp.zeros((n, sbw), BF16))
    past = cache_k.shape[2]
    new16 = qkv16[n_p:].reshape(bs, ts, 3 * sbw)
    kv_s = jnp.concatenate([
        jnp.concatenate([_from_heads(cache_k).astype(BF16), new16[:, :, sbw:2 * sbw]], axis=1),
        jnp.concatenate([_from_heads(cache_v).astype(BF16), new16[:, :, 2 * sbw:]], axis=1)], axis=2)
    oa = _stick_breaking(qkv16, n_p, 0, kv_s.reshape(bs * (past + ts), 2 * sbw), 0, pairs, bs, ts, past, ts,
                         min(SB_SUB, past), oa)
    kp, vp = kv[:n_p, :sbw], kv[:n_p, sbw:]
    ks, vs = kv[n_p:, :sbw], kv[n_p:, sbw:]

    s0_p = jnp.zeros((bp, RW_HEADS, RW_HEAD_DIM, RW_HEAD_DIM), F32)
    prev_p = jnp.zeros((bp, 1, rw_proj), F32)
    ob, st_p = _rwkv(rw, bp, tp, tp, prev_p, s0_p, lp, jnp.zeros((n, RW_WIDTH), BF16))
    rw_s = rw[n_p:].reshape(bs, ts, rw_proj)
    ts_pad = -(-ts // RW_CHUNK) * RW_CHUNK
    rw_s_pad = jnp.pad(rw_s, ((0, 0), (0, ts_pad - ts), (0, 0))).reshape(bs * ts_pad, rw_proj)
    ob_s, st_s = _rwkv(rw_s_pad, bs, ts_pad, ts, state_shift, state_rwkv, lp,
                       jnp.zeros((bs * ts_pad, RW_WIDTH), BF16))
    ob = lax.dynamic_update_slice(
        ob, ob_s.reshape(bs, ts_pad, RW_WIDTH)[:, :ts].reshape(bs * ts, RW_WIDTH), (n_p, 0))
    sh_p = jnp.stack([rw[(b + 1) * tp - 1:(b + 1) * tp] for b in range(bp)])
    sh_s = rw_s[:, ts - 1:]

    x2, h2, idx, gate, rank, counts = _merge_route(oa, ob, gab, x_all, lp, MERGE_TILE)
    y = _moe(h2, x2, idx, gate, rank, counts, lp, final_g, n_p)
    return (y, _to_heads(kp, bp, tp), _to_heads(vp, bp, tp), st_p, sh_p,
            _to_heads(ks, bs, ts), _to_heads(vs, bs, ts), st_s, sh_s)


def kernel(x_prompt, x_sample, cache_sb_k, cache_sb_v, state_rwkv, state_shift, norm1_g, w_in, rw_mu, rw_w0, rw_w2, rw_a0, rw_a2, rw_g2, rw_k_k, rw_k_a, rw_r_k, rw_ln_g, rw_ln_b, w_branch_a, w_branch_b, w_out, norm2_g, w_router, b_router, w_gate_up, b_gate_up, w_down, b_down, final_norm_g):
    depth = w_in.shape[0]
    assert depth == 1, "the final RMSNorm is fused into the last (only) layer"
    bp, tp, d = x_prompt.shape
    bs, ts, _ = x_sample.shape
    n_p = bp * tp
    x_all = jnp.concatenate([x_prompt.reshape(n_p, d), x_sample.reshape(bs * ts, d)], axis=0)
    lp = dict(norm1_g=norm1_g[0], w_in=w_in[0], rw_mu=rw_mu[0], rw_w0=rw_w0[0], rw_w2=rw_w2[0], rw_a0=rw_a0[0],
              rw_a2=rw_a2[0], rw_g2=rw_g2[0], rw_k_k=rw_k_k[0], rw_k_a=rw_k_a[0], rw_r_k=rw_r_k[0].reshape(-1),
              rw_ln_g=rw_ln_g[0], rw_ln_b=rw_ln_b[0], w_branch_a=w_branch_a[0], w_branch_b=w_branch_b[0],
              w_out=w_out[0], norm2_g=norm2_g[0], w_router=w_router[0], b_router=b_router[0],
              w_gate_up=w_gate_up[0], b_gate_up=b_gate_up[0], w_down=w_down[0], b_down=b_down[0])
    y, kp, vp, st_p, sh_p, ks, vs, st_s, sh_s = _layer(
        x_all, bp, tp, bs, ts, cache_sb_k[0], cache_sb_v[0], state_rwkv[0], state_shift[0], lp, final_norm_g)
    return (y[0].reshape(bp, tp, d), y[1].reshape(bs, ts, d),
            kp[None], vp[None], st_p[None], sh_p[None], ks[None], vs[None], st_s[None], sh_s[None])
```

```python
import functools

import jax
import jax.numpy as jnp
from jax import lax
from jax.experimental import pallas as pl
from jax.experimental.pallas import tpu as pltpu

F32 = jnp.float32
BF16 = jnp.bfloat16
HIGHEST = lax.Precision.HIGHEST

SB_HEADS = 8
SB_HEAD_DIM = 64
RW_HEADS = 8
RW_HEAD_DIM = 64
RW_WIDTH = RW_HEADS * RW_HEAD_DIM
RW_DECAY_LORA = 64
RW_AAA_LORA = 64
RW_GATE_LORA = 128
RW_GN_EPS = 64e-5
N_EXPERTS = 32
TOP_K = 4
SWIGLU_LIMIT = 7.0
SWIGLU_ALPHA = 1.702
RMS_EPS = 1e-6

VMEM_LIMIT_BYTES = 56 * 1024 * 1024
ROW_TILE = 256
MERGE_TILE = 512
SB_BLOCK = 512
SB_SUB = 256
LOG2E = 1.4426950408889634
SB_DEAD_LOG2 = -200.0
RW_CHUNK = 64
RW_PREP_CHUNKS = 4
RW_SCAN_CHUNKS = 4
EXPERT_ROWS = 512
DISPATCH_TILE = 512
COMBINE_TILE = 128
DMA_UNROLL = 8


def _params(semantics):
    return pltpu.CompilerParams(dimension_semantics=semantics, vmem_limit_bytes=VMEM_LIMIT_BYTES)


def _dot(a, b, precision=None):
    return jnp.dot(a, b, preferred_element_type=F32, precision=precision)


def _dot_nt(a, b, precision=None):
    return lax.dot_general(a, b, (((1,), (1,)), ((), ())), preferred_element_type=F32, precision=precision)


def _dot_tn(a, b, precision=None):
    return lax.dot_general(a, b, (((0,), (0,)), ((), ())), preferred_element_type=F32, precision=precision)


def _softplus_parts(z):
    l = jnp.log(1.0 + jnp.exp(-jnp.abs(z)))
    sp = jnp.maximum(z, 0.0) + l
    return sp, sp - z


def _rmsnorm(x, g):
    return x * lax.rsqrt(jnp.mean(x * x, axis=-1, keepdims=True) + RMS_EPS) * g


def _inproj_body(xa_ref, xb_ref, g_ref, w_ref, qkv16_ref, ka_ref, va_ref, kvb_ref, rw_ref, gab_ref, *,
                 q_scale, tiles_a):
    i = pl.program_id(0)
    x = jnp.where(i < tiles_a, xa_ref[...], xb_ref[...])
    h = _rmsnorm(x, g_ref[...]).astype(BF16)
    sbw = kvb_ref.shape[-1] // 2
    dh = ka_ref.shape[-1]
    q = _dot(h, w_ref[:, 0:sbw])
    qkv16_ref[:, 0:sbw] = (q * q_scale).astype(BF16)
    for c, hm_ref in ((1, ka_ref), (2, va_ref)):
        kv = _dot(h, w_ref[:, c * sbw:(c + 1) * sbw])
        qkv16_ref[:, c * sbw:(c + 1) * sbw] = kv.astype(BF16)

        @pl.when(i < tiles_a)
        def _():
            for hd in range(sbw // dh):
                hm_ref[0, hd] = kv[:, hd * dh:(hd + 1) * dh]

        @pl.when(i >= tiles_a)
        def _():
            kvb_ref[:, (c - 1) * sbw:c * sbw] = kv
    off = 3 * sbw
    for ref in (rw_ref, gab_ref):
        width = ref.shape[-1]
        for c0 in range(0, width, sbw):
            cw = min(sbw, width - c0)
            ref[:, c0:c0 + cw] = _dot(h, w_ref[:, off + c0:off + c0 + cw])
        off += width


def _inproj(xa, xb, g, w_bf16, rw_proj, q_scale, tm):
    ba, ta, d = xa.shape
    nb = xb.shape[0]
    n = ba * ta + nb
    sbw = SB_HEADS * SB_HEAD_DIM
    total = w_bf16.shape[1]
    assert total == 3 * sbw + rw_proj + 2 * d and ta % tm == 0 and nb % tm == 0
    tiles_a = ba * ta // tm
    per_seq = ta // tm
    last_a = tiles_a - 1
    row = lambda wd: pl.BlockSpec((tm, wd), lambda i: (i, 0))
    head_major = pl.BlockSpec(
        (1, SB_HEADS, tm, SB_HEAD_DIM),
        lambda i: (jnp.minimum(i, last_a) // per_seq, 0, jnp.minimum(i, last_a) % per_seq, 0))
    hm_shape = jax.ShapeDtypeStruct((ba, SB_HEADS, ta, SB_HEAD_DIM), F32)
    return pl.pallas_call(
        functools.partial(_inproj_body, q_scale=q_scale, tiles_a=tiles_a),
        grid=(n // tm,),
        in_specs=[
            pl.BlockSpec((tm, d), lambda i: (jnp.minimum(i, last_a), 0)),
            pl.BlockSpec((tm, d), lambda i: (jnp.maximum(i - tiles_a, 0), 0)),
            pl.BlockSpec((1, d), lambda i: (0, 0)),
            pl.BlockSpec((d, total), lambda i: (0, 0), pipeline_mode=pl.Buffered(1)),
        ],
        out_specs=[row(3 * sbw), head_major, head_major,
                   pl.BlockSpec((tm, 2 * sbw), lambda i: (jnp.maximum(i - tiles_a, 0), 0)),
                   row(rw_proj), row(2 * d)],
        out_shape=[jax.ShapeDtypeStruct((n, 3 * sbw), BF16), hm_shape, hm_shape,
                   jax.ShapeDtypeStruct((nb, 2 * sbw), F32),
                   jax.ShapeDtypeStruct((n, rw_proj), F32), jax.ShapeDtypeStruct((n, 2 * d), F32)],
        compiler_params=_params(("arbitrary",)),
        name="inproj",
    )(xa.reshape(ba * ta, d), xb, g.reshape(1, d), w_bf16)


def _sb_tile(q, kb, vb, ntri, carry, mask, sb):
    tk = kb.shape[0]
    z = _dot_nt(q, kb)
    neg_abs = lax.bitcast_convert_type(lax.bitcast_convert_type(z, jnp.uint32) | jnp.uint32(0x80000000), F32)
    sp = jnp.maximum(z, 0.0) + jnp.log(1.0 + jnp.exp2(neg_abs)) * LOG2E
    if mask is not None:
        sp = jnp.where(mask, sp, 0.0)
    hi = sp.astype(BF16)
    lo = (sp - hi.astype(F32)).astype(BF16)
    log_beta = z - sp
    wts = []
    for j in range(tk // sb - 1, -1, -1):
        cols = slice(j * sb, (j + 1) * sb)
        later = _dot(hi[:, cols], ntri) + _dot(lo[:, cols], ntri) + carry
        w = jnp.exp2(log_beta[:, cols] + later)
        if mask is not None:
            w = jnp.where(mask[:, cols], w, 0.0)
        wts.append(w.astype(BF16))
        carry = carry - jnp.sum(sp[:, cols], axis=-1, keepdims=True)
    wts = wts[0] if len(wts) == 1 else jnp.concatenate(wts[::-1], axis=1)
    return _dot(wts, vb), carry


def _sb_body(q_ref, k_ref, v_ref, triq_ref, trik_ref, o_in_ref, o_ref, *, tq, tk, past):
    del o_in_ref
    qi = pl.program_id(2)
    q = q_ref[...]
    first = lax.broadcasted_iota(jnp.int32, q.shape, 1) < SB_HEAD_DIM
    zero = jnp.zeros_like(q)
    qs = (jnp.where(first, q, zero), jnp.where(first, zero, q))
    d0 = pl.multiple_of(past + qi * tq, tq)
    row = lax.broadcasted_iota(jnp.int32, (tq, tq), 0)
    col = lax.broadcasted_iota(jnp.int32, (tq, tq), 1)
    kd = k_ref[pl.ds(d0, tq), :]
    vd = v_ref[pl.ds(d0, tq), :]
    sbq = triq_ref.shape[0]
    state = []
    for qh in qs:
        state.extend(_sb_tile(qh, kd, vd, triq_ref[...], jnp.zeros((tq, 1), F32), col < row, sbq))
    nb = (past + qi * tq) // tk
    sbk = trik_ref.shape[0]

    def live(state):
        return jnp.max(jnp.maximum(state[1], state[3])) > SB_DEAD_LOG2

    def cond(loop):
        i, alive, _ = loop
        return jnp.logical_and(i < nb, alive)

    def step(loop):
        i, _, state = loop
        s0 = pl.multiple_of((nb - 1 - i) * tk, tk)
        kb = k_ref[pl.ds(s0, tk), :]
        vb = v_ref[pl.ds(s0, tk), :]
        new = []
        for h, qh in enumerate(qs):
            out, carry = _sb_tile(qh, kb, vb, trik_ref[...], state[2 * h + 1], None, sbk)
            new.extend((state[2 * h] + out, carry))
        return i + 1, live(new), tuple(new)

    _, _, state = lax.while_loop(cond, step, (jnp.int32(0), live(state), tuple(state)))
    o_ref[...] = jnp.where(first, state[0], state[2]).astype(o_ref.dtype)


def _later_matrix(n):
    r = lax.broadcasted_iota(jnp.int32, (n, n), 0)
    c = lax.broadcasted_iota(jnp.int32, (n, n), 1)
    return jnp.where(r > c, -1.0, 0.0).astype(BF16)


def _stick_breaking(q, q_row0, q_lane0, kv, k_lane0, v_lane0, batch, t, past, tq, tk, o_prev):
    s = past + t
    pair = 2 * SB_HEAD_DIM
    sbq = min(tq, SB_SUB)
    sbk = min(tk, SB_SUB)
    nq = t // tq
    qblk = q_row0 // tq
    q_spec = lambda lane0: pl.BlockSpec((tq, pair), lambda b, h, i: (qblk + b * nq + i, lane0 + h))
    kv_spec = lambda lane0: pl.BlockSpec((s, pair), lambda b, h, i: (b, lane0 + h))
    return pl.pallas_call(
        functools.partial(_sb_body, tq=tq, tk=tk, past=past),
        grid=(batch, SB_HEADS // 2, nq),
        in_specs=[
            q_spec(q_lane0), kv_spec(k_lane0), kv_spec(v_lane0),
            pl.BlockSpec((sbq, sbq), lambda b, h, i: (0, 0)),
            pl.BlockSpec((sbk, sbk), lambda b, h, i: (0, 0)),
            pl.BlockSpec(memory_space=pl.ANY),
        ],
        out_specs=q_spec(0),
        out_shape=jax.ShapeDtypeStruct(o_prev.shape, BF16),
        input_output_aliases={5: 0},
        compiler_params=_params(("parallel", "parallel", "arbitrary")),
        name="stick_breaking",
    )(q, kv, kv, _later_matrix(sbq), _later_matrix(sbk), o_prev)


def _split_bf16(x):
    hi = x.astype(BF16)
    return hi, (x - hi.astype(F32)).astype(BF16)


def _dot3(a, b, dot=_dot):
    ah, al = _split_bf16(a)
    bh, bl = _split_bf16(b)
    return dot(ah, bh) + dot(ah, bl) + dot(al, bh)


def _dot_ones_rhs(x, ones_bf16):
    hi, lo = _split_bf16(x)
    return _dot(hi, ones_bf16) + _dot(lo, ones_bf16)


def _dot_ones_lhs(ones_bf16, x):
    hi, lo = _split_bf16(x)
    return _dot(ones_bf16, hi) + _dot(ones_bf16, lo)


HEADS_PER_MATMUL = 4
GROUP_WIDTH = HEADS_PER_MATMUL * RW_HEAD_DIM


def _block_diag(y, mask):
    return jnp.concatenate([y] * HEADS_PER_MATMUL, axis=0) * mask


def _bdmm(xs, ys, mask, dot=_dot, extended=False):
    xh, xl = xs
    yh, yl = ys
    m = xh.shape[0]
    x2 = jnp.concatenate([xh, xl], axis=0) if extended else xh
    outs = []
    for g in range(RW_WIDTH // GROUP_WIDTH):
        ls = slice(g * GROUP_WIDTH, (g + 1) * GROUP_WIDTH)
        out = dot(x2[:, ls], _block_diag(yh[:, ls], mask))
        if extended:
            out = out[:m] + out[m:] + dot(xh[:, ls], _block_diag(yl[:, ls], mask))
        outs.append(out)
    return jnp.concatenate(outs, axis=1)


def _bd_tn(x, y, mask):
    n = RW_HEAD_DIM
    outs = []
    for g in range(RW_WIDTH // GROUP_WIDTH):
        ls = slice(g * GROUP_WIDTH, (g + 1) * GROUP_WIDTH)
        z = _dot3(x[:, ls].T, y[:, ls]) * mask
        outs.append(z[0:n] + z[n:2 * n] + z[2 * n:3 * n] + z[3 * n:4 * n])
    return jnp.concatenate(outs, axis=1)


def _rwkv_prep_body(p_ref, pb_ref, prev_ref, mu_ref, w0_ref, w2_ref, a0_ref, a2_ref, g2_ref, kk_ref, ka_ref,
                    rk_ref, tri_ref, bd_ref, mbf_ref, mf32_ref,
                    rhat_ref, y0_ref, a_ref, d_ref, bonus_ref, g_ref, *, chunk, groups, n_sq, t_valid):
    n = RW_HEAD_DIM
    w_ = RW_WIDTH
    p = p_ref[...]
    rows = lax.broadcasted_iota(jnp.int32, p.shape, 0)
    prev_row = jnp.where(pl.program_id(1) == 0, prev_ref[0], pb_ref[7:8, :])
    p_shift = jnp.where(rows == 0, prev_row, pltpu.roll(p, 1, 0))

    xs = p + (p_shift - p) * mu_ref[...]
    r = xs[:, 0:w_]
    k = xs[:, w_:2 * w_]
    v = xs[:, 2 * w_:3 * w_]
    o1 = 3 * w_
    dw = xs[:, o1:o1 + RW_DECAY_LORA]
    da = xs[:, o1 + RW_DECAY_LORA:o1 + RW_DECAY_LORA + RW_AAA_LORA]
    dg = xs[:, o1 + RW_DECAY_LORA + RW_AAA_LORA:]

    sp_neg = _softplus_parts(w0_ref[...] + _dot3(jnp.tanh(dw), w2_ref[...]))[1]
    log_decay = -jnp.exp(-sp_neg - 0.5)
    a = jax.nn.sigmoid(a0_ref[...] + _dot3(da, a2_ref[...]))
    g_ref[...] = _dot(jax.nn.sigmoid(dg).astype(BF16), g2_ref[...])

    bd = bd_ref[...]
    kk = k * kk_ref[...]
    kk = kk / jnp.maximum(jnp.sqrt(_dot_ones_rhs(kk * kk, bd)), 1e-12)
    k2 = k * (1.0 + (a - 1.0) * ka_ref[...])
    a_vec = -kk
    b_vec = kk * a
    bonus_ref[...] = _dot_ones_rhs(r * k2 * rk_ref[...], bd) * v
    if t_valid < chunk:
        live = lax.broadcasted_iota(jnp.int32, r.shape, 0) < t_valid
        keep = lambda x: jnp.where(live, x, 0.0)
        log_decay, r, k2, v, a_vec, b_vec = (keep(x) for x in (log_decay, r, k2, v, a_vec, b_vec))

    cw = _dot_ones_lhs(tri_ref[...], log_decay)
    cw_last = jnp.concatenate(
        [jnp.broadcast_to(cw[(j + 1) * chunk - 1:(j + 1) * chunk, :], (chunk, w_)) for j in range(groups)], axis=0)
    r_t = r * jnp.exp(cw)
    a_t = a_vec * jnp.exp(cw - log_decay)
    e_neg = jnp.exp(-cw)
    b_t = b_vec * e_neg
    k_t = k2 * e_neg
    e_rel = jnp.exp(cw_last - cw)
    b_h = b_vec * e_rel
    k_h = k2 * e_rel
    w_c = jnp.exp(cw_last)

    rr = lax.broadcasted_iota(jnp.int32, (chunk, w_), 0)
    ss = lax.broadcasted_iota(jnp.int32, (chunk, w_), 1) % n
    strict = ss < rr
    incl = ss <= rr
    eye = (ss == rr).astype(F32)
    m_bf16 = mbf_ref[...]
    m_f32 = mf32_ref[...]

    for j in range(groups):
        rs = slice(j * chunk, (j + 1) * chunk)
        v_j = v[rs]
        v_s = _split_bf16(v_j)
        ar_s = _split_bf16(jnp.concatenate([a_t[rs], r_t[rs]], axis=0))
        gram_b = _bdmm(ar_s, _split_bf16(b_t[rs]), m_bf16, _dot_nt)
        gram_k = _bdmm(ar_s, _split_bf16(k_t[rs]), m_bf16, _dot_nt)
        l_ab = jnp.where(strict, gram_b[:chunk], 0.0)
        l_ak = jnp.where(strict, gram_k[:chunk], 0.0)
        m_rb_s = _split_bf16(jnp.where(incl, gram_b[chunk:], 0.0))
        m_rk_s = _split_bf16(jnp.where(incl, gram_k[chunk:], 0.0))
        l_s = _split_bf16(l_ab)
        t_inv = eye + l_ab
        l_pow = _bdmm(l_s, l_s, m_bf16)
        for i in range(n_sq):
            lp_s = _split_bf16(l_pow)
            t_inv = t_inv + _bdmm(lp_s, _split_bf16(t_inv), m_bf16)
            if i + 1 < n_sq:
                l_pow = _bdmm(lp_s, lp_s, m_bf16)
        t_s = _split_bf16(t_inv)
        p_mat = _bdmm(t_s, _split_bf16(a_t[rs]), m_bf16)
        q_mat = _bdmm(t_s, _split_bf16(_bdmm(_split_bf16(l_ak), v_s, m_bf16)), m_bf16)
        rhat_ref[rs, :] = r_t[rs] + _bdmm(m_rb_s, _split_bf16(p_mat), m_bf16)
        y0_ref[rs, :] = _bdmm(m_rb_s, _split_bf16(q_mat), m_bf16) + _bdmm(m_rk_s, v_s, m_bf16)
        ns = slice(j * n, (j + 1) * n)
        a_ref[ns, :] = _bd_tn(p_mat, b_h[rs], m_f32) + eye * w_c[j * chunk:j * chunk + 1, :]
        d_ref[ns, :] = _bd_tn(jnp.concatenate([q_mat, v_j], axis=0),
                              jnp.concatenate([b_h[rs], k_h[rs]], axis=0), m_f32)


def _rwkv_scan_body(rhat_ref, y0_ref, a_ref, d_ref, bonus_ref, g_ref, s0_ref, lng_ref, lnb_ref, bd_ref, mbf_ref,
                    o_ref, s_ref, s_scr, y_scr, *, chunk, groups, steps):
    n = RW_HEAD_DIM
    c = pl.program_id(1)

    @pl.when(c == 0)
    def _():
        for h in range(RW_HEADS):
            s_scr[:, h * n:(h + 1) * n] = s0_ref[0, h]

    mask = mbf_ref[...]
    s = s_scr[...]
    for j in range(groups):
        rs = slice(j * chunk, (j + 1) * chunk)
        ns = slice(j * n, (j + 1) * n)
        s_s = _split_bf16(s)
        y_scr[rs, :] = y0_ref[rs, :] + _bdmm(_split_bf16(rhat_ref[rs, :]), s_s, mask, _dot_nt, extended=True)
        s = _bdmm(s_s, _split_bf16(a_ref[ns, :]), mask, extended=True) + d_ref[ns, :]
    s_scr[...] = s

    @pl.when(c == steps - 1)
    def _():
        for h in range(RW_HEADS):
            s_ref[0, h] = s[:, h * n:(h + 1) * n]

    bd = bd_ref[...]
    y = y_scr[...]
    mean = _dot_ones_rhs(y, bd) * (1.0 / n)
    yc = y - mean
    var = _dot_ones_rhs(yc * yc, bd) * (1.0 / n)
    yn = yc * lax.rsqrt(var + RW_GN_EPS) * lng_ref[...] + lnb_ref[...]
    o_ref[...] = ((yn + bonus_ref[...]) * g_ref[...]).astype(o_ref.dtype)


def _rwkv(rw_all, batch, t, t_valid, prev, s0, lp, o_prev):
    proj = rw_all.shape[1]
    n_all = o_prev.shape[0]
    row_off = 0
    n = RW_HEAD_DIM
    chunk = RW_CHUNK
    assert chunk == n and t % chunk == 0 and (t_valid == t or t == chunk)
    n_sq = chunk.bit_length() - 2
    nc = t // chunk
    gp = min(RW_PREP_CHUNKS, nc)
    gs = min(RW_SCAN_CHUNKS, nc)
    rows_p = gp * chunk
    rows_s = gs * chunk
    n_tok = batch * t
    row = lambda x: x.reshape(1, -1).astype(F32)
    r = lax.broadcasted_iota(jnp.int32, (rows_p, rows_p), 0)
    c = lax.broadcasted_iota(jnp.int32, (rows_p, rows_p), 1)
    tri = ((c <= r) & (c // chunk == r // chunk)).astype(BF16)
    hr = lax.broadcasted_iota(jnp.int32, (RW_WIDTH, RW_WIDTH), 0) // n
    hc = lax.broadcasted_iota(jnp.int32, (RW_WIDTH, RW_WIDTH), 1) // n
    bd = (hr == hc).astype(BF16)
    group_mask = (hr == hc)[:GROUP_WIDTH, :GROUP_WIDTH]
    const = lambda shape: pl.BlockSpec(shape, lambda b, c: (0,) * len(shape))

    blk_p = row_off // rows_p
    prep_args = [rw_all, rw_all, prev.astype(F32), row(lp['rw_mu']), row(lp['rw_w0']), lp['rw_w2'].astype(F32),
                 row(lp['rw_a0']), lp['rw_a2'].astype(F32), lp['rw_g2'].astype(BF16), row(lp['rw_k_k']),
                 row(lp['rw_k_a']), row(lp['rw_r_k']), tri, bd, group_mask.astype(BF16), group_mask.astype(F32)]
    tok_spec = lambda rows: pl.BlockSpec((rows, RW_WIDTH), lambda b, c: (b * (t // rows) + c, 0))
    mat_spec = lambda g: pl.BlockSpec((g * n, RW_WIDTH), lambda b, c: (b * (nc // g) + c, 0))
    tok_shape = jax.ShapeDtypeStruct((n_tok, RW_WIDTH), F32)
    mat_shape = jax.ShapeDtypeStruct((batch * nc * n, RW_WIDTH), F32)
    rhat, y0, a_mat, d_mat, bonus, g = pl.pallas_call(
        functools.partial(_rwkv_prep_body, chunk=chunk, groups=gp, n_sq=n_sq, t_valid=t_valid),
        grid=(batch, nc // gp),
        in_specs=[
            pl.BlockSpec((rows_p, proj), lambda b, c: (blk_p + b * (nc // gp) + c, 0)),
            pl.BlockSpec((8, proj), lambda b, c: (jnp.maximum((row_off + b * t + c * rows_p) // 8 - 1, 0), 0)),
            pl.BlockSpec((1, 1, proj), lambda b, c: (b, 0, 0)),
        ] + [const(x.shape) for x in prep_args[3:]],
        out_specs=[tok_spec(rows_p), tok_spec(rows_p), mat_spec(gp), mat_spec(gp), tok_spec(rows_p),
                   tok_spec(rows_p)],
        out_shape=[tok_shape, tok_shape, mat_shape, mat_shape, tok_shape, tok_shape],
        compiler_params=_params(("parallel", "parallel")),
        name="rwkv7_prep",
    )(*prep_args)

    blk_s = row_off // rows_s
    scan_args = [rhat, y0, a_mat, d_mat, bonus, g, s0.astype(F32), row(lp['rw_ln_g']), row(lp['rw_ln_b']), bd,
                 group_mask.astype(BF16), o_prev]
    state_spec = pl.BlockSpec((1, RW_HEADS, n, n), lambda b, c: (b, 0, 0, 0))
    n_in = len(scan_args) - 1

    def scan_body(*refs):
        _rwkv_scan_body(*refs[:n_in], *refs[n_in + 1:], chunk=chunk, groups=gs, steps=nc // gs)

    o, state = pl.pallas_call(
        scan_body,
        grid=(batch, nc // gs),
        in_specs=[tok_spec(rows_s), tok_spec(rows_s), mat_spec(gs), mat_spec(gs), tok_spec(rows_s),
                  tok_spec(rows_s), state_spec, const((1, RW_WIDTH)), const((1, RW_WIDTH)), const(bd.shape),
                  const(group_mask.shape), pl.BlockSpec(memory_space=pl.ANY)],
        out_specs=[pl.BlockSpec((rows_s, RW_WIDTH), lambda b, c: (blk_s + b * (nc // gs) + c, 0)), state_spec],
        out_shape=[jax.ShapeDtypeStruct((n_all, RW_WIDTH), BF16),
                   jax.ShapeDtypeStruct((batch, RW_HEADS, n, n), F32)],
        scratch_shapes=[pltpu.VMEM((n, RW_WIDTH), F32), pltpu.VMEM((rows_s, RW_WIDTH), F32)],
        input_output_aliases={n_in: 0},
        compiler_params=_params(("parallel", "arbitrary")),
        name="rwkv7_scan",
    )(*scan_args)
    return o, state


def _merge_body(oa_ref, ob_ref, gab_ref, xa_ref, xb_ref, wa_ref, wb_ref, wo_ref, g2_ref, wr_ref, br_ref, tri_ref,
                x2_ref, h2_ref, idx_ref, gate_ref, rank_ref, cnt_ref, carry_scr, *, tiles_a):
    i = pl.program_id(0)
    d = xa_ref.shape[-1]

    @pl.when(i == 0)
    def _():
        carry_scr[...] = jnp.zeros_like(carry_scr)

    gab = gab_ref[...]
    merged = (jax.nn.sigmoid(gab[:, :d]) * _dot(oa_ref[...], wa_ref[...])
              + jax.nn.sigmoid(gab[:, d:]) * _dot(ob_ref[...], wb_ref[...]))
    x = jnp.where(i < tiles_a, xa_ref[...], xb_ref[...])
    x2 = x + _dot(merged.astype(BF16), wo_ref[...])
    x2_ref[...] = x2
    h2 = _rmsnorm(x2, g2_ref[...])
    h2_ref[...] = h2
    logits = _dot3(h2, wr_ref[...]) + br_ref[...]

    tm, ne = logits.shape
    col = lax.broadcasted_iota(jnp.int32, (tm, ne), 1)
    c4 = lax.broadcasted_iota(jnp.int32, (tm, TOP_K), 1)
    work = logits
    tops, idxs = [], []
    for _ in range(TOP_K):
        m = jnp.max(work, axis=-1, keepdims=True)
        ix = jnp.min(jnp.where(work == m, col, ne), axis=-1, keepdims=True)
        tops.append(m)
        idxs.append(ix)
        work = jnp.where(col == ix, -jnp.inf, work)
    es = [jnp.exp(m - tops[0]) for m in tops]
    denom = es[0] + es[1] + es[2] + es[3]
    onehot = jnp.zeros((tm, ne), F32)
    for ix in idxs:
        onehot = onehot + (col == ix).astype(F32)
    before = carry_scr[...] + _dot(tri_ref[...], onehot.astype(BF16))
    idx_out = jnp.zeros((tm, TOP_K), jnp.int32)
    gate_out = jnp.zeros((tm, TOP_K), F32)
    rank_out = jnp.zeros((tm, TOP_K), jnp.int32)
    for kk in range(TOP_K):
        rk = jnp.sum(jnp.where(col == idxs[kk], before, 0.0), axis=-1, keepdims=True)
        idx_out = jnp.where(c4 == kk, idxs[kk], idx_out)
        gate_out = jnp.where(c4 == kk, es[kk] / denom, gate_out)
        rank_out = jnp.where(c4 == kk, rk.astype(jnp.int32), rank_out)
    idx_ref[...] = idx_out
    gate_ref[...] = gate_out
    rank_ref[...] = rank_out
    carry = carry_scr[...] + jnp.sum(onehot, axis=0, keepdims=True)
    carry_scr[...] = carry
    cnt_ref[...] = carry.astype(jnp.int32)


def _merge_route(oa, ob, gab, xa, xb, lp, tm):
    d = xa.shape[1]
    n = xa.shape[0] + xb.shape[0]
    assert xa.shape[0] % tm == 0 and xb.shape[0] % tm == 0
    tiles_a = xa.shape[0] // tm
    ne = N_EXPERTS
    r = lax.broadcasted_iota(jnp.int32, (tm, tm), 0)
    c = lax.broadcasted_iota(jnp.int32, (tm, tm), 1)
    tri = (c < r).astype(BF16)
    const = lambda shape: pl.BlockSpec(shape, lambda i: (0,) * len(shape))
    rowblk = lambda wd: pl.BlockSpec((tm, wd), lambda i: (i, 0))
    wa = lp['w_branch_a'].astype(BF16)
    wb = lp['w_branch_b'].astype(BF16)
    wo = lp['w_out'].astype(BF16)
    return pl.pallas_call(
        functools.partial(_merge_body, tiles_a=tiles_a),
        grid=(n // tm,),
        in_specs=[rowblk(oa.shape[1]), rowblk(ob.shape[1]), rowblk(2 * d),
                  pl.BlockSpec((tm, d), lambda i: (jnp.minimum(i, tiles_a - 1), 0)),
                  pl.BlockSpec((tm, d), lambda i: (jnp.maximum(i - tiles_a, 0), 0)),
                  const(wa.shape), const(wb.shape), const(wo.shape), const((1, d)), const((d, ne)),
                  const((1, ne)), const((tm, tm))],
        out_specs=[rowblk(d), rowblk(d), rowblk(TOP_K), rowblk(TOP_K), rowblk(TOP_K), const((1, ne))],
        out_shape=[jax.ShapeDtypeStruct((n, d), F32), jax.ShapeDtypeStruct((n, d), F32),
                   jax.ShapeDtypeStruct((n, TOP_K), jnp.int32), jax.ShapeDtypeStruct((n, TOP_K), F32),
                   jax.ShapeDtypeStruct((n, TOP_K), jnp.int32), jax.ShapeDtypeStruct((1, ne), jnp.int32)],
        scratch_shapes=[pltpu.VMEM((1, ne), F32)],
        compiler_params=_params(("arbitrary",)),
        name="merge_route",
    )(oa, ob, gab, xa, xb, wa, wb, wo, lp['norm2_g'].reshape(1, d).astype(F32), lp['w_router'].astype(F32),
      lp['b_router'].reshape(1, ne).astype(F32), tri)


def _dispatch_body(dest_ref, end_ref, padded_ref, h_ref, xs_ref, zeros_ref, sem, zsem, *, tile, bm):
    @pl.when(pl.program_id(0) == 0)
    def _():
        zeros_ref[...] = jnp.zeros_like(zeros_ref)

        def clear(e):
            start = pl.multiple_of(end_ref[e] - bm, bm)
            return pltpu.make_async_copy(zeros_ref, xs_ref.at[pl.ds(start, bm)], zsem)

        for e in range(N_EXPERTS):
            @pl.when(padded_ref[e] > 0)
            def _():
                clear(e).start()

        for e in range(N_EXPERTS):
            @pl.when(padded_ref[e] > 0)
            def _():
                clear(e).wait()

    def copy(t, kk):
        return pltpu.make_async_copy(h_ref.at[pl.ds(t, 1)],
                                     xs_ref.at[pl.ds(dest_ref[0, 0, t * TOP_K + kk], 1)], sem)

    def issue(t, carry):
        for kk in range(TOP_K):
            copy(t, kk).start(priority=kk % 2)
        return carry

    def drain(t, carry):
        for kk in range(TOP_K):
            copy(t, kk).wait()
        return carry

    lax.fori_loop(0, tile, issue, 0, unroll=DMA_UNROLL)
    lax.fori_loop(0, tile, drain, 0, unroll=DMA_UNROLL)


def _dispatch(h2, dest, pad_end, padded, n_rows, tile, bm):
    n, d = h2.shape
    nt = n // tile
    smem = pl.BlockSpec(memory_space=pltpu.SMEM)
    return pl.pallas_call(
        functools.partial(_dispatch_body, tile=tile, bm=bm),
        grid=(nt,),
        in_specs=[pl.BlockSpec((1, 1, tile * TOP_K), lambda i: (i, 0, 0), memory_space=pltpu.SMEM), smem, smem,
                  pl.BlockSpec((tile, d), lambda i: (i, 0))],
        out_specs=pl.BlockSpec(memory_space=pl.ANY),
        out_shape=jax.ShapeDtypeStruct((n_rows, d), F32),
        scratch_shapes=[pltpu.VMEM((bm, d), F32), pltpu.SemaphoreType.DMA, pltpu.SemaphoreType.DMA],
        compiler_params=_params(("arbitrary",)),
        name="moe_dispatch",
    )(dest.reshape(nt, 1, tile * TOP_K), pad_end.astype(jnp.int32), padded.astype(jnp.int32), h2)


def _expert_body(be_ref, used_ref, xs_ref, wgu_ref, bgu_ref, wd_ref, bd_ref, ys_ref, wgu16, wd16):
    i = pl.program_id(0)
    de = wd_ref.shape[1]

    @pl.when(jnp.logical_or(i == 0, be_ref[i] != be_ref[jnp.maximum(i - 1, 0)]))
    def _():
        wgu16[...] = wgu_ref[0].astype(BF16)
        wd16[...] = wd_ref[0].astype(BF16)

    @pl.when(i < used_ref[0])
    def _():
        gu = _dot(xs_ref[...].astype(BF16), wgu16[...]) + bgu_ref[0]
        g = jnp.minimum(gu[:, :de], SWIGLU_LIMIT)
        u = jnp.clip(gu[:, de:], -SWIGLU_LIMIT, SWIGLU_LIMIT)
        act = (u + 1.0) * (g * jax.nn.sigmoid(g * SWIGLU_ALPHA))
        ys_ref[...] = _dot(act.astype(BF16), wd16[...]) + bd_ref[0]

    @pl.when(i >= used_ref[0])
    def _():
        ys_ref[...] = jnp.zeros_like(ys_ref)


def _experts(xs, block_expert, n_used, wgu, bgu, wd, bd, bm):
    n_rows, d = xs.shape
    ne, _, de2 = wgu.shape
    nb = n_rows // bm
    grid_spec = pltpu.PrefetchScalarGridSpec(
        num_scalar_prefetch=2,
        grid=(nb,),
        in_specs=[
            pl.BlockSpec((bm, d), lambda i, be, nu: (i, 0)),
            pl.BlockSpec((1, d, de2), lambda i, be, nu: (be[i], 0, 0)),
            pl.BlockSpec((1, 1, de2), lambda i, be, nu: (be[i], 0, 0)),
            pl.BlockSpec((1, de2 // 2, d), lambda i, be, nu: (be[i], 0, 0)),
            pl.BlockSpec((1, 1, d), lambda i, be, nu: (be[i], 0, 0)),
        ],
        out_specs=pl.BlockSpec((bm, d), lambda i, be, nu: (i, 0)),
        scratch_shapes=[pltpu.VMEM((d, de2), BF16), pltpu.VMEM((de2 // 2, d), BF16)],
    )
    return pl.pallas_call(
        _expert_body,
        grid_spec=grid_spec,
        out_shape=jax.ShapeDtypeStruct((n_rows, d), F32),
        compiler_params=_params(("arbitrary",)),
        name="moe_experts",
    )(block_expert, n_used, xs, wgu, bgu.reshape(ne, 1, de2), wd, bd.reshape(ne, 1, d))


def _combine_body(dest_ref, ys_ref, gate_ref, x2_ref, g_ref, ya_ref, yb_ref, buf, sem, *, tile, tiles_a):
    def copy(t, kk):
        return pltpu.make_async_copy(ys_ref.at[pl.ds(dest_ref[0, 0, t * TOP_K + kk], 1)],
                                     buf.at[kk, pl.ds(t, 1)], sem)

    def issue(t, carry):
        for kk in range(TOP_K):
            copy(t, kk).start(priority=kk % 2)
        return carry

    def drain(t, carry):
        for kk in range(TOP_K):
            copy(t, kk).wait()
        return carry

    lax.fori_loop(0, tile, issue, 0, unroll=DMA_UNROLL)
    lax.fori_loop(0, tile, drain, 0, unroll=DMA_UNROLL)
    gate = gate_ref[...]
    moe = gate[:, 0:1] * buf[0]
    for kk in range(1, TOP_K):
        moe = moe + gate[:, kk:kk + 1] * buf[kk]
    y = _rmsnorm(x2_ref[...] + moe, g_ref[...])
    i = pl.program_id(0)

    @pl.when(i < tiles_a)
    def _():
        ya_ref[...] = y

    @pl.when(i >= tiles_a)
    def _():
        yb_ref[...] = y


def _combine(ys, dest, gate, x2, final_g, tile, n_a):
    n, d = x2.shape
    nt = n // tile
    tiles_a = n_a // tile
    return pl.pallas_call(
        functools.partial(_combine_body, tile=tile, tiles_a=tiles_a),
        grid=(nt,),
        in_specs=[pl.BlockSpec((1, 1, tile * TOP_K), lambda i: (i, 0, 0), memory_space=pltpu.SMEM),
                  pl.BlockSpec(memory_space=pl.ANY),
                  pl.BlockSpec((tile, TOP_K), lambda i: (i, 0)),
                  pl.BlockSpec((tile, d), lambda i: (i, 0)),
                  pl.BlockSpec((1, d), lambda i: (0, 0))],
        out_specs=[pl.BlockSpec((tile, d), lambda i: (jnp.minimum(i, tiles_a - 1), 0)),
                   pl.BlockSpec((tile, d), lambda i: (jnp.maximum(i - tiles_a, 0), 0))],
        out_shape=[jax.ShapeDtypeStruct((n_a, d), F32), jax.ShapeDtypeStruct((n - n_a, d), F32)],
        scratch_shapes=[pltpu.VMEM((TOP_K, tile, d), F32), pltpu.SemaphoreType.DMA],
        compiler_params=_params(("arbitrary",)),
        name="moe_combine",
    )(dest.reshape(nt, 1, tile * TOP_K), ys, gate, x2, final_g.reshape(1, d).astype(F32))


def _moe(h2, x2, idx, gate, rank, counts, lp, final_g, n_a):
    n, d = x2.shape
    bm = EXPERT_ROWS
    counts = counts.reshape(N_EXPERTS)
    padded = (counts + bm - 1) // bm * bm
    pad_end = jnp.cumsum(padded)
    dest = ((pad_end - padded)[idx] + rank).reshape(n * TOP_K).astype(jnp.int32)
    n_blocks = -(-(n * TOP_K) // bm) + N_EXPERTS
    block_start = jnp.arange(n_blocks, dtype=jnp.int32) * bm
    block_expert = jnp.minimum(jnp.sum(pad_end[None, :] <= block_start[:, None], axis=1),
                               N_EXPERTS - 1).astype(jnp.int32)
    n_used = (pad_end[-1:] // bm).astype(jnp.int32)
    xs = _dispatch(h2, dest, pad_end, padded, n_blocks * bm, DISPATCH_TILE, bm)
    ys = _experts(xs, block_expert, n_used, lp['w_gate_up'].astype(F32), lp['b_gate_up'].astype(F32),
                  lp['w_down'].astype(F32), lp['b_down'].astype(F32), bm)
    return _combine(ys, dest, gate, x2, final_g, COMBINE_TILE, n_a)


def _to_heads(t, b, s):
    return t.reshape(b, s, SB_HEADS, SB_HEAD_DIM).transpose(0, 2, 1, 3)


def _from_heads(t):
    b, h, s, dh = t.shape
    return t.transpose(0, 2, 1, 3).reshape(b, s, h * dh)


def _layer(x_p, x_s, cache_k, cache_v, state_rwkv, state_shift, lp, final_g):
    bp, tp, d = x_p.shape
    bs, ts, _ = x_s.shape
    sbw = SB_HEADS * SB_HEAD_DIM
    rw_proj = lp['rw_mu'].shape[0]
    n_p = bp * tp
    n = n_p + bs * ts
    x_s = x_s.reshape(bs * ts, d)
    scale = LOG2E * SB_HEAD_DIM ** -0.5
    qkv16, kp, vp, kv_new, rw, gab = _inproj(x_p, x_s, lp['norm1_g'].astype(F32), lp['w_in'].astype(BF16),
                                             rw_proj, scale, ROW_TILE)
    pairs = SB_HEADS // 2
    oa = _stick_breaking(qkv16, 0, 0, qkv16, pairs, 2 * pairs, bp, tp, 0, SB_BLOCK, SB_SUB,
                         jnp.zeros((n, sbw), BF16))
    past = cache_k.shape[2]
    new16 = qkv16[n_p:].reshape(bs, ts, 3 * sbw)
    kv_s = jnp.concatenate([
        jnp.concatenate([_from_heads(cache_k).astype(BF16), new16[:, :, sbw:2 * sbw]], axis=1),
        jnp.concatenate([_from_heads(cache_v).astype(BF16), new16[:, :, 2 * sbw:]], axis=1)], axis=2)
    oa = _stick_breaking(qkv16, n_p, 0, kv_s.reshape(bs * (past + ts), 2 * sbw), 0, pairs, bs, ts, past, ts,
                         min(SB_SUB, past), oa)
    ks, vs = kv_new[:, :sbw], kv_new[:, sbw:]

    s0_p = jnp.zeros((bp, RW_HEADS, RW_HEAD_DIM, RW_HEAD_DIM), F32)
    prev_p = jnp.zeros((bp, 1, rw_proj), F32)
    ob, st_p = _rwkv(rw, bp, tp, tp, prev_p, s0_p, lp, jnp.zeros((n, RW_WIDTH), BF16))
    rw_s = rw[n_p:].reshape(bs, ts, rw_proj)
    ts_pad = -(-ts // RW_CHUNK) * RW_CHUNK
    rw_s_pad = jnp.pad(rw_s, ((0, 0), (0, ts_pad - ts), (0, 0))).reshape(bs * ts_pad, rw_proj)
    ob_s, st_s = _rwkv(rw_s_pad, bs, ts_pad, ts, state_shift, state_rwkv, lp,
                       jnp.zeros((bs * ts_pad, RW_WIDTH), BF16))
    ob = lax.dynamic_update_slice(
        ob, ob_s.reshape(bs, ts_pad, RW_WIDTH)[:, :ts].reshape(bs * ts, RW_WIDTH), (n_p, 0))
    sh_p = jnp.stack([rw[(b + 1) * tp - 1:(b + 1) * tp] for b in range(bp)])
    sh_s = rw_s[:, ts - 1:]

    x2, h2, idx, gate, rank, counts = _merge_route(oa, ob, gab, x_p.reshape(n_p, d), x_s, lp, MERGE_TILE)
    y = _moe(h2, x2, idx, gate, rank, counts, lp, final_g, n_p)
    return (y, kp, vp, st_p, sh_p, _to_heads(ks, bs, ts), _to_heads(vs, bs, ts), st_s, sh_s)


def kernel(x_prompt, x_sample, cache_sb_k, cache_sb_v, state_rwkv, state_shift, norm1_g, w_in, rw_mu, rw_w0, rw_w2, rw_a0, rw_a2, rw_g2, rw_k_k, rw_k_a, rw_r_k, rw_ln_g, rw_ln_b, w_branch_a, w_branch_b, w_out, norm2_g, w_router, b_router, w_gate_up, b_gate_up, w_down, b_down, final_norm_g):
    depth = w_in.shape[0]
    assert depth == 1, "the final RMSNorm is fused into the last (only) layer"
    bp, tp, d = x_prompt.shape
    bs, ts, _ = x_sample.shape
    lp = dict(norm1_g=norm1_g[0], w_in=w_in[0], rw_mu=rw_mu[0], rw_w0=rw_w0[0], rw_w2=rw_w2[0], rw_a0=rw_a0[0],
              rw_a2=rw_a2[0], rw_g2=rw_g2[0], rw_k_k=rw_k_k[0], rw_k_a=rw_k_a[0], rw_r_k=rw_r_k[0].reshape(-1),
              rw_ln_g=rw_ln_g[0], rw_ln_b=rw_ln_b[0], w_branch_a=w_branch_a[0], w_branch_b=w_branch_b[0],
              w_out=w_out[0], norm2_g=norm2_g[0], w_router=w_router[0], b_router=b_router[0],
              w_gate_up=w_gate_up[0], b_gate_up=b_gate_up[0], w_down=w_down[0], b_down=b_down[0])
    y, kp, vp, st_p, sh_p, ks, vs, st_s, sh_s = _layer(
        x_prompt, x_sample, cache_sb_k[0], cache_sb_v[0], state_rwkv[0], state_shift[0], lp, final_norm_g)
    return (y[0].reshape(bp, tp, d), y[1].reshape(bs, ts, d),
            kp[None], vp[None], st_p[None], sh_p[None], ks[None], vs[None], st_s[None], sh_s[None])
```

```python
import functools

import jax
import jax.numpy as jnp
from jax import lax
from jax.experimental import pallas as pl
from jax.experimental.pallas import tpu as pltpu

F32 = jnp.float32
BF16 = jnp.bfloat16

SB_HEADS = 8
SB_HEAD_DIM = 64
RW_HEADS = 8
RW_HEAD_DIM = 64
RW_WIDTH = RW_HEADS * RW_HEAD_DIM
RW_DECAY_LORA = 64
RW_AAA_LORA = 64
RW_GATE_LORA = 128
RW_GN_EPS = 64e-5
N_EXPERTS = 32
TOP_K = 4
SWIGLU_LIMIT = 7.0
SWIGLU_ALPHA = 1.702
RMS_EPS = 1e-6

VMEM_LIMIT_BYTES = 56 * 1024 * 1024
ROW_TILE = 256
MERGE_TILE = 512
SB_BLOCK = 512
SB_SUB = 256
LOG2E = 1.4426950408889634
SB_DEAD_LOG2 = -200.0
RW_CHUNK = 64
RW_PREP_CHUNKS = 4
RW_SCAN_CHUNKS = 4
EXPERT_ROWS = 512
DISPATCH_TILE = 512
COMBINE_TILE = 128
DMA_UNROLL = 8


def _params(semantics):
    return pltpu.CompilerParams(dimension_semantics=semantics, vmem_limit_bytes=VMEM_LIMIT_BYTES)


def _dot(a, b):
    return jnp.dot(a, b, preferred_element_type=F32)


def _dot_nt(a, b):
    return lax.dot_general(a, b, (((1,), (1,)), ((), ())), preferred_element_type=F32)


def _softplus_parts(z):
    l = jnp.log(1.0 + jnp.exp(-jnp.abs(z)))
    sp = jnp.maximum(z, 0.0) + l
    return sp, sp - z


def _rmsnorm(x, g):
    return x * lax.rsqrt(jnp.mean(x * x, axis=-1, keepdims=True) + RMS_EPS) * g


def _inproj_body(xa_ref, xb_ref, g_ref, w_ref, qkv16_ref, ka_ref, va_ref, kvb_ref, rw_ref, gab_ref, *,
                 q_scale, tiles_a):
    i = pl.program_id(0)
    x = jnp.where(i < tiles_a, xa_ref[...], xb_ref[...])
    h = _rmsnorm(x, g_ref[...]).astype(BF16)
    sbw = kvb_ref.shape[-1] // 2
    dh = ka_ref.shape[-1]
    q = _dot(h, w_ref[:, 0:sbw])
    qkv16_ref[:, 0:sbw] = (q * q_scale).astype(BF16)
    k = _dot(h, w_ref[:, sbw:2 * sbw])
    v = _dot(h, w_ref[:, 2 * sbw:3 * sbw])
    qkv16_ref[:, sbw:2 * sbw] = k.astype(BF16)
    qkv16_ref[:, 2 * sbw:3 * sbw] = v.astype(BF16)
    off = 3 * sbw
    for ref in (rw_ref, gab_ref):
        width = ref.shape[-1]
        for c0 in range(0, width, sbw):
            cw = min(sbw, width - c0)
            ref[:, c0:c0 + cw] = _dot(h, w_ref[:, off + c0:off + c0 + cw])
        off += width

    @pl.when(i < tiles_a)
    def _():
        for hd in range(sbw // dh):
            ka_ref[0, hd] = k[:, hd * dh:(hd + 1) * dh]
            va_ref[0, hd] = v[:, hd * dh:(hd + 1) * dh]

    @pl.when(i >= tiles_a)
    def _():
        kvb_ref[:, 0:sbw] = k
        kvb_ref[:, sbw:2 * sbw] = v


def _inproj(xa, xb, g, w_bf16, rw_proj, q_scale, tm):
    ba, ta, d = xa.shape
    nb = xb.shape[0]
    n = ba * ta + nb
    sbw = SB_HEADS * SB_HEAD_DIM
    total = w_bf16.shape[1]
    assert total == 3 * sbw + rw_proj + 2 * d and ta % tm == 0 and nb % tm == 0
    tiles_a = ba * ta // tm
    per_seq = ta // tm
    last_a = tiles_a - 1
    row = lambda wd: pl.BlockSpec((tm, wd), lambda i: (i, 0))
    head_major = pl.BlockSpec(
        (1, SB_HEADS, tm, SB_HEAD_DIM),
        lambda i: (jnp.minimum(i, last_a) // per_seq, 0, jnp.minimum(i, last_a) % per_seq, 0))
    hm_shape = jax.ShapeDtypeStruct((ba, SB_HEADS, ta, SB_HEAD_DIM), F32)
    return pl.pallas_call(
        functools.partial(_inproj_body, q_scale=q_scale, tiles_a=tiles_a),
        grid=(n // tm,),
        in_specs=[
            pl.BlockSpec((tm, d), lambda i: (jnp.minimum(i, last_a), 0)),
            pl.BlockSpec((tm, d), lambda i: (jnp.maximum(i - tiles_a, 0), 0)),
            pl.BlockSpec((1, d), lambda i: (0, 0)),
            pl.BlockSpec((d, total), lambda i: (0, 0), pipeline_mode=pl.Buffered(1)),
        ],
        out_specs=[row(3 * sbw), head_major, head_major,
                   pl.BlockSpec((tm, 2 * sbw), lambda i: (jnp.maximum(i - tiles_a, 0), 0)),
                   row(rw_proj), row(2 * d)],
        out_shape=[jax.ShapeDtypeStruct((n, 3 * sbw), BF16), hm_shape, hm_shape,
                   jax.ShapeDtypeStruct((nb, 2 * sbw), F32),
                   jax.ShapeDtypeStruct((n, rw_proj), F32), jax.ShapeDtypeStruct((n, 2 * d), F32)],
        compiler_params=_params(("arbitrary",)),
        name="inproj",
    )(xa.reshape(ba * ta, d), xb, g.reshape(1, d), w_bf16)


def _sb_tile(q, kb, vb, ntri, carry, mask, sb):
    tk = kb.shape[0]
    z = _dot_nt(q, kb)
    neg_abs = lax.bitcast_convert_type(lax.bitcast_convert_type(z, jnp.uint32) | jnp.uint32(0x80000000), F32)
    sp = jnp.maximum(z, 0.0) + jnp.log(1.0 + jnp.exp2(neg_abs)) * LOG2E
    if mask is not None:
        sp = jnp.where(mask, sp, 0.0)
    hi = sp.astype(BF16)
    lo = (sp - hi.astype(F32)).astype(BF16)
    log_beta = z - sp
    wts = []
    for j in range(tk // sb - 1, -1, -1):
        cols = slice(j * sb, (j + 1) * sb)
        later = _dot(hi[:, cols], ntri) + _dot(lo[:, cols], ntri) + carry
        w = jnp.exp2(log_beta[:, cols] + later)
        if mask is not None:
            w = jnp.where(mask[:, cols], w, 0.0)
        wts.append(w.astype(BF16))
        carry = carry - jnp.sum(sp[:, cols], axis=-1, keepdims=True)
    wts = wts[0] if len(wts) == 1 else jnp.concatenate(wts[::-1], axis=1)
    return _dot(wts, vb), carry


def _sb_body(q_ref, k_ref, v_ref, triq_ref, trik_ref, o_in_ref, o_ref, *, tq, tk, past):
    del o_in_ref
    qi = pl.program_id(2)
    q = q_ref[...]
    first = lax.broadcasted_iota(jnp.int32, q.shape, 1) < SB_HEAD_DIM
    zero = jnp.zeros_like(q)
    qs = (jnp.where(first, q, zero), jnp.where(first, zero, q))
    d0 = pl.multiple_of(past + qi * tq, tq)
    row = lax.broadcasted_iota(jnp.int32, (tq, tq), 0)
    col = lax.broadcasted_iota(jnp.int32, (tq, tq), 1)
    kd = k_ref[pl.ds(d0, tq), :]
    vd = v_ref[pl.ds(d0, tq), :]
    sbq = triq_ref.shape[0]
    state = []
    for qh in qs:
        state.extend(_sb_tile(qh, kd, vd, triq_ref[...], jnp.zeros((tq, 1), F32), col < row, sbq))
    nb = (past + qi * tq) // tk
    sbk = trik_ref.shape[0]

    def live(state):
        return jnp.max(jnp.maximum(state[1], state[3])) > SB_DEAD_LOG2

    def cond(loop):
        i, alive, _ = loop
        return jnp.logical_and(i < nb, alive)

    def step(loop):
        i, _, state = loop
        s0 = pl.multiple_of((nb - 1 - i) * tk, tk)
        kb = k_ref[pl.ds(s0, tk), :]
        vb = v_ref[pl.ds(s0, tk), :]
        new = []
        for h, qh in enumerate(qs):
            out, carry = _sb_tile(qh, kb, vb, trik_ref[...], state[2 * h + 1], None, sbk)
            new.extend((state[2 * h] + out, carry))
        return i + 1, live(new), tuple(new)

    _, _, state = lax.while_loop(cond, step, (jnp.int32(0), live(state), tuple(state)))
    o_ref[...] = jnp.where(first, state[0], state[2]).astype(o_ref.dtype)


def _later_matrix(n):
    r = lax.broadcasted_iota(jnp.int32, (n, n), 0)
    c = lax.broadcasted_iota(jnp.int32, (n, n), 1)
    return jnp.where(r > c, -1.0, 0.0).astype(BF16)


def _stick_breaking(q, q_row0, q_lane0, kv, k_lane0, v_lane0, batch, t, past, tq, tk, o_prev):
    s = past + t
    pair = 2 * SB_HEAD_DIM
    sbq = min(tq, SB_SUB)
    sbk = min(tk, SB_SUB)
    nq = t // tq
    qblk = q_row0 // tq
    q_spec = lambda lane0: pl.BlockSpec((tq, pair), lambda b, h, i: (qblk + b * nq + i, lane0 + h))
    kv_spec = lambda lane0: pl.BlockSpec((s, pair), lambda b, h, i: (b, lane0 + h))
    return pl.pallas_call(
        functools.partial(_sb_body, tq=tq, tk=tk, past=past),
        grid=(batch, SB_HEADS // 2, nq),
        in_specs=[
            q_spec(q_lane0), kv_spec(k_lane0), kv_spec(v_lane0),
            pl.BlockSpec((sbq, sbq), lambda b, h, i: (0, 0)),
            pl.BlockSpec((sbk, sbk), lambda b, h, i: (0, 0)),
            pl.BlockSpec(memory_space=pl.ANY),
        ],
        out_specs=q_spec(0),
        out_shape=jax.ShapeDtypeStruct(o_prev.shape, BF16),
        input_output_aliases={5: 0},
        compiler_params=_params(("parallel", "parallel", "arbitrary")),
        name="stick_breaking",
    )(q, kv, kv, _later_matrix(sbq), _later_matrix(sbk), o_prev)


def _split_bf16(x):
    hi = x.astype(BF16)
    return hi, (x - hi.astype(F32)).astype(BF16)


def _dot3(a, b, dot=_dot):
    ah, al = _split_bf16(a)
    bh, bl = _split_bf16(b)
    return dot(ah, bh) + dot(ah, bl) + dot(al, bh)


def _dot_ones_rhs(x, ones_bf16):
    hi, lo = _split_bf16(x)
    return _dot(hi, ones_bf16) + _dot(lo, ones_bf16)


def _dot_ones_lhs(ones_bf16, x):
    hi, lo = _split_bf16(x)
    return _dot(ones_bf16, hi) + _dot(ones_bf16, lo)


HEADS_PER_MATMUL = 4
GROUP_WIDTH = HEADS_PER_MATMUL * RW_HEAD_DIM


def _block_diag(y, mask):
    return jnp.concatenate([y] * HEADS_PER_MATMUL, axis=0) * mask


def _bdmm(xs, ys, mask, dot=_dot, extended=False):
    xh, xl = xs
    yh, yl = ys
    m = xh.shape[0]
    x2 = jnp.concatenate([xh, xl], axis=0) if extended else xh
    outs = []
    for g in range(RW_WIDTH // GROUP_WIDTH):
        ls = slice(g * GROUP_WIDTH, (g + 1) * GROUP_WIDTH)
        out = dot(x2[:, ls], _block_diag(yh[:, ls], mask))
        if extended:
            out = out[:m] + out[m:] + dot(xh[:, ls], _block_diag(yl[:, ls], mask))
        outs.append(out)
    return jnp.concatenate(outs, axis=1)


def _bd_tn(x, y, mask):
    n = RW_HEAD_DIM
    outs = []
    for g in range(RW_WIDTH // GROUP_WIDTH):
        ls = slice(g * GROUP_WIDTH, (g + 1) * GROUP_WIDTH)
        z = _dot3(x[:, ls].T, y[:, ls]) * mask
        outs.append(z[0:n] + z[n:2 * n] + z[2 * n:3 * n] + z[3 * n:4 * n])
    return jnp.concatenate(outs, axis=1)


def _rwkv_prep_body(p_ref, pb_ref, prev_ref, mu_ref, w0_ref, w2_ref, a0_ref, a2_ref, g2_ref, kk_ref, ka_ref,
                    rk_ref, tri_ref, bd_ref, mbf_ref, mf32_ref,
                    rhat_ref, y0_ref, a_ref, d_ref, bonus_ref, g_ref, *, chunk, groups, n_sq, t_valid):
    n = RW_HEAD_DIM
    w_ = RW_WIDTH
    p = p_ref[...]
    rows = lax.broadcasted_iota(jnp.int32, p.shape, 0)
    prev_row = jnp.where(pl.program_id(1) == 0, prev_ref[0], pb_ref[7:8, :])
    p_shift = jnp.where(rows == 0, prev_row, pltpu.roll(p, 1, 0))

    xs = p + (p_shift - p) * mu_ref[...]
    r = xs[:, 0:w_]
    k = xs[:, w_:2 * w_]
    v = xs[:, 2 * w_:3 * w_]
    o1 = 3 * w_
    dw = xs[:, o1:o1 + RW_DECAY_LORA]
    da = xs[:, o1 + RW_DECAY_LORA:o1 + RW_DECAY_LORA + RW_AAA_LORA]
    dg = xs[:, o1 + RW_DECAY_LORA + RW_AAA_LORA:]

    sp_neg = _softplus_parts(w0_ref[...] + _dot3(jnp.tanh(dw), w2_ref[...]))[1]
    log_decay = -jnp.exp(-sp_neg - 0.5)
    a = jax.nn.sigmoid(a0_ref[...] + _dot3(da, a2_ref[...]))
    g_ref[...] = _dot(jax.nn.sigmoid(dg).astype(BF16), g2_ref[...])

    bd = bd_ref[...]
    kk = k * kk_ref[...]
    kk = kk / jnp.maximum(jnp.sqrt(_dot_ones_rhs(kk * kk, bd)), 1e-12)
    k2 = k * (1.0 + (a - 1.0) * ka_ref[...])
    a_vec = -kk
    b_vec = kk * a
    bonus_ref[...] = _dot_ones_rhs(r * k2 * rk_ref[...], bd) * v
    if t_valid < chunk:
        live = lax.broadcasted_iota(jnp.int32, r.shape, 0) < t_valid
        keep = lambda x: jnp.where(live, x, 0.0)
        log_decay, r, k2, v, a_vec, b_vec = (keep(x) for x in (log_decay, r, k2, v, a_vec, b_vec))

    cw = _dot_ones_lhs(tri_ref[...], log_decay)
    cw_last = jnp.concatenate(
        [jnp.broadcast_to(cw[(j + 1) * chunk - 1:(j + 1) * chunk, :], (chunk, w_)) for j in range(groups)], axis=0)
    r_t = r * jnp.exp(cw)
    a_t = a_vec * jnp.exp(cw - log_decay)
    e_neg = jnp.exp(-cw)
    b_t = b_vec * e_neg
    k_t = k2 * e_neg
    e_rel = jnp.exp(cw_last - cw)
    b_h = b_vec * e_rel
    k_h = k2 * e_rel
    w_c = jnp.exp(cw_last)

    rr = lax.broadcasted_iota(jnp.int32, (chunk, w_), 0)
    ss = lax.broadcasted_iota(jnp.int32, (chunk, w_), 1) % n
    strict = ss < rr
    incl = ss <= rr
    eye = (ss == rr).astype(F32)
    m_bf16 = mbf_ref[...]
    m_f32 = mf32_ref[...]

    for j in range(groups):
        rs = slice(j * chunk, (j + 1) * chunk)
        v_j = v[rs]
        v_s = _split_bf16(v_j)
        ar_s = _split_bf16(jnp.concatenate([a_t[rs], r_t[rs]], axis=0))
        gram_b = _bdmm(ar_s, _split_bf16(b_t[rs]), m_bf16, _dot_nt)
        gram_k = _bdmm(ar_s, _split_bf16(k_t[rs]), m_bf16, _dot_nt)
        l_ab = jnp.where(strict, gram_b[:chunk], 0.0)
        l_ak = jnp.where(strict, gram_k[:chunk], 0.0)
        m_rb_s = _split_bf16(jnp.where(incl, gram_b[chunk:], 0.0))
        m_rk_s = _split_bf16(jnp.where(incl, gram_k[chunk:], 0.0))
        l_s = _split_bf16(l_ab)
        t_inv = eye + l_ab
        l_pow = _bdmm(l_s, l_s, m_bf16)
        for i in range(n_sq):
            lp_s = _split_bf16(l_pow)
            t_inv = t_inv + _bdmm(lp_s, _split_bf16(t_inv), m_bf16)
            if i + 1 < n_sq:
                l_pow = _bdmm(lp_s, lp_s, m_bf16)
        t_s = _split_bf16(t_inv)
        p_mat = _bdmm(t_s, _split_bf16(a_t[rs]), m_bf16)
        q_mat = _bdmm(t_s, _split_bf16(_bdmm(_split_bf16(l_ak), v_s, m_bf16)), m_bf16)
        rhat_ref[rs, :] = r_t[rs] + _bdmm(m_rb_s, _split_bf16(p_mat), m_bf16)
        y0_ref[rs, :] = _bdmm(m_rb_s, _split_bf16(q_mat), m_bf16) + _bdmm(m_rk_s, v_s, m_bf16)
        ns = slice(j * n, (j + 1) * n)
        a_ref[ns, :] = _bd_tn(p_mat, b_h[rs], m_f32) + eye * w_c[j * chunk:j * chunk + 1, :]
        d_ref[ns, :] = _bd_tn(jnp.concatenate([q_mat, v_j], axis=0),
                              jnp.concatenate([b_h[rs], k_h[rs]], axis=0), m_f32)


def _rwkv_scan_body(rhat_ref, y0_ref, a_ref, d_ref, bonus_ref, g_ref, s0_ref, lng_ref, lnb_ref, bd_ref, mbf_ref,
                    o_ref, s_ref, s_scr, y_scr, *, chunk, groups, steps):
    n = RW_HEAD_DIM
    c = pl.program_id(1)

    @pl.when(c == 0)
    def _():
        for h in range(RW_HEADS):
            s_scr[:, h * n:(h + 1) * n] = s0_ref[0, h]

    mask = mbf_ref[...]
    s = s_scr[...]
    for j in range(groups):
        rs = slice(j * chunk, (j + 1) * chunk)
        ns = slice(j * n, (j + 1) * n)
        s_s = _split_bf16(s)
        y_scr[rs, :] = y0_ref[rs, :] + _bdmm(_split_bf16(rhat_ref[rs, :]), s_s, mask, _dot_nt, extended=True)
        s = _bdmm(s_s, _split_bf16(a_ref[ns, :]), mask, extended=True) + d_ref[ns, :]
    s_scr[...] = s

    @pl.when(c == steps - 1)
    def _():
        for h in range(RW_HEADS):
            s_ref[0, h] = s[:, h * n:(h + 1) * n]

    bd = bd_ref[...]
    y = y_scr[...]
    mean = _dot_ones_rhs(y, bd) * (1.0 / n)
    yc = y - mean
    var = _dot_ones_rhs(yc * yc, bd) * (1.0 / n)
    yn = yc * lax.rsqrt(var + RW_GN_EPS) * lng_ref[...] + lnb_ref[...]
    o_ref[...] = ((yn + bonus_ref[...]) * g_ref[...]).astype(o_ref.dtype)


def _rwkv(rw_all, batch, t, t_valid, prev, s0, lp, o_prev):
    proj = rw_all.shape[1]
    n_all = o_prev.shape[0]
    row_off = 0
    n = RW_HEAD_DIM
    chunk = RW_CHUNK
    assert chunk == n and t % chunk == 0 and (t_valid == t or t == chunk)
    n_sq = chunk.bit_length() - 2
    nc = t // chunk
    gp = min(RW_PREP_CHUNKS, nc)
    gs = min(RW_SCAN_CHUNKS, nc)
    rows_p = gp * chunk
    rows_s = gs * chunk
    n_tok = batch * t
    row = lambda x: x.reshape(1, -1).astype(F32)
    r = lax.broadcasted_iota(jnp.int32, (rows_p, rows_p), 0)
    c = lax.broadcasted_iota(jnp.int32, (rows_p, rows_p), 1)
    tri = ((c <= r) & (c // chunk == r // chunk)).astype(BF16)
    hr = lax.broadcasted_iota(jnp.int32, (RW_WIDTH, RW_WIDTH), 0) // n
    hc = lax.broadcasted_iota(jnp.int32, (RW_WIDTH, RW_WIDTH), 1) // n
    bd = (hr == hc).astype(BF16)
    group_mask = (hr == hc)[:GROUP_WIDTH, :GROUP_WIDTH]
    const = lambda shape: pl.BlockSpec(shape, lambda b, c: (0,) * len(shape))

    blk_p = row_off // rows_p
    prep_args = [rw_all, rw_all, prev.astype(F32), row(lp['rw_mu']), row(lp['rw_w0']), lp['rw_w2'].astype(F32),
                 row(lp['rw_a0']), lp['rw_a2'].astype(F32), lp['rw_g2'].astype(BF16), row(lp['rw_k_k']),
                 row(lp['rw_k_a']), row(lp['rw_r_k']), tri, bd, group_mask.astype(BF16), group_mask.astype(F32)]
    tok_spec = lambda rows: pl.BlockSpec((rows, RW_WIDTH), lambda b, c: (b * (t // rows) + c, 0))
    mat_spec = lambda g: pl.BlockSpec((g * n, RW_WIDTH), lambda b, c: (b * (nc // g) + c, 0))
    tok_shape = jax.ShapeDtypeStruct((n_tok, RW_WIDTH), F32)
    mat_shape = jax.ShapeDtypeStruct((batch * nc * n, RW_WIDTH), F32)
    rhat, y0, a_mat, d_mat, bonus, g = pl.pallas_call(
        functools.partial(_rwkv_prep_body, chunk=chunk, groups=gp, n_sq=n_sq, t_valid=t_valid),
        grid=(batch, nc // gp),
        in_specs=[
            pl.BlockSpec((rows_p, proj), lambda b, c: (blk_p + b * (nc // gp) + c, 0)),
            pl.BlockSpec((8, proj), lambda b, c: (jnp.maximum((row_off + b * t + c * rows_p) // 8 - 1, 0), 0)),
            pl.BlockSpec((1, 1, proj), lambda b, c: (b, 0, 0)),
        ] + [const(x.shape) for x in prep_args[3:]],
        out_specs=[tok_spec(rows_p), tok_spec(rows_p), mat_spec(gp), mat_spec(gp), tok_spec(rows_p),
                   tok_spec(rows_p)],
        out_shape=[tok_shape, tok_shape, mat_shape, mat_shape, tok_shape, tok_shape],
        compiler_params=_params(("parallel", "parallel")),
        name="rwkv7_prep",
    )(*prep_args)

    blk_s = row_off // rows_s
    scan_args = [rhat, y0, a_mat, d_mat, bonus, g, s0.astype(F32), row(lp['rw_ln_g']), row(lp['rw_ln_b']), bd,
                 group_mask.astype(BF16), o_prev]
    state_spec = pl.BlockSpec((1, RW_HEADS, n, n), lambda b, c: (b, 0, 0, 0))
    n_in = len(scan_args) - 1

    def scan_body(*refs):
        _rwkv_scan_body(*refs[:n_in], *refs[n_in + 1:], chunk=chunk, groups=gs, steps=nc // gs)

    o, state = pl.pallas_call(
        scan_body,
        grid=(batch, nc // gs),
        in_specs=[tok_spec(rows_s), tok_spec(rows_s), mat_spec(gs), mat_spec(gs), tok_spec(rows_s),
                  tok_spec(rows_s), state_spec, const((1, RW_WIDTH)), const((1, RW_WIDTH)), const(bd.shape),
                  const(group_mask.shape), pl.BlockSpec(memory_space=pl.ANY)],
        out_specs=[pl.BlockSpec((rows_s, RW_WIDTH), lambda b, c: (blk_s + b * (nc // gs) + c, 0)), state_spec],
        out_shape=[jax.ShapeDtypeStruct((n_all, RW_WIDTH), BF16),
                   jax.ShapeDtypeStruct((batch, RW_HEADS, n, n), F32)],
        scratch_shapes=[pltpu.VMEM((n, RW_WIDTH), F32), pltpu.VMEM((rows_s, RW_WIDTH), F32)],
        input_output_aliases={n_in: 0},
        compiler_params=_params(("parallel", "arbitrary")),
        name="rwkv7_scan",
    )(*scan_args)
    return o, state


def _merge_body(oa_ref, ob_ref, gab_ref, xa_ref, xb_ref, wa_ref, wb_ref, wo_ref, g2_ref, wr_ref, br_ref, tri_ref,
                x2_ref, h2_ref, idx_ref, gate_ref, rank_ref, cnt_ref, carry_scr, *, tiles_a):
    i = pl.program_id(0)
    d = xa_ref.shape[-1]

    @pl.when(i == 0)
    def _():
        carry_scr[...] = jnp.zeros_like(carry_scr)

    gab = gab_ref[...]
    merged = (jax.nn.sigmoid(gab[:, :d]) * _dot(oa_ref[...], wa_ref[...])
              + jax.nn.sigmoid(gab[:, d:]) * _dot(ob_ref[...], wb_ref[...]))
    x = jnp.where(i < tiles_a, xa_ref[...], xb_ref[...])
    x2 = x + _dot(merged.astype(BF16), wo_ref[...])
    x2_ref[...] = x2
    h2 = _rmsnorm(x2, g2_ref[...])
    h2_ref[...] = h2
    logits = _dot3(h2, wr_ref[...]) + br_ref[...]

    tm, ne = logits.shape
    col = lax.broadcasted_iota(jnp.int32, (tm, ne), 1)
    c4 = lax.broadcasted_iota(jnp.int32, (tm, TOP_K), 1)
    work = logits
    tops, idxs = [], []
    for _ in range(TOP_K):
        m = jnp.max(work, axis=-1, keepdims=True)
        ix = jnp.min(jnp.where(work == m, col, ne), axis=-1, keepdims=True)
        tops.append(m)
        idxs.append(ix)
        work = jnp.where(col == ix, -jnp.inf, work)
    es = [jnp.exp(m - tops[0]) for m in tops]
    denom = es[0] + es[1] + es[2] + es[3]
    onehot = jnp.zeros((tm, ne), F32)
    for ix in idxs:
        onehot = onehot + (col == ix).astype(F32)
    before = carry_scr[...] + _dot(tri_ref[...], onehot.astype(BF16))
    idx_out = jnp.zeros((tm, TOP_K), jnp.int32)
    gate_out = jnp.zeros((tm, TOP_K), F32)
    rank_out = jnp.zeros((tm, TOP_K), jnp.int32)
    for kk in range(TOP_K):
        rk = jnp.sum(jnp.where(col == idxs[kk], before, 0.0), axis=-1, keepdims=True)
        idx_out = jnp.where(c4 == kk, idxs[kk], idx_out)
        gate_out = jnp.where(c4 == kk, es[kk] / denom, gate_out)
        rank_out = jnp.where(c4 == kk, rk.astype(jnp.int32), rank_out)
    idx_ref[...] = idx_out
    gate_ref[...] = gate_out
    rank_ref[...] = rank_out
    carry = carry_scr[...] + jnp.sum(onehot, axis=0, keepdims=True)
    carry_scr[...] = carry
    cnt_ref[...] = carry.astype(jnp.int32)


def _merge_route(oa, ob, gab, xa, xb, lp, tm):
    d = xa.shape[1]
    n = xa.shape[0] + xb.shape[0]
    assert xa.shape[0] % tm == 0 and xb.shape[0] % tm == 0
    tiles_a = xa.shape[0] // tm
    ne = N_EXPERTS
    r = lax.broadcasted_iota(jnp.int32, (tm, tm), 0)
    c = lax.broadcasted_iota(jnp.int32, (tm, tm), 1)
    tri = (c < r).astype(BF16)
    const = lambda shape: pl.BlockSpec(shape, lambda i: (0,) * len(shape))
    rowblk = lambda wd: pl.BlockSpec((tm, wd), lambda i: (i, 0))
    wa = lp['w_branch_a'].astype(BF16)
    wb = lp['w_branch_b'].astype(BF16)
    wo = lp['w_out'].astype(BF16)
    return pl.pallas_call(
        functools.partial(_merge_body, tiles_a=tiles_a),
        grid=(n // tm,),
        in_specs=[rowblk(oa.shape[1]), rowblk(ob.shape[1]), rowblk(2 * d),
                  pl.BlockSpec((tm, d), lambda i: (jnp.minimum(i, tiles_a - 1), 0)),
                  pl.BlockSpec((tm, d), lambda i: (jnp.maximum(i - tiles_a, 0), 0)),
                  const(wa.shape), const(wb.shape), const(wo.shape), const((1, d)), const((d, ne)),
                  const((1, ne)), const((tm, tm))],
        out_specs=[rowblk(d), rowblk(d), rowblk(TOP_K), rowblk(TOP_K), rowblk(TOP_K), const((1, ne))],
        out_shape=[jax.ShapeDtypeStruct((n, d), F32), jax.ShapeDtypeStruct((n, d), F32),
                   jax.ShapeDtypeStruct((n, TOP_K), jnp.int32), jax.ShapeDtypeStruct((n, TOP_K), F32),
                   jax.ShapeDtypeStruct((n, TOP_K), jnp.int32), jax.ShapeDtypeStruct((1, ne), jnp.int32)],
        scratch_shapes=[pltpu.VMEM((1, ne), F32)],
        compiler_params=_params(("arbitrary",)),
        name="merge_route",
    )(oa, ob, gab, xa, xb, wa, wb, wo, lp['norm2_g'].reshape(1, d).astype(F32), lp['w_router'].astype(F32),
      lp['b_router'].reshape(1, ne).astype(F32), tri)


def _dispatch_body(dest_ref, end_ref, padded_ref, h_ref, xs_ref, zeros_ref, sem, zsem, *, tile, bm):
    @pl.when(pl.program_id(0) == 0)
    def _():
        zeros_ref[...] = jnp.zeros_like(zeros_ref)

        def clear(e):
            start = pl.multiple_of(end_ref[e] - bm, bm)
            return pltpu.make_async_copy(zeros_ref, xs_ref.at[pl.ds(start, bm)], zsem)

        for e in range(N_EXPERTS):
            @pl.when(padded_ref[e] > 0)
            def _():
                clear(e).start()

        for e in range(N_EXPERTS):
            @pl.when(padded_ref[e] > 0)
            def _():
                clear(e).wait()

    def copy(t, kk):
        return pltpu.make_async_copy(h_ref.at[pl.ds(t, 1)],
                                     xs_ref.at[pl.ds(dest_ref[0, 0, t * TOP_K + kk], 1)], sem)

    def issue(t, carry):
        for kk in range(TOP_K):
            copy(t, kk).start(priority=kk % 2)
        return carry

    def drain(t, carry):
        for kk in range(TOP_K):
            copy(t, kk).wait()
        return carry

    lax.fori_loop(0, tile, issue, 0, unroll=DMA_UNROLL)
    lax.fori_loop(0, tile, drain, 0, unroll=DMA_UNROLL)


def _dispatch(h2, dest, pad_end, padded, n_rows, tile, bm):
    n, d = h2.shape
    nt = n // tile
    smem = pl.BlockSpec(memory_space=pltpu.SMEM)
    return pl.pallas_call(
        functools.partial(_dispatch_body, tile=tile, bm=bm),
        grid=(nt,),
        in_specs=[pl.BlockSpec((1, 1, tile * TOP_K), lambda i: (i, 0, 0), memory_space=pltpu.SMEM), smem, smem,
                  pl.BlockSpec((tile, d), lambda i: (i, 0))],
        out_specs=pl.BlockSpec(memory_space=pl.ANY),
        out_shape=jax.ShapeDtypeStruct((n_rows, d), F32),
        scratch_shapes=[pltpu.VMEM((bm, d), F32), pltpu.SemaphoreType.DMA, pltpu.SemaphoreType.DMA],
        compiler_params=_params(("arbitrary",)),
        name="moe_dispatch",
    )(dest.reshape(nt, 1, tile * TOP_K), pad_end.astype(jnp.int32), padded.astype(jnp.int32), h2)


def _expert_body(be_ref, used_ref, xs_ref, wgu_ref, bgu_ref, wd_ref, bd_ref, ys_ref, wgu16, wd16):
    i = pl.program_id(0)
    de = wd_ref.shape[1]

    @pl.when(jnp.logical_or(i == 0, be_ref[i] != be_ref[jnp.maximum(i - 1, 0)]))
    def _():
        wgu16[...] = wgu_ref[0].astype(BF16)
        wd16[...] = wd_ref[0].astype(BF16)

    @pl.when(i < used_ref[0])
    def _():
        gu = _dot(xs_ref[...].astype(BF16), wgu16[...]) + bgu_ref[0]
        g = jnp.minimum(gu[:, :de], SWIGLU_LIMIT)
        u = jnp.clip(gu[:, de:], -SWIGLU_LIMIT, SWIGLU_LIMIT)
        act = (u + 1.0) * (g * jax.nn.sigmoid(g * SWIGLU_ALPHA))
        ys_ref[...] = _dot(act.astype(BF16), wd16[...]) + bd_ref[0]

    @pl.when(i >= used_ref[0])
    def _():
        ys_ref[...] = jnp.zeros_like(ys_ref)


def _experts(xs, block_expert, n_used, wgu, bgu, wd, bd, bm):
    n_rows, d = xs.shape
    ne, _, de2 = wgu.shape
    nb = n_rows // bm
    grid_spec = pltpu.PrefetchScalarGridSpec(
        num_scalar_prefetch=2,
        grid=(nb,),
        in_specs=[
            pl.BlockSpec((bm, d), lambda i, be, nu: (i, 0)),
            pl.BlockSpec((1, d, de2), lambda i, be, nu: (be[i], 0, 0)),
            pl.BlockSpec((1, 1, de2), lambda i, be, nu: (be[i], 0, 0)),
            pl.BlockSpec((1, de2 // 2, d), lambda i, be, nu: (be[i], 0, 0)),
            pl.BlockSpec((1, 1, d), lambda i, be, nu: (be[i], 0, 0)),
        ],
        out_specs=pl.BlockSpec((bm, d), lambda i, be, nu: (i, 0)),
        scratch_shapes=[pltpu.VMEM((d, de2), BF16), pltpu.VMEM((de2 // 2, d), BF16)],
    )
    return pl.pallas_call(
        _expert_body,
        grid_spec=grid_spec,
        out_shape=jax.ShapeDtypeStruct((n_rows, d), F32),
        compiler_params=_params(("arbitrary",)),
        name="moe_experts",
    )(block_expert, n_used, xs, wgu, bgu.reshape(ne, 1, de2), wd, bd.reshape(ne, 1, d))


def _combine_body(dest_ref, next_ref, ys_ref, gate_ref, x2_ref, g_ref, ya_ref, yb_ref, buf, sems, *,
                  tile, tiles_a, steps):
    i = pl.program_id(0)
    slot = i % 2

    def copy(dref, s, t, kk):
        return pltpu.make_async_copy(ys_ref.at[pl.ds(dref[0, 0, t * TOP_K + kk], 1)],
                                     buf.at[s, kk, pl.ds(t, 1)], sems.at[s])

    def issue(dref, s):
        def body(t, carry):
            for kk in range(TOP_K):
                copy(dref, s, t, kk).start(priority=kk % 2)
            return carry
        lax.fori_loop(0, tile, body, 0, unroll=DMA_UNROLL)

    @pl.when(i == 0)
    def _():
        issue(dest_ref, 0)

    @pl.when(i + 1 < steps)
    def _():
        issue(next_ref, 1 - slot)

    def drain(t, carry):
        for kk in range(TOP_K):
            copy(dest_ref, slot, t, kk).wait()
        return carry

    lax.fori_loop(0, tile, drain, 0, unroll=DMA_UNROLL)
    gate = gate_ref[...]
    moe = gate[:, 0:1] * buf[slot, 0]
    for kk in range(1, TOP_K):
        moe = moe + gate[:, kk:kk + 1] * buf[slot, kk]
    y = _rmsnorm(x2_ref[...] + moe, g_ref[...])

    @pl.when(i < tiles_a)
    def _():
        ya_ref[...] = y

    @pl.when(i >= tiles_a)
    def _():
        yb_ref[...] = y


def _combine(ys, dest, gate, x2, final_g, tile, n_a):
    n, d = x2.shape
    nt = n // tile
    tiles_a = n_a // tile
    dest3 = dest.reshape(nt, 1, tile * TOP_K)
    return pl.pallas_call(
        functools.partial(_combine_body, tile=tile, tiles_a=tiles_a, steps=nt),
        grid=(nt,),
        in_specs=[pl.BlockSpec((1, 1, tile * TOP_K), lambda i: (i, 0, 0), memory_space=pltpu.SMEM),
                  pl.BlockSpec((1, 1, tile * TOP_K), lambda i: (jnp.minimum(i + 1, nt - 1), 0, 0),
                               memory_space=pltpu.SMEM),
                  pl.BlockSpec(memory_space=pl.ANY),
                  pl.BlockSpec((tile, TOP_K), lambda i: (i, 0)),
                  pl.BlockSpec((tile, d), lambda i: (i, 0)),
                  pl.BlockSpec((1, d), lambda i: (0, 0))],
        out_specs=[pl.BlockSpec((tile, d), lambda i: (jnp.minimum(i, tiles_a - 1), 0)),
                   pl.BlockSpec((tile, d), lambda i: (jnp.maximum(i - tiles_a, 0), 0))],
        out_shape=[jax.ShapeDtypeStruct((n_a, d), F32), jax.ShapeDtypeStruct((n - n_a, d), F32)],
        scratch_shapes=[pltpu.VMEM((2, TOP_K, tile, d), F32), pltpu.SemaphoreType.DMA((2,))],
        compiler_params=_params(("arbitrary",)),
        name="moe_combine",
    )(dest3, dest3, ys, gate, x2, final_g.reshape(1, d).astype(F32))


def _moe(h2, x2, idx, gate, rank, counts, lp, final_g, n_a):
    n, d = x2.shape
    bm = EXPERT_ROWS
    counts = counts.reshape(N_EXPERTS)
    padded = (counts + bm - 1) // bm * bm
    pad_end = jnp.cumsum(padded)
    dest = ((pad_end - padded)[idx] + rank).reshape(n * TOP_K).astype(jnp.int32)
    n_blocks = -(-(n * TOP_K) // bm) + N_EXPERTS
    block_start = jnp.arange(n_blocks, dtype=jnp.int32) * bm
    block_expert = jnp.minimum(jnp.sum(pad_end[None, :] <= block_start[:, None], axis=1),
                               N_EXPERTS - 1).astype(jnp.int32)
    n_used = (pad_end[-1:] // bm).astype(jnp.int32)
    xs = _dispatch(h2, dest, pad_end, padded, n_blocks * bm, DISPATCH_TILE, bm)
    ys = _experts(xs, block_expert, n_used, lp['w_gate_up'].astype(F32), lp['b_gate_up'].astype(F32),
                  lp['w_down'].astype(F32), lp['b_down'].astype(F32), bm)
    return _combine(ys, dest, gate, x2, final_g, COMBINE_TILE, n_a)


def _to_heads(t, b, s):
    return t.reshape(b, s, SB_HEADS, SB_HEAD_DIM).transpose(0, 2, 1, 3)


def _from_heads(t):
    b, h, s, dh = t.shape
    return t.transpose(0, 2, 1, 3).reshape(b, s, h * dh)


def _layer(x_p, x_s, cache_k, cache_v, state_rwkv, state_shift, lp, final_g):
    bp, tp, d = x_p.shape
    bs, ts, _ = x_s.shape
    sbw = SB_HEADS * SB_HEAD_DIM
    rw_proj = lp['rw_mu'].shape[0]
    n_p = bp * tp
    n = n_p + bs * ts
    x_s = x_s.reshape(bs * ts, d)
    scale = LOG2E * SB_HEAD_DIM ** -0.5
    qkv16, kp, vp, kv_new, rw, gab = _inproj(x_p, x_s, lp['norm1_g'].astype(F32), lp['w_in'].astype(BF16),
                                             rw_proj, scale, ROW_TILE)
    pairs = SB_HEADS // 2
    oa = _stick_breaking(qkv16, 0, 0, qkv16, pairs, 2 * pairs, bp, tp, 0, SB_BLOCK, SB_SUB,
                         jnp.zeros((n, sbw), BF16))
    past = cache_k.shape[2]
    new16 = qkv16[n_p:].reshape(bs, ts, 3 * sbw)
    kv_s = jnp.concatenate([
        jnp.concatenate([_from_heads(cache_k).astype(BF16), new16[:, :, sbw:2 * sbw]], axis=1),
        jnp.concatenate([_from_heads(cache_v).astype(BF16), new16[:, :, 2 * sbw:]], axis=1)], axis=2)
    oa = _stick_breaking(qkv16, n_p, 0, kv_s.reshape(bs * (past + ts), 2 * sbw), 0, pairs, bs, ts, past, ts,
                         min(SB_SUB, past), oa)
    ks, vs = kv_new[:, :sbw], kv_new[:, sbw:]

    s0_p = jnp.zeros((bp, RW_HEADS, RW_HEAD_DIM, RW_HEAD_DIM), F32)
    prev_p = jnp.zeros((bp, 1, rw_proj), F32)
    ob, st_p = _rwkv(rw, bp, tp, tp, prev_p, s0_p, lp, jnp.zeros((n, RW_WIDTH), BF16))
    rw_s = rw[n_p:].reshape(bs, ts, rw_proj)
    ts_pad = -(-ts // RW_CHUNK) * RW_CHUNK
    rw_s_pad = jnp.pad(rw_s, ((0, 0), (0, ts_pad - ts), (0, 0))).reshape(bs * ts_pad, rw_proj)
    ob_s, st_s = _rwkv(rw_s_pad, bs, ts_pad, ts, state_shift, state_rwkv, lp,
                       jnp.zeros((bs * ts_pad, RW_WIDTH), BF16))
    ob = lax.dynamic_update_slice(
        ob, ob_s.reshape(bs, ts_pad, RW_WIDTH)[:, :ts].reshape(bs * ts, RW_WIDTH), (n_p, 0))
    sh_p = jnp.stack([rw[(b + 1) * tp - 1:(b + 1) * tp] for b in range(bp)])
    sh_s = rw_s[:, ts - 1:]

    x2, h2, idx, gate, rank, counts = _merge_route(oa, ob, gab, x_p.reshape(n_p, d), x_s, lp, MERGE_TILE)
    y = _moe(h2, x2, idx, gate, rank, counts, lp, final_g, n_p)
    return (y, kp, vp, st_p, sh_p, _to_heads(ks, bs, ts), _to_heads(vs, bs, ts), st_s, sh_s)


def kernel(x_prompt, x_sample, cache_sb_k, cache_sb_v, state_rwkv, state_shift, norm1_g, w_in, rw_mu, rw_w0, rw_w2, rw_a0, rw_a2, rw_g2, rw_k_k, rw_k_a, rw_r_k, rw_ln_g, rw_ln_b, w_branch_a, w_branch_b, w_out, norm2_g, w_router, b_router, w_gate_up, b_gate_up, w_down, b_down, final_norm_g):
    depth = w_in.shape[0]
    assert depth == 1, "the final RMSNorm is fused into the last (only) layer"
    bp, tp, d = x_prompt.shape
    bs, ts, _ = x_sample.shape
    lp = dict(norm1_g=norm1_g[0], w_in=w_in[0], rw_mu=rw_mu[0], rw_w0=rw_w0[0], rw_w2=rw_w2[0], rw_a0=rw_a0[0],
              rw_a2=rw_a2[0], rw_g2=rw_g2[0], rw_k_k=rw_k_k[0], rw_k_a=rw_k_a[0], rw_r_k=rw_r_k[0].reshape(-1),
              rw_ln_g=rw_ln_g[0], rw_ln_b=rw_ln_b[0], w_branch_a=w_branch_a[0], w_branch_b=w_branch_b[0],
              w_out=w_out[0], norm2_g=norm2_g[0], w_router=w_router[0], b_router=b_router[0],
              w_gate_up=w_gate_up[0], b_gate_up=b_gate_up[0], w_down=w_down[0], b_down=b_down[0])
    y, kp, vp, st_p, sh_p, ks, vs, st_s, sh_s = _layer(
        x_prompt, x_sample, cache_sb_k[0], cache_sb_v[0], state_rwkv[0], state_shift[0], lp, final_norm_g)
    return (y[0].reshape(bp, tp, d), y[1].reshape(bs, ts, d),
            kp[None], vp[None], st_p[None], sh_p[None], ks[None], vs[None], st_s[None], sh_s[None])
```

```python
import functools

import jax
import jax.numpy as jnp
from jax import lax
from jax.experimental import pallas as pl
from jax.experimental.pallas import tpu as pltpu

F32 = jnp.float32
BF16 = jnp.bfloat16

SB_HEADS = 8
SB_HEAD_DIM = 64
RW_HEADS = 8
RW_HEAD_DIM = 64
RW_WIDTH = RW_HEADS * RW_HEAD_DIM
RW_DECAY_LORA = 64
RW_AAA_LORA = 64
RW_GATE_LORA = 128
RW_GN_EPS = 64e-5
N_EXPERTS = 32
TOP_K = 4
SWIGLU_LIMIT = 7.0
SWIGLU_ALPHA = 1.702
RMS_EPS = 1e-6

VMEM_LIMIT_BYTES = 56 * 1024 * 1024
ROW_TILE = 256
MERGE_TILE = 512
SB_BLOCK = 512
SB_SUB = 256
LOG2E = 1.4426950408889634
SB_DEAD_LOG2 = -200.0
RW_CHUNK = 64
RW_PREP_CHUNKS = 4
RW_SCAN_CHUNKS = 4
EXPERT_ROWS = 512
DISPATCH_TILE = 512
COMBINE_TILE = 128
SUBLANES = 8


def _params(semantics):
    return pltpu.CompilerParams(dimension_semantics=semantics, vmem_limit_bytes=VMEM_LIMIT_BYTES)


def _dot(a, b):
    return jnp.dot(a, b, preferred_element_type=F32)


def _dot_nt(a, b):
    return lax.dot_general(a, b, (((1,), (1,)), ((), ())), preferred_element_type=F32)


def _softplus_parts(z):
    l = jnp.log(1.0 + jnp.exp(-jnp.abs(z)))
    sp = jnp.maximum(z, 0.0) + l
    return sp, sp - z


def _rmsnorm(x, g):
    return x * lax.rsqrt(jnp.mean(x * x, axis=-1, keepdims=True) + RMS_EPS) * g


def _inproj_body(xa_ref, xb_ref, g_ref, w_ref, qkv16_ref, ka_ref, va_ref, kvb_ref, rw_ref, gab_ref, *,
                 q_scale, tiles_a):
    i = pl.program_id(0)
    x = jnp.where(i < tiles_a, xa_ref[...], xb_ref[...])
    h = _rmsnorm(x, g_ref[...]).astype(BF16)
    sbw = kvb_ref.shape[-1] // 2
    dh = ka_ref.shape[-1]
    q = _dot(h, w_ref[:, 0:sbw])
    qkv16_ref[:, 0:sbw] = (q * q_scale).astype(BF16)
    k = _dot(h, w_ref[:, sbw:2 * sbw])
    v = _dot(h, w_ref[:, 2 * sbw:3 * sbw])
    qkv16_ref[:, sbw:2 * sbw] = k.astype(BF16)
    qkv16_ref[:, 2 * sbw:3 * sbw] = v.astype(BF16)
    off = 3 * sbw
    for ref in (rw_ref, gab_ref):
        width = ref.shape[-1]
        for c0 in range(0, width, sbw):
            cw = min(sbw, width - c0)
            ref[:, c0:c0 + cw] = _dot(h, w_ref[:, off + c0:off + c0 + cw])
        off += width

    @pl.when(i < tiles_a)
    def _():
        for hd in range(sbw // dh):
            ka_ref[0, hd] = k[:, hd * dh:(hd + 1) * dh]
            va_ref[0, hd] = v[:, hd * dh:(hd + 1) * dh]

    @pl.when(i >= tiles_a)
    def _():
        kvb_ref[:, 0:sbw] = k
        kvb_ref[:, sbw:2 * sbw] = v


def _inproj(xa, xb, g, w_bf16, rw_proj, q_scale, tm):
    ba, ta, d = xa.shape
    nb = xb.shape[0]
    n = ba * ta + nb
    sbw = SB_HEADS * SB_HEAD_DIM
    total = w_bf16.shape[1]
    assert total == 3 * sbw + rw_proj + 2 * d and ta % tm == 0 and nb % tm == 0
    tiles_a = ba * ta // tm
    per_seq = ta // tm
    last_a = tiles_a - 1
    row = lambda wd: pl.BlockSpec((tm, wd), lambda i: (i, 0))
    head_major = pl.BlockSpec(
        (1, SB_HEADS, tm, SB_HEAD_DIM),
        lambda i: (jnp.minimum(i, last_a) // per_seq, 0, jnp.minimum(i, last_a) % per_seq, 0))
    hm_shape = jax.ShapeDtypeStruct((ba, SB_HEADS, ta, SB_HEAD_DIM), F32)
    return pl.pallas_call(
        functools.partial(_inproj_body, q_scale=q_scale, tiles_a=tiles_a),
        grid=(n // tm,),
        in_specs=[
            pl.BlockSpec((tm, d), lambda i: (jnp.minimum(i, last_a), 0)),
            pl.BlockSpec((tm, d), lambda i: (jnp.maximum(i - tiles_a, 0), 0)),
            pl.BlockSpec((1, d), lambda i: (0, 0)),
            pl.BlockSpec((d, total), lambda i: (0, 0), pipeline_mode=pl.Buffered(1)),
        ],
        out_specs=[row(3 * sbw), head_major, head_major,
                   pl.BlockSpec((tm, 2 * sbw), lambda i: (jnp.maximum(i - tiles_a, 0), 0)),
                   row(rw_proj), row(2 * d)],
        out_shape=[jax.ShapeDtypeStruct((n, 3 * sbw), BF16), hm_shape, hm_shape,
                   jax.ShapeDtypeStruct((nb, 2 * sbw), F32),
                   jax.ShapeDtypeStruct((n, rw_proj), F32), jax.ShapeDtypeStruct((n, 2 * d), F32)],
        compiler_params=_params(("arbitrary",)),
        name="inproj",
    )(xa.reshape(ba * ta, d), xb, g.reshape(1, d), w_bf16)


def _sb_tile(q, kb, vb, ntri, carry, mask, sb):
    tk = kb.shape[0]
    z = _dot_nt(q, kb)
    neg_abs = lax.bitcast_convert_type(lax.bitcast_convert_type(z, jnp.uint32) | jnp.uint32(0x80000000), F32)
    sp = jnp.maximum(z, 0.0) + jnp.log(1.0 + jnp.exp2(neg_abs)) * LOG2E
    if mask is not None:
        sp = jnp.where(mask, sp, 0.0)
    hi = sp.astype(BF16)
    lo = (sp - hi.astype(F32)).astype(BF16)
    log_beta = z - sp
    wts = []
    for j in range(tk // sb - 1, -1, -1):
        cols = slice(j * sb, (j + 1) * sb)
        later = _dot(hi[:, cols], ntri) + _dot(lo[:, cols], ntri) + carry
        w = jnp.exp2(log_beta[:, cols] + later)
        if mask is not None:
            w = jnp.where(mask[:, cols], w, 0.0)
        wts.append(w.astype(BF16))
        carry = carry - jnp.sum(sp[:, cols], axis=-1, keepdims=True)
    wts = wts[0] if len(wts) == 1 else jnp.concatenate(wts[::-1], axis=1)
    return _dot(wts, vb), carry


def _sb_body(q_ref, k_ref, v_ref, triq_ref, trik_ref, o_in_ref, o_ref, *, tq, tk, past):
    del o_in_ref
    qi = pl.program_id(2)
    q = q_ref[...]
    first = lax.broadcasted_iota(jnp.int32, q.shape, 1) < SB_HEAD_DIM
    zero = jnp.zeros_like(q)
    qs = (jnp.where(first, q, zero), jnp.where(first, zero, q))
    d0 = pl.multiple_of(past + qi * tq, tq)
    row = lax.broadcasted_iota(jnp.int32, (tq, tq), 0)
    col = lax.broadcasted_iota(jnp.int32, (tq, tq), 1)
    kd = k_ref[pl.ds(d0, tq), :]
    vd = v_ref[pl.ds(d0, tq), :]
    sbq = triq_ref.shape[0]
    state = []
    for qh in qs:
        state.extend(_sb_tile(qh, kd, vd, triq_ref[...], jnp.zeros((tq, 1), F32), col < row, sbq))
    nb = (past + qi * tq) // tk
    sbk = trik_ref.shape[0]

    def live(state):
        return jnp.max(jnp.maximum(state[1], state[3])) > SB_DEAD_LOG2

    def cond(loop):
        i, alive, _ = loop
        return jnp.logical_and(i < nb, alive)

    def step(loop):
        i, _, state = loop
        s0 = pl.multiple_of((nb - 1 - i) * tk, tk)
        kb = k_ref[pl.ds(s0, tk), :]
        vb = v_ref[pl.ds(s0, tk), :]
        new = []
        for h, qh in enumerate(qs):
            out, carry = _sb_tile(qh, kb, vb, trik_ref[...], state[2 * h + 1], None, sbk)
            new.extend((state[2 * h] + out, carry))
        return i + 1, live(new), tuple(new)

    _, _, state = lax.while_loop(cond, step, (jnp.int32(0), live(state), tuple(state)))
    o_ref[...] = jnp.where(first, state[0], state[2]).astype(o_ref.dtype)


def _later_matrix(n):
    r = lax.broadcasted_iota(jnp.int32, (n, n), 0)
    c = lax.broadcasted_iota(jnp.int32, (n, n), 1)
    return jnp.where(r > c, -1.0, 0.0).astype(BF16)


def _stick_breaking(q, q_row0, q_lane0, kv, k_lane0, v_lane0, batch, t, past, tq, tk, o_prev):
    s = past + t
    pair = 2 * SB_HEAD_DIM
    sbq = min(tq, SB_SUB)
    sbk = min(tk, SB_SUB)
    nq = t // tq
    qblk = q_row0 // tq
    q_spec = lambda lane0: pl.BlockSpec((tq, pair), lambda b, h, i: (qblk + b * nq + i, lane0 + h))
    kv_spec = lambda lane0: pl.BlockSpec((s, pair), lambda b, h, i: (b, lane0 + h))
    return pl.pallas_call(
        functools.partial(_sb_body, tq=tq, tk=tk, past=past),
        grid=(batch, SB_HEADS // 2, nq),
        in_specs=[
            q_spec(q_lane0), kv_spec(k_lane0), kv_spec(v_lane0),
            pl.BlockSpec((sbq, sbq), lambda b, h, i: (0, 0)),
            pl.BlockSpec((sbk, sbk), lambda b, h, i: (0, 0)),
            pl.BlockSpec(memory_space=pl.ANY),
        ],
        out_specs=q_spec(0),
        out_shape=jax.ShapeDtypeStruct(o_prev.shape, BF16),
        input_output_aliases={5: 0},
        compiler_params=_params(("parallel", "parallel", "arbitrary")),
        name="stick_breaking",
    )(q, kv, kv, _later_matrix(sbq), _later_matrix(sbk), o_prev)


def _split_bf16(x):
    hi = x.astype(BF16)
    return hi, (x - hi.astype(F32)).astype(BF16)


def _dot3(a, b, dot=_dot):
    ah, al = _split_bf16(a)
    bh, bl = _split_bf16(b)
    return dot(ah, bh) + dot(ah, bl) + dot(al, bh)


def _dot_ones_rhs(x, ones_bf16):
    hi, lo = _split_bf16(x)
    return _dot(hi, ones_bf16) + _dot(lo, ones_bf16)


def _dot_ones_lhs(ones_bf16, x):
    hi, lo = _split_bf16(x)
    return _dot(ones_bf16, hi) + _dot(ones_bf16, lo)


HEADS_PER_MATMUL = 4
GROUP_WIDTH = HEADS_PER_MATMUL * RW_HEAD_DIM


def _block_diag(y, mask):
    return jnp.concatenate([y] * HEADS_PER_MATMUL, axis=0) * mask


def _bdmm(xs, ys, mask, dot=_dot, extended=False):
    xh, xl = xs
    yh, yl = ys
    m = xh.shape[0]
    x2 = jnp.concatenate([xh, xl], axis=0) if extended else xh
    outs = []
    for g in range(RW_WIDTH // GROUP_WIDTH):
        ls = slice(g * GROUP_WIDTH, (g + 1) * GROUP_WIDTH)
        out = dot(x2[:, ls], _block_diag(yh[:, ls], mask))
        if extended:
            out = out[:m] + out[m:] + dot(xh[:, ls], _block_diag(yl[:, ls], mask))
        outs.append(out)
    return jnp.concatenate(outs, axis=1)


def _bd_tn(x, y, mask):
    n = RW_HEAD_DIM
    outs = []
    for g in range(RW_WIDTH // GROUP_WIDTH):
        ls = slice(g * GROUP_WIDTH, (g + 1) * GROUP_WIDTH)
        z = _dot(x[:, ls].T.astype(BF16), y[:, ls].astype(BF16)) * mask
        outs.append(z[0:n] + z[n:2 * n] + z[2 * n:3 * n] + z[3 * n:4 * n])
    return jnp.concatenate(outs, axis=1)


def _rwkv_prep_body(p_ref, pb_ref, prev_ref, mu_ref, w0_ref, w2_ref, a0_ref, a2_ref, g2_ref, kk_ref, ka_ref,
                    rk_ref, tri_ref, bd_ref, mbf_ref, mf32_ref,
                    rhat_ref, y0_ref, a_ref, d_ref, bonus_ref, g_ref, *, chunk, groups, n_sq, t_valid):
    n = RW_HEAD_DIM
    w_ = RW_WIDTH
    p = p_ref[...]
    rows = lax.broadcasted_iota(jnp.int32, p.shape, 0)
    prev_row = jnp.where(pl.program_id(1) == 0, prev_ref[0], pb_ref[7:8, :])
    p_shift = jnp.where(rows == 0, prev_row, pltpu.roll(p, 1, 0))

    xs = p + (p_shift - p) * mu_ref[...]
    r = xs[:, 0:w_]
    k = xs[:, w_:2 * w_]
    v = xs[:, 2 * w_:3 * w_]
    o1 = 3 * w_
    dw = xs[:, o1:o1 + RW_DECAY_LORA]
    da = xs[:, o1 + RW_DECAY_LORA:o1 + RW_DECAY_LORA + RW_AAA_LORA]
    dg = xs[:, o1 + RW_DECAY_LORA + RW_AAA_LORA:]

    sp_neg = _softplus_parts(w0_ref[...] + _dot3(jnp.tanh(dw), w2_ref[...]))[1]
    log_decay = -jnp.exp(-sp_neg - 0.5)
    a = jax.nn.sigmoid(a0_ref[...] + _dot3(da, a2_ref[...]))
    g_ref[...] = _dot(jax.nn.sigmoid(dg).astype(BF16), g2_ref[...])

    bd = bd_ref[...]
    kk = k * kk_ref[...]
    kk = kk / jnp.maximum(jnp.sqrt(_dot_ones_rhs(kk * kk, bd)), 1e-12)
    k2 = k * (1.0 + (a - 1.0) * ka_ref[...])
    a_vec = -kk
    b_vec = kk * a
    bonus_ref[...] = _dot((r * k2 * rk_ref[...]).astype(BF16), bd) * v
    if t_valid < chunk:
        live = lax.broadcasted_iota(jnp.int32, r.shape, 0) < t_valid
        keep = lambda x: jnp.where(live, x, 0.0)
        log_decay, r, k2, v, a_vec, b_vec = (keep(x) for x in (log_decay, r, k2, v, a_vec, b_vec))

    cw = _dot_ones_lhs(tri_ref[...], log_decay)
    cw_last = jnp.concatenate(
        [jnp.broadcast_to(cw[(j + 1) * chunk - 1:(j + 1) * chunk, :], (chunk, w_)) for j in range(groups)], axis=0)
    r_t = r * jnp.exp(cw)
    a_t = a_vec * jnp.exp(cw - log_decay)
    e_neg = jnp.exp(-cw)
    b_t = b_vec * e_neg
    k_t = k2 * e_neg
    e_rel = jnp.exp(cw_last - cw)
    b_h = b_vec * e_rel
    k_h = k2 * e_rel
    w_c = jnp.exp(cw_last)

    rr = lax.broadcasted_iota(jnp.int32, (chunk, w_), 0)
    ss = lax.broadcasted_iota(jnp.int32, (chunk, w_), 1) % n
    strict = ss < rr
    incl = ss <= rr
    eye = (ss == rr).astype(F32)
    m_bf16 = mbf_ref[...]
    m_f32 = mf32_ref[...]

    for j in range(groups):
        rs = slice(j * chunk, (j + 1) * chunk)
        v_j = v[rs]
        v_s = _split_bf16(v_j)
        ar_s = _split_bf16(jnp.concatenate([a_t[rs], r_t[rs]], axis=0))
        gram_b = _bdmm(ar_s, _split_bf16(b_t[rs]), m_bf16, _dot_nt)
        gram_k = _bdmm(ar_s, _split_bf16(k_t[rs]), m_bf16, _dot_nt)
        l_ab = jnp.where(strict, gram_b[:chunk], 0.0)
        l_ak = jnp.where(strict, gram_k[:chunk], 0.0)
        m_rb_s = _split_bf16(jnp.where(incl, gram_b[chunk:], 0.0))
        m_rk_s = _split_bf16(jnp.where(incl, gram_k[chunk:], 0.0))
        l_s = _split_bf16(l_ab)
        t_inv = eye + l_ab
        l_pow = _bdmm(l_s, l_s, m_bf16)
        for i in range(n_sq):
            lp_s = _split_bf16(l_pow)
            t_inv = t_inv + _bdmm(lp_s, _split_bf16(t_inv), m_bf16)
            if i + 1 < n_sq:
                l_pow = _bdmm(lp_s, lp_s, m_bf16)
        t_s = _split_bf16(t_inv)
        p_mat = _bdmm(t_s, _split_bf16(a_t[rs]), m_bf16)
        q_mat = _bdmm(t_s, _split_bf16(_bdmm(_split_bf16(l_ak), v_s, m_bf16)), m_bf16)
        rhat_ref[rs, :] = r_t[rs] + _bdmm(m_rb_s, _split_bf16(p_mat), m_bf16)
        y0_ref[rs, :] = _bdmm(m_rb_s, _split_bf16(q_mat), m_bf16) + _bdmm(m_rk_s, v_s, m_bf16)
        ns = slice(j * n, (j + 1) * n)
        a_ref[ns, :] = _bd_tn(p_mat, b_h[rs], m_f32) + eye * w_c[j * chunk:j * chunk + 1, :]
        d_ref[ns, :] = _bd_tn(jnp.concatenate([q_mat, v_j], axis=0),
                              jnp.concatenate([b_h[rs], k_h[rs]], axis=0), m_f32)


def _rwkv_scan_body(rhat_ref, y0_ref, a_ref, d_ref, bonus_ref, g_ref, s0_ref, lng_ref, lnb_ref, bd_ref, mbf_ref,
                    o_ref, s_ref, s_scr, y_scr, *, chunk, groups, steps):
    n = RW_HEAD_DIM
    c = pl.program_id(1)

    @pl.when(c == 0)
    def _():
        for h in range(RW_HEADS):
            s_scr[:, h * n:(h + 1) * n] = s0_ref[0, h]

    mask = mbf_ref[...]
    s = s_scr[...]
    for j in range(groups):
        rs = slice(j * chunk, (j + 1) * chunk)
        ns = slice(j * n, (j + 1) * n)
        s_s = _split_bf16(s)
        y_scr[rs, :] = y0_ref[rs, :] + _bdmm(_split_bf16(rhat_ref[rs, :]), s_s, mask, _dot_nt, extended=True)
        s = _bdmm(s_s, _split_bf16(a_ref[ns, :]), mask, extended=True) + d_ref[ns, :]
    s_scr[...] = s

    @pl.when(c == steps - 1)
    def _():
        for h in range(RW_HEADS):
            s_ref[0, h] = s[:, h * n:(h + 1) * n]

    bd = bd_ref[...]
    y = y_scr[...]
    mean = _dot_ones_rhs(y, bd) * (1.0 / n)
    yc = y - mean
    var = _dot_ones_rhs(yc * yc, bd) * (1.0 / n)
    yn = yc * lax.rsqrt(var + RW_GN_EPS) * lng_ref[...] + lnb_ref[...]
    o_ref[...] = ((yn + bonus_ref[...]) * g_ref[...]).astype(o_ref.dtype)


def _rwkv(rw_all, batch, t, t_valid, prev, s0, lp, o_prev):
    proj = rw_all.shape[1]
    n_all = o_prev.shape[0]
    row_off = 0
    n = RW_HEAD_DIM
    chunk = RW_CHUNK
    assert chunk == n and t % chunk == 0 and (t_valid == t or t == chunk)
    n_sq = chunk.bit_length() - 2
    nc = t // chunk
    gp = min(RW_PREP_CHUNKS, nc)
    gs = min(RW_SCAN_CHUNKS, nc)
    rows_p = gp * chunk
    rows_s = gs * chunk
    n_tok = batch * t
    row = lambda x: x.reshape(1, -1).astype(F32)
    r = lax.broadcasted_iota(jnp.int32, (rows_p, rows_p), 0)
    c = lax.broadcasted_iota(jnp.int32, (rows_p, rows_p), 1)
    tri = ((c <= r) & (c // chunk == r // chunk)).astype(BF16)
    hr = lax.broadcasted_iota(jnp.int32, (RW_WIDTH, RW_WIDTH), 0) // n
    hc = lax.broadcasted_iota(jnp.int32, (RW_WIDTH, RW_WIDTH), 1) // n
    bd = (hr == hc).astype(BF16)
    group_mask = (hr == hc)[:GROUP_WIDTH, :GROUP_WIDTH]
    const = lambda shape: pl.BlockSpec(shape, lambda b, c: (0,) * len(shape))

    blk_p = row_off // rows_p
    prep_args = [rw_all, rw_all, prev.astype(F32), row(lp['rw_mu']), row(lp['rw_w0']), lp['rw_w2'].astype(F32),
                 row(lp['rw_a0']), lp['rw_a2'].astype(F32), lp['rw_g2'].astype(BF16), row(lp['rw_k_k']),
                 row(lp['rw_k_a']), row(lp['rw_r_k']), tri, bd, group_mask.astype(BF16), group_mask.astype(F32)]
    tok_spec = lambda rows: pl.BlockSpec((rows, RW_WIDTH), lambda b, c: (b * (t // rows) + c, 0))
    mat_spec = lambda g: pl.BlockSpec((g * n, RW_WIDTH), lambda b, c: (b * (nc // g) + c, 0))
    tok_shape = jax.ShapeDtypeStruct((n_tok, RW_WIDTH), F32)
    mat_shape = jax.ShapeDtypeStruct((batch * nc * n, RW_WIDTH), F32)
    rhat, y0, a_mat, d_mat, bonus, g = pl.pallas_call(
        functools.partial(_rwkv_prep_body, chunk=chunk, groups=gp, n_sq=n_sq, t_valid=t_valid),
        grid=(batch, nc // gp),
        in_specs=[
            pl.BlockSpec((rows_p, proj), lambda b, c: (blk_p + b * (nc // gp) + c, 0)),
            pl.BlockSpec((8, proj), lambda b, c: (jnp.maximum((row_off + b * t + c * rows_p) // 8 - 1, 0), 0)),
            pl.BlockSpec((1, 1, proj), lambda b, c: (b, 0, 0)),
        ] + [const(x.shape) for x in prep_args[3:]],
        out_specs=[tok_spec(rows_p), tok_spec(rows_p), mat_spec(gp), mat_spec(gp), tok_spec(rows_p),
                   tok_spec(rows_p)],
        out_shape=[tok_shape, tok_shape, mat_shape, mat_shape, tok_shape, tok_shape],
        compiler_params=_params(("parallel", "parallel")),
        name="rwkv7_prep",
    )(*prep_args)

    blk_s = row_off // rows_s
    scan_args = [rhat, y0, a_mat, d_mat, bonus, g, s0.astype(F32), row(lp['rw_ln_g']), row(lp['rw_ln_b']), bd,
                 group_mask.astype(BF16), o_prev]
    state_spec = pl.BlockSpec((1, RW_HEADS, n, n), lambda b, c: (b, 0, 0, 0))
    n_in = len(scan_args) - 1

    def scan_body(*refs):
        _rwkv_scan_body(*refs[:n_in], *refs[n_in + 1:], chunk=chunk, groups=gs, steps=nc // gs)

    o, state = pl.pallas_call(
        scan_body,
        grid=(batch, nc // gs),
        in_specs=[tok_spec(rows_s), tok_spec(rows_s), mat_spec(gs), mat_spec(gs), tok_spec(rows_s),
                  tok_spec(rows_s), state_spec, const((1, RW_WIDTH)), const((1, RW_WIDTH)), const(bd.shape),
                  const(group_mask.shape), pl.BlockSpec(memory_space=pl.ANY)],
        out_specs=[pl.BlockSpec((rows_s, RW_WIDTH), lambda b, c: (blk_s + b * (nc // gs) + c, 0)), state_spec],
        out_shape=[jax.ShapeDtypeStruct((n_all, RW_WIDTH), BF16),
                   jax.ShapeDtypeStruct((batch, RW_HEADS, n, n), F32)],
        scratch_shapes=[pltpu.VMEM((n, RW_WIDTH), F32), pltpu.VMEM((rows_s, RW_WIDTH), F32)],
        input_output_aliases={n_in: 0},
        compiler_params=_params(("parallel", "arbitrary")),
        name="rwkv7_scan",
    )(*scan_args)
    return o, state


def _merge_body(oa_ref, ob_ref, gab_ref, xa_ref, xb_ref, wa_ref, wb_ref, wo_ref, g2_ref, wr_ref, br_ref, tri_ref,
                x2_ref, h2_ref, idx_ref, gate_ref, rank_ref, cnt_ref, carry_scr, *, tiles_a):
    i = pl.program_id(0)
    d = xa_ref.shape[-1]

    @pl.when(i == 0)
    def _():
        carry_scr[...] = jnp.zeros_like(carry_scr)

    gab = gab_ref[...]
    merged = (jax.nn.sigmoid(gab[:, :d]) * _dot(oa_ref[...], wa_ref[...])
              + jax.nn.sigmoid(gab[:, d:]) * _dot(ob_ref[...], wb_ref[...]))
    x = jnp.where(i < tiles_a, xa_ref[...], xb_ref[...])
    x2 = x + _dot(merged.astype(BF16), wo_ref[...])
    x2_ref[...] = x2
    h2 = _rmsnorm(x2, g2_ref[...])
    h2_ref[...] = h2
    logits = _dot3(h2, wr_ref[...]) + br_ref[...]

    tm, ne = logits.shape
    col = lax.broadcasted_iota(jnp.int32, (tm, ne), 1)
    c4 = lax.broadcasted_iota(jnp.int32, (tm, TOP_K), 1)
    work = logits
    tops, idxs = [], []
    for _ in range(TOP_K):
        m = jnp.max(work, axis=-1, keepdims=True)
        ix = jnp.min(jnp.where(work == m, col, ne), axis=-1, keepdims=True)
        tops.append(m)
        idxs.append(ix)
        work = jnp.where(col == ix, -jnp.inf, work)
    es = [jnp.exp(m - tops[0]) for m in tops]
    denom = es[0] + es[1] + es[2] + es[3]
    onehot = jnp.zeros((tm, ne), F32)
    for ix in idxs:
        onehot = onehot + (col == ix).astype(F32)
    before = carry_scr[...] + _dot(tri_ref[...], onehot.astype(BF16))
    idx_out = jnp.zeros((tm, TOP_K), jnp.int32)
    gate_out = jnp.zeros((tm, TOP_K), F32)
    rank_out = jnp.zeros((tm, TOP_K), jnp.int32)
    for kk in range(TOP_K):
        rk = jnp.sum(jnp.where(col == idxs[kk], before, 0.0), axis=-1, keepdims=True)
        idx_out = jnp.where(c4 == kk, idxs[kk], idx_out)
        gate_out = jnp.where(c4 == kk, es[kk] / denom, gate_out)
        rank_out = jnp.where(c4 == kk, rk.astype(jnp.int32), rank_out)
    idx_ref[...] = idx_out
    gate_ref[...] = gate_out
    rank_ref[...] = rank_out
    carry = carry_scr[...] + jnp.sum(onehot, axis=0, keepdims=True)
    carry_scr[...] = carry
    cnt_ref[...] = carry.astype(jnp.int32)


def _merge_route(oa, ob, gab, xa, xb, lp, tm):
    d = xa.shape[1]
    n = xa.shape[0] + xb.shape[0]
    assert xa.shape[0] % tm == 0 and xb.shape[0] % tm == 0
    tiles_a = xa.shape[0] // tm
    ne = N_EXPERTS
    r = lax.broadcasted_iota(jnp.int32, (tm, tm), 0)
    c = lax.broadcasted_iota(jnp.int32, (tm, tm), 1)
    tri = (c < r).astype(BF16)
    const = lambda shape: pl.BlockSpec(shape, lambda i: (0,) * len(shape))
    rowblk = lambda wd: pl.BlockSpec((tm, wd), lambda i: (i, 0))
    wa = lp['w_branch_a'].astype(BF16)
    wb = lp['w_branch_b'].astype(BF16)
    wo = lp['w_out'].astype(BF16)
    return pl.pallas_call(
        functools.partial(_merge_body, tiles_a=tiles_a),
        grid=(n // tm,),
        in_specs=[rowblk(oa.shape[1]), rowblk(ob.shape[1]), rowblk(2 * d),
                  pl.BlockSpec((tm, d), lambda i: (jnp.minimum(i, tiles_a - 1), 0)),
                  pl.BlockSpec((tm, d), lambda i: (jnp.maximum(i - tiles_a, 0), 0)),
                  const(wa.shape), const(wb.shape), const(wo.shape), const((1, d)), const((d, ne)),
                  const((1, ne)), const((tm, tm))],
        out_specs=[rowblk(d), rowblk(d), rowblk(TOP_K), rowblk(TOP_K), rowblk(TOP_K), const((1, ne))],
        out_shape=[jax.ShapeDtypeStruct((n, d), F32), jax.ShapeDtypeStruct((n, d), F32),
                   jax.ShapeDtypeStruct((n, TOP_K), jnp.int32), jax.ShapeDtypeStruct((n, TOP_K), F32),
                   jax.ShapeDtypeStruct((n, TOP_K), jnp.int32), jax.ShapeDtypeStruct((1, ne), jnp.int32)],
        scratch_shapes=[pltpu.VMEM((1, ne), F32)],
        compiler_params=_params(("arbitrary",)),
        name="merge_route",
    )(oa, ob, gab, xa, xb, wa, wb, wo, lp['norm2_g'].reshape(1, d).astype(F32), lp['w_router'].astype(F32),
      lp['b_router'].reshape(1, ne).astype(F32), tri)


def _for_each_row(n_rows, fn):
    def body(g, carry):
        base = pl.multiple_of(g * SUBLANES, SUBLANES)
        for j in range(SUBLANES):
            fn(base + j)
        return carry

    lax.fori_loop(0, n_rows // SUBLANES, body, 0)


def _dispatch_body(dest_ref, end_ref, padded_ref, h_ref, xs_ref, zeros_ref, sem, zsem, *, tile, bm):
    @pl.when(pl.program_id(0) == 0)
    def _():
        zeros_ref[...] = jnp.zeros_like(zeros_ref)

        def clear(e):
            start = pl.multiple_of(end_ref[e] - bm, bm)
            return pltpu.make_async_copy(zeros_ref, xs_ref.at[pl.ds(start, bm)], zsem)

        for e in range(N_EXPERTS):
            @pl.when(padded_ref[e] > 0)
            def _():
                clear(e).start()

        for e in range(N_EXPERTS):
            @pl.when(padded_ref[e] > 0)
            def _():
                clear(e).wait()

    def copy(t, kk):
        return pltpu.make_async_copy(h_ref.at[pl.ds(t, 1)],
                                     xs_ref.at[pl.ds(dest_ref[0, 0, t * TOP_K + kk], 1)], sem)

    def issue(t):
        for kk in range(TOP_K):
            copy(t, kk).start(priority=kk % 2)

    def drain(t):
        for kk in range(TOP_K):
            copy(t, kk).wait()

    _for_each_row(tile, issue)
    _for_each_row(tile, drain)


def _dispatch(h2, dest, pad_end, padded, n_rows, tile, bm):
    n, d = h2.shape
    nt = n // tile
    smem = pl.BlockSpec(memory_space=pltpu.SMEM)
    return pl.pallas_call(
        functools.partial(_dispatch_body, tile=tile, bm=bm),
        grid=(nt,),
        in_specs=[pl.BlockSpec((1, 1, tile * TOP_K), lambda i: (i, 0, 0), memory_space=pltpu.SMEM), smem, smem,
                  pl.BlockSpec((tile, d), lambda i: (i, 0))],
        out_specs=pl.BlockSpec(memory_space=pl.ANY),
        out_shape=jax.ShapeDtypeStruct((n_rows, d), F32),
        scratch_shapes=[pltpu.VMEM((bm, d), F32), pltpu.SemaphoreType.DMA, pltpu.SemaphoreType.DMA],
        compiler_params=_params(("arbitrary",)),
        name="moe_dispatch",
    )(dest.reshape(nt, 1, tile * TOP_K), pad_end.astype(jnp.int32), padded.astype(jnp.int32), h2)


def _expert_body(be_ref, used_ref, xs_ref, wgu_ref, bgu_ref, wd_ref, bd_ref, ys_ref, wgu16, wd16):
    i = pl.program_id(0)
    de = wd_ref.shape[1]

    @pl.when(jnp.logical_or(i == 0, be_ref[i] != be_ref[jnp.maximum(i - 1, 0)]))
    def _():
        wgu16[...] = wgu_ref[0].astype(BF16)
        wd16[...] = wd_ref[0].astype(BF16)

    @pl.when(i < used_ref[0])
    def _():
        gu = _dot(xs_ref[...].astype(BF16), wgu16[...]) + bgu_ref[0]
        g = jnp.minimum(gu[:, :de], SWIGLU_LIMIT)
        u = jnp.clip(gu[:, de:], -SWIGLU_LIMIT, SWIGLU_LIMIT)
        act = (u + 1.0) * (g * jax.nn.sigmoid(g * SWIGLU_ALPHA))
        ys_ref[...] = _dot(act.astype(BF16), wd16[...]) + bd_ref[0]

    @pl.when(i >= used_ref[0])
    def _():
        ys_ref[...] = jnp.zeros_like(ys_ref)


def _experts(xs, block_expert, n_used, wgu, bgu, wd, bd, bm):
    n_rows, d = xs.shape
    ne, _, de2 = wgu.shape
    nb = n_rows // bm
    grid_spec = pltpu.PrefetchScalarGridSpec(
        num_scalar_prefetch=2,
        grid=(nb,),
        in_specs=[
            pl.BlockSpec((bm, d), lambda i, be, nu: (i, 0)),
            pl.BlockSpec((1, d, de2), lambda i, be, nu: (be[i], 0, 0)),
            pl.BlockSpec((1, 1, de2), lambda i, be, nu: (be[i], 0, 0)),
            pl.BlockSpec((1, de2 // 2, d), lambda i, be, nu: (be[i], 0, 0)),
            pl.BlockSpec((1, 1, d), lambda i, be, nu: (be[i], 0, 0)),
        ],
        out_specs=pl.BlockSpec((bm, d), lambda i, be, nu: (i, 0)),
        scratch_shapes=[pltpu.VMEM((d, de2), BF16), pltpu.VMEM((de2 // 2, d), BF16)],
    )
    return pl.pallas_call(
        _expert_body,
        grid_spec=grid_spec,
        out_shape=jax.ShapeDtypeStruct((n_rows, d), F32),
        compiler_params=_params(("arbitrary",)),
        name="moe_experts",
    )(block_expert, n_used, xs, wgu, bgu.reshape(ne, 1, de2), wd, bd.reshape(ne, 1, d))


def _combine_body(dest_ref, next_ref, ys_ref, gate_ref, x2_ref, g_ref, ya_ref, yb_ref, buf, sems, *,
                  tile, tiles_a, steps):
    i = pl.program_id(0)
    slot = i % 2

    def copy(dref, s, t, kk):
        return pltpu.make_async_copy(ys_ref.at[pl.ds(dref[0, 0, t * TOP_K + kk], 1)],
                                     buf.at[s, kk, pl.ds(t, 1)], sems.at[s])

    def issue(dref, s):
        def body(t):
            for kk in range(TOP_K):
                copy(dref, s, t, kk).start(priority=kk % 2)
        _for_each_row(tile, body)

    @pl.when(i == 0)
    def _():
        issue(dest_ref, 0)

    @pl.when(i + 1 < steps)
    def _():
        issue(next_ref, 1 - slot)

    def drain(t):
        for kk in range(TOP_K):
            copy(dest_ref, slot, t, kk).wait()

    _for_each_row(tile, drain)
    gate = gate_ref[...]
    moe = gate[:, 0:1] * buf[slot, 0]
    for kk in range(1, TOP_K):
        moe = moe + gate[:, kk:kk + 1] * buf[slot, kk]
    y = _rmsnorm(x2_ref[...] + moe, g_ref[...])

    @pl.when(i < tiles_a)
    def _():
        ya_ref[...] = y

    @pl.when(i >= tiles_a)
    def _():
        yb_ref[...] = y


def _combine(ys, dest, gate, x2, final_g, tile, n_a):
    n, d = x2.shape
    nt = n // tile
    tiles_a = n_a // tile
    dest3 = dest.reshape(nt, 1, tile * TOP_K)
    return pl.pallas_call(
        functools.partial(_combine_body, tile=tile, tiles_a=tiles_a, steps=nt),
        grid=(nt,),
        in_specs=[pl.BlockSpec((1, 1, tile * TOP_K), lambda i: (i, 0, 0), memory_space=pltpu.SMEM),
                  pl.BlockSpec((1, 1, tile * TOP_K), lambda i: (jnp.minimum(i + 1, nt - 1), 0, 0),
                               memory_space=pltpu.SMEM),
                  pl.BlockSpec(memory_space=pl.ANY),
                  pl.BlockSpec((tile, TOP_K), lambda i: (i, 0)),
                  pl.BlockSpec((tile, d), lambda i: (i, 0)),
                  pl.BlockSpec((1, d), lambda i: (0, 0))],
        out_specs=[pl.BlockSpec((tile, d), lambda i: (jnp.minimum(i, tiles_a - 1), 0)),
                   pl.BlockSpec((tile, d), lambda i: (jnp.maximum(i - tiles_a, 0), 0))],
        out_shape=[jax.ShapeDtypeStruct((n_a, d), F32), jax.ShapeDtypeStruct((n - n_a, d), F32)],
        scratch_shapes=[pltpu.VMEM((2, TOP_K, tile, d), F32), pltpu.SemaphoreType.DMA((2,))],
        compiler_params=_params(("arbitrary",)),
        name="moe_combine",
    )(dest3, dest3, ys, gate, x2, final_g.reshape(1, d).astype(F32))


def _moe(h2, x2, idx, gate, rank, counts, lp, final_g, n_a):
    n, d = x2.shape
    bm = EXPERT_ROWS
    counts = counts.reshape(N_EXPERTS)
    padded = (counts + bm - 1) // bm * bm
    pad_end = jnp.cumsum(padded)
    dest = ((pad_end - padded)[idx] + rank).reshape(n * TOP_K).astype(jnp.int32)
    n_blocks = -(-(n * TOP_K) // bm) + N_EXPERTS
    block_start = jnp.arange(n_blocks, dtype=jnp.int32) * bm
    block_expert = jnp.minimum(jnp.sum(pad_end[None, :] <= block_start[:, None], axis=1),
                               N_EXPERTS - 1).astype(jnp.int32)
    n_used = (pad_end[-1:] // bm).astype(jnp.int32)
    xs = _dispatch(h2, dest, pad_end, padded, n_blocks * bm, DISPATCH_TILE, bm)
    ys = _experts(xs, block_expert, n_used, lp['w_gate_up'].astype(F32), lp['b_gate_up'].astype(F32),
                  lp['w_down'].astype(F32), lp['b_down'].astype(F32), bm)
    return _combine(ys, dest, gate, x2, final_g, COMBINE_TILE, n_a)


def _to_heads(t, b, s):
    return t.reshape(b, s, SB_HEADS, SB_HEAD_DIM).transpose(0, 2, 1, 3)


def _from_heads(t):
    b, h, s, dh = t.shape
    return t.transpose(0, 2, 1, 3).reshape(b, s, h * dh)


def _layer(x_p, x_s, cache_k, cache_v, state_rwkv, state_shift, lp, final_g):
    bp, tp, d = x_p.shape
    bs, ts, _ = x_s.shape
    sbw = SB_HEADS * SB_HEAD_DIM
    rw_proj = lp['rw_mu'].shape[0]
    n_p = bp * tp
    n = n_p + bs * ts
    x_s = x_s.reshape(bs * ts, d)
    scale = LOG2E * SB_HEAD_DIM ** -0.5
    qkv16, kp, vp, kv_new, rw, gab = _inproj(x_p, x_s, lp['norm1_g'].astype(F32), lp['w_in'].astype(BF16),
                                             rw_proj, scale, ROW_TILE)
    pairs = SB_HEADS // 2
    oa = _stick_breaking(qkv16, 0, 0, qkv16, pairs, 2 * pairs, bp, tp, 0, SB_BLOCK, SB_SUB,
                         jnp.zeros((n, sbw), BF16))
    past = cache_k.shape[2]
    new16 = qkv16[n_p:].reshape(bs, ts, 3 * sbw)
    kv_s = jnp.concatenate([
        jnp.concatenate([_from_heads(cache_k).astype(BF16), new16[:, :, sbw:2 * sbw]], axis=1),
        jnp.concatenate([_from_heads(cache_v).astype(BF16), new16[:, :, 2 * sbw:]], axis=1)], axis=2)
    oa = _stick_breaking(qkv16, n_p, 0, kv_s.reshape(bs * (past + ts), 2 * sbw), 0, pairs, bs, ts, past, ts,
                         min(SB_SUB, past), oa)
    ks, vs = kv_new[:, :sbw], kv_new[:, sbw:]

    s0_p = jnp.zeros((bp, RW_HEADS, RW_HEAD_DIM, RW_HEAD_DIM), F32)
    prev_p = jnp.zeros((bp, 1, rw_proj), F32)
    ob, st_p = _rwkv(rw, bp, tp, tp, prev_p, s0_p, lp, jnp.zeros((n, RW_WIDTH), BF16))
    rw_s = rw[n_p:].reshape(bs, ts, rw_proj)
    ts_pad = -(-ts // RW_CHUNK) * RW_CHUNK
    rw_s_pad = jnp.pad(rw_s, ((0, 0), (0, ts_pad - ts), (0, 0))).reshape(bs * ts_pad, rw_proj)
    ob_s, st_s = _rwkv(rw_s_pad, bs, ts_pad, ts, state_shift, state_rwkv, lp,
                       jnp.zeros((bs * ts_pad, RW_WIDTH), BF16))
    ob = lax.dynamic_update_slice(
        ob, ob_s.reshape(bs, ts_pad, RW_WIDTH)[:, :ts].reshape(bs * ts, RW_WIDTH), (n_p, 0))
    sh_p = jnp.stack([rw[(b + 1) * tp - 1:(b + 1) * tp] for b in range(bp)])
    sh_s = rw_s[:, ts - 1:]

    x2, h2, idx, gate, rank, counts = _merge_route(oa, ob, gab, x_p.reshape(n_p, d), x_s, lp, MERGE_TILE)
    y = _moe(h2, x2, idx, gate, rank, counts, lp, final_g, n_p)
    return (y, kp, vp, st_p, sh_p, _to_heads(ks, bs, ts), _to_heads(vs, bs, ts), st_s, sh_s)


def kernel(x_prompt, x_sample, cache_sb_k, cache_sb_v, state_rwkv, state_shift, norm1_g, w_in, rw_mu, rw_w0, rw_w2, rw_a0, rw_a2, rw_g2, rw_k_k, rw_k_a, rw_r_k, rw_ln_g, rw_ln_b, w_branch_a, w_branch_b, w_out, norm2_g, w_router, b_router, w_gate_up, b_gate_up, w_down, b_down, final_norm_g):
    depth = w_in.shape[0]
    assert depth == 1, "the final RMSNorm is fused into the last (only) layer"
    bp, tp, d = x_prompt.shape
    bs, ts, _ = x_sample.shape
    lp = dict(norm1_g=norm1_g[0], w_in=w_in[0], rw_mu=rw_mu[0], rw_w0=rw_w0[0], rw_w2=rw_w2[0], rw_a0=rw_a0[0],
              rw_a2=rw_a2[0], rw_g2=rw_g2[0], rw_k_k=rw_k_k[0], rw_k_a=rw_k_a[0], rw_r_k=rw_r_k[0].reshape(-1),
              rw_ln_g=rw_ln_g[0], rw_ln_b=rw_ln_b[0], w_branch_a=w_branch_a[0], w_branch_b=w_branch_b[0],
              w_out=w_out[0], norm2_g=norm2_g[0], w_router=w_router[0], b_router=b_router[0],
              w_gate_up=w_gate_up[0], b_gate_up=b_gate_up[0], w_down=w_down[0], b_down=b_down[0])
    y, kp, vp, st_p, sh_p, ks, vs, st_s, sh_s = _layer(
        x_prompt, x_sample, cache_sb_k[0], cache_sb_v[0], state_rwkv[0], state_shift[0], lp, final_norm_g)
    return (y[0].reshape(bp, tp, d), y[1].reshape(bs, ts, d),
            kp[None], vp[None], st_p[None], sh_p[None], ks[None], vs[None], st_s[None], sh_s[None])
```

```python
import functools
import math

import jax
import jax.numpy as jnp
from jax import lax
from jax.experimental import pallas as pl
from jax.experimental.pallas import tpu as pltpu
from jax.experimental.pallas import tpu_sc as plsc

F32 = jnp.float32
BF16 = jnp.bfloat16

SB_HEADS = 8
SB_HEAD_DIM = 64
RW_HEADS = 8
RW_HEAD_DIM = 64
RW_WIDTH = RW_HEADS * RW_HEAD_DIM
RW_DECAY_LORA = 64
RW_AAA_LORA = 64
RW_GATE_LORA = 128
RW_GN_EPS = 64e-5
N_EXPERTS = 32
TOP_K = 4
SWIGLU_LIMIT = 7.0
SWIGLU_ALPHA = 1.702
RMS_EPS = 1e-6

VMEM_LIMIT_BYTES = 56 * 1024 * 1024
ROW_TILE = 256
MERGE_TILE = 512
SB_BLOCK = 512
SB_SUB = 256
LOG2E = 1.4426950408889634
SB_DEAD_LOG2 = -200.0
RW_CHUNK = 64
RW_PREP_CHUNKS = 4
RW_SCAN_CHUNKS = 4
EXPERT_ROWS = 512
DISPATCH_TILE = 2560
COMBINE_TILE = 256
SUBLANES = 8
SC_GATHER_WINDOW = 128
SC_ROW_SLAB = 256


def _params(semantics):
    return pltpu.CompilerParams(dimension_semantics=semantics, vmem_limit_bytes=VMEM_LIMIT_BYTES)


def _dot(a, b):
    return jnp.dot(a, b, preferred_element_type=F32)


def _dot_nt(a, b):
    return lax.dot_general(a, b, (((1,), (1,)), ((), ())), preferred_element_type=F32)


def _softplus_parts(z):
    l = jnp.log(1.0 + jnp.exp(-jnp.abs(z)))
    sp = jnp.maximum(z, 0.0) + l
    return sp, sp - z


def _rmsnorm(x, g):
    return x * lax.rsqrt(jnp.mean(x * x, axis=-1, keepdims=True) + RMS_EPS) * g


def _inproj_body(xa_ref, xb_ref, g_ref, w_ref, qkv16_ref, ka_ref, va_ref, kvb_ref, rw_ref, gab_ref, *,
                 q_scale, tiles_a):
    i = pl.program_id(0)
    x = jnp.where(i < tiles_a, xa_ref[...], xb_ref[...])
    h = _rmsnorm(x, g_ref[...]).astype(BF16)
    sbw = kvb_ref.shape[-1] // 2
    dh = ka_ref.shape[-1]
    q = _dot(h, w_ref[:, 0:sbw])
    qkv16_ref[:, 0:sbw] = (q * q_scale).astype(BF16)
    k = _dot(h, w_ref[:, sbw:2 * sbw])
    v = _dot(h, w_ref[:, 2 * sbw:3 * sbw])
    qkv16_ref[:, sbw:2 * sbw] = k.astype(BF16)
    qkv16_ref[:, 2 * sbw:3 * sbw] = v.astype(BF16)
    off = 3 * sbw
    for ref in (rw_ref, gab_ref):
        width = ref.shape[-1]
        for c0 in range(0, width, sbw):
            cw = min(sbw, width - c0)
            ref[:, c0:c0 + cw] = _dot(h, w_ref[:, off + c0:off + c0 + cw])
        off += width

    @pl.when(i < tiles_a)
    def _():
        for hd in range(sbw // dh):
            ka_ref[0, hd] = k[:, hd * dh:(hd + 1) * dh]
            va_ref[0, hd] = v[:, hd * dh:(hd + 1) * dh]

    @pl.when(i >= tiles_a)
    def _():
        kvb_ref[:, 0:sbw] = k
        kvb_ref[:, sbw:2 * sbw] = v


def _inproj(xa, xb, g, w_bf16, rw_proj, q_scale, tm):
    ba, ta, d = xa.shape
    nb = xb.shape[0]
    n = ba * ta + nb
    sbw = SB_HEADS * SB_HEAD_DIM
    total = w_bf16.shape[1]
    assert total == 3 * sbw + rw_proj + 2 * d and ta % tm == 0 and nb % tm == 0
    tiles_a = ba * ta // tm
    per_seq = ta // tm
    last_a = tiles_a - 1
    row = lambda wd: pl.BlockSpec((tm, wd), lambda i: (i, 0))
    head_major = pl.BlockSpec(
        (1, SB_HEADS, tm, SB_HEAD_DIM),
        lambda i: (jnp.minimum(i, last_a) // per_seq, 0, jnp.minimum(i, last_a) % per_seq, 0))
    hm_shape = jax.ShapeDtypeStruct((ba, SB_HEADS, ta, SB_HEAD_DIM), F32)
    return pl.pallas_call(
        functools.partial(_inproj_body, q_scale=q_scale, tiles_a=tiles_a),
        grid=(n // tm,),
        in_specs=[
            pl.BlockSpec((tm, d), lambda i: (jnp.minimum(i, last_a), 0)),
            pl.BlockSpec((tm, d), lambda i: (jnp.maximum(i - tiles_a, 0), 0)),
            pl.BlockSpec((1, d), lambda i: (0, 0)),
            pl.BlockSpec((d, total), lambda i: (0, 0), pipeline_mode=pl.Buffered(1)),
        ],
        out_specs=[row(3 * sbw), head_major, head_major,
                   pl.BlockSpec((tm, 2 * sbw), lambda i: (jnp.maximum(i - tiles_a, 0), 0)),
                   row(rw_proj), row(2 * d)],
        out_shape=[jax.ShapeDtypeStruct((n, 3 * sbw), BF16), hm_shape, hm_shape,
                   jax.ShapeDtypeStruct((nb, 2 * sbw), F32),
                   jax.ShapeDtypeStruct((n, rw_proj), F32), jax.ShapeDtypeStruct((n, 2 * d), F32)],
        compiler_params=_params(("arbitrary",)),
        name="inproj",
    )(xa.reshape(ba * ta, d), xb, g.reshape(1, d), w_bf16)


def _sb_tile(q, kb, vb, ntri, carry, mask, sb):
    tk = kb.shape[0]
    z = _dot_nt(q, kb)
    neg_abs = lax.bitcast_convert_type(lax.bitcast_convert_type(z, jnp.uint32) | jnp.uint32(0x80000000), F32)
    sp = jnp.maximum(z, 0.0) + jnp.log(1.0 + jnp.exp2(neg_abs)) * LOG2E
    if mask is not None:
        sp = jnp.where(mask, sp, 0.0)
    hi = sp.astype(BF16)
    lo = (sp - hi.astype(F32)).astype(BF16)
    log_beta = z - sp
    wts = []
    for j in range(tk // sb - 1, -1, -1):
        cols = slice(j * sb, (j + 1) * sb)
        later = _dot(hi[:, cols], ntri) + _dot(lo[:, cols], ntri) + carry
        w = jnp.exp2(log_beta[:, cols] + later)
        if mask is not None:
            w = jnp.where(mask[:, cols], w, 0.0)
        wts.append(w.astype(BF16))
        carry = carry - jnp.sum(sp[:, cols], axis=-1, keepdims=True)
    wts = wts[0] if len(wts) == 1 else jnp.concatenate(wts[::-1], axis=1)
    return _dot(wts, vb), carry


def _sb_body(q_ref, k_ref, v_ref, triq_ref, trik_ref, o_in_ref, o_ref, *, tq, tk, past):
    del o_in_ref
    qi = pl.program_id(2)
    q = q_ref[...]
    first = lax.broadcasted_iota(jnp.int32, q.shape, 1) < SB_HEAD_DIM
    zero = jnp.zeros_like(q)
    qs = (jnp.where(first, q, zero), jnp.where(first, zero, q))
    d0 = pl.multiple_of(past + qi * tq, tq)
    row = lax.broadcasted_iota(jnp.int32, (tq, tq), 0)
    col = lax.broadcasted_iota(jnp.int32, (tq, tq), 1)
    kd = k_ref[pl.ds(d0, tq), :]
    vd = v_ref[pl.ds(d0, tq), :]
    sbq = triq_ref.shape[0]
    state = []
    for qh in qs:
        state.extend(_sb_tile(qh, kd, vd, triq_ref[...], jnp.zeros((tq, 1), F32), col < row, sbq))
    nb = (past + qi * tq) // tk
    sbk = trik_ref.shape[0]

    def live(state):
        return jnp.max(jnp.maximum(state[1], state[3])) > SB_DEAD_LOG2

    def cond(loop):
        i, alive, _ = loop
        return jnp.logical_and(i < nb, alive)

    def step(loop):
        i, _, state = loop
        s0 = pl.multiple_of((nb - 1 - i) * tk, tk)
        kb = k_ref[pl.ds(s0, tk), :]
        vb = v_ref[pl.ds(s0, tk), :]
        new = []
        for h, qh in enumerate(qs):
            out, carry = _sb_tile(qh, kb, vb, trik_ref[...], state[2 * h + 1], None, sbk)
            new.extend((state[2 * h] + out, carry))
        return i + 1, live(new), tuple(new)

    _, _, state = lax.while_loop(cond, step, (jnp.int32(0), live(state), tuple(state)))
    o_ref[...] = jnp.where(first, state[0], state[2]).astype(o_ref.dtype)


def _later_matrix(n):
    r = lax.broadcasted_iota(jnp.int32, (n, n), 0)
    c = lax.broadcasted_iota(jnp.int32, (n, n), 1)
    return jnp.where(r > c, -1.0, 0.0).astype(BF16)


def _stick_breaking(q, q_row0, q_lane0, kv, k_lane0, v_lane0, batch, t, past, tq, tk, o_prev):
    s = past + t
    pair = 2 * SB_HEAD_DIM
    sbq = min(tq, SB_SUB)
    sbk = min(tk, SB_SUB)
    nq = t // tq
    qblk = q_row0 // tq
    q_spec = lambda lane0: pl.BlockSpec((tq, pair), lambda b, h, i: (qblk + b * nq + i, lane0 + h))
    kv_spec = lambda lane0: pl.BlockSpec((s, pair), lambda b, h, i: (b, lane0 + h))
    return pl.pallas_call(
        functools.partial(_sb_body, tq=tq, tk=tk, past=past),
        grid=(batch, SB_HEADS // 2, nq),
        in_specs=[
            q_spec(q_lane0), kv_spec(k_lane0), kv_spec(v_lane0),
            pl.BlockSpec((sbq, sbq), lambda b, h, i: (0, 0)),
            pl.BlockSpec((sbk, sbk), lambda b, h, i: (0, 0)),
            pl.BlockSpec(memory_space=pl.ANY),
        ],
        out_specs=q_spec(0),
        out_shape=jax.ShapeDtypeStruct(o_prev.shape, BF16),
        input_output_aliases={5: 0},
        compiler_params=_params(("parallel", "parallel", "arbitrary")),
        name="stick_breaking",
    )(q, kv, kv, _later_matrix(sbq), _later_matrix(sbk), o_prev)


def _split_bf16(x):
    hi = x.astype(BF16)
    return hi, (x - hi.astype(F32)).astype(BF16)


def _dot3(a, b, dot=_dot):
    ah, al = _split_bf16(a)
    bh, bl = _split_bf16(b)
    return dot(ah, bh) + dot(ah, bl) + dot(al, bh)


def _dot_ones_rhs(x, ones_bf16):
    hi, lo = _split_bf16(x)
    return _dot(hi, ones_bf16) + _dot(lo, ones_bf16)


def _dot_ones_lhs(ones_bf16, x):
    hi, lo = _split_bf16(x)
    return _dot(ones_bf16, hi) + _dot(ones_bf16, lo)


HEADS_PER_MATMUL = 4
GROUP_WIDTH = HEADS_PER_MATMUL * RW_HEAD_DIM


def _block_diag(y, mask):
    return jnp.concatenate([y] * HEADS_PER_MATMUL, axis=0) * mask


def _bdmm(xs, ys, mask, dot=_dot, extended=False):
    xh, xl = xs
    yh, yl = ys
    m = xh.shape[0]
    x2 = jnp.concatenate([xh, xl], axis=0) if extended else xh
    outs = []
    for g in range(RW_WIDTH // GROUP_WIDTH):
        ls = slice(g * GROUP_WIDTH, (g + 1) * GROUP_WIDTH)
        out = dot(x2[:, ls], _block_diag(yh[:, ls], mask))
        if extended:
            out = out[:m] + out[m:] + dot(xh[:, ls], _block_diag(yl[:, ls], mask))
        outs.append(out)
    return jnp.concatenate(outs, axis=1)


def _bd_tn(x, y, mask):
    n = RW_HEAD_DIM
    outs = []
    for g in range(RW_WIDTH // GROUP_WIDTH):
        ls = slice(g * GROUP_WIDTH, (g + 1) * GROUP_WIDTH)
        z = _dot(x[:, ls].T.astype(BF16), y[:, ls].astype(BF16)) * mask
        outs.append(z[0:n] + z[n:2 * n] + z[2 * n:3 * n] + z[3 * n:4 * n])
    return jnp.concatenate(outs, axis=1)


def _rwkv_prep_body(p_ref, pb_ref, prev_ref, mu_ref, w0_ref, w2_ref, a0_ref, a2_ref, g2_ref, kk_ref, ka_ref,
                    rk_ref, tri_ref, bd_ref, mbf_ref, mf32_ref,
                    rhat_ref, y0_ref, a_ref, d_ref, bonus_ref, g_ref, *, chunk, groups, n_sq, t_valid):
    n = RW_HEAD_DIM
    w_ = RW_WIDTH
    p = p_ref[...]
    rows = lax.broadcasted_iota(jnp.int32, p.shape, 0)
    prev_row = jnp.where(pl.program_id(1) == 0, prev_ref[0], pb_ref[7:8, :])
    p_shift = jnp.where(rows == 0, prev_row, pltpu.roll(p, 1, 0))

    xs = p + (p_shift - p) * mu_ref[...]
    r = xs[:, 0:w_]
    k = xs[:, w_:2 * w_]
    v = xs[:, 2 * w_:3 * w_]
    o1 = 3 * w_
    dw = xs[:, o1:o1 + RW_DECAY_LORA]
    da = xs[:, o1 + RW_DECAY_LORA:o1 + RW_DECAY_LORA + RW_AAA_LORA]
    dg = xs[:, o1 + RW_DECAY_LORA + RW_AAA_LORA:]

    sp_neg = _softplus_parts(w0_ref[...] + _dot3(jnp.tanh(dw), w2_ref[...]))[1]
    log_decay = -jnp.exp(-sp_neg - 0.5)
    a = jax.nn.sigmoid(a0_ref[...] + _dot3(da, a2_ref[...]))
    g_ref[...] = _dot(jax.nn.sigmoid(dg).astype(BF16), g2_ref[...])

    bd = bd_ref[...]
    kk = k * kk_ref[...]
    kk = kk / jnp.maximum(jnp.sqrt(_dot_ones_rhs(kk * kk, bd)), 1e-12)
    k2 = k * (1.0 + (a - 1.0) * ka_ref[...])
    a_vec = -kk
    b_vec = kk * a
    bonus_ref[...] = _dot((r * k2 * rk_ref[...]).astype(BF16), bd) * v
    if t_valid < chunk:
        live = lax.broadcasted_iota(jnp.int32, r.shape, 0) < t_valid
        keep = lambda x: jnp.where(live, x, 0.0)
        log_decay, r, k2, v, a_vec, b_vec = (keep(x) for x in (log_decay, r, k2, v, a_vec, b_vec))

    cw = _dot_ones_lhs(tri_ref[...], log_decay)
    cw_last = jnp.concatenate(
        [jnp.broadcast_to(cw[(j + 1) * chunk - 1:(j + 1) * chunk, :], (chunk, w_)) for j in range(groups)], axis=0)
    r_t = r * jnp.exp(cw)
    a_t = a_vec * jnp.exp(cw - log_decay)
    e_neg = jnp.exp(-cw)
    b_t = b_vec * e_neg
    k_t = k2 * e_neg
    e_rel = jnp.exp(cw_last - cw)
    b_h = b_vec * e_rel
    k_h = k2 * e_rel
    w_c = jnp.exp(cw_last)

    rr = lax.broadcasted_iota(jnp.int32, (chunk, w_), 0)
    ss = lax.broadcasted_iota(jnp.int32, (chunk, w_), 1) % n
    strict = ss < rr
    incl = ss <= rr
    eye = (ss == rr).astype(F32)
    m_bf16 = mbf_ref[...]
    m_f32 = mf32_ref[...]

    for j in range(groups):
        rs = slice(j * chunk, (j + 1) * chunk)
        v_j = v[rs]
        v_s = _split_bf16(v_j)
        ar_s = _split_bf16(jnp.concatenate([a_t[rs], r_t[rs]], axis=0))
        gram_b = _bdmm(ar_s, _split_bf16(b_t[rs]), m_bf16, _dot_nt)
        gram_k = _bdmm(ar_s, _split_bf16(k_t[rs]), m_bf16, _dot_nt)
        l_ab = jnp.where(strict, gram_b[:chunk], 0.0)
        l_ak = jnp.where(strict, gram_k[:chunk], 0.0)
        m_rb_s = _split_bf16(jnp.where(incl, gram_b[chunk:], 0.0))
        m_rk_s = _split_bf16(jnp.where(incl, gram_k[chunk:], 0.0))
        l_s = _split_bf16(l_ab)
        t_inv = eye + l_ab
        l_pow = _bdmm(l_s, l_s, m_bf16)
        for i in range(n_sq):
            lp_s = _split_bf16(l_pow)
            t_inv = t_inv + _bdmm(lp_s, _split_bf16(t_inv), m_bf16)
            if i + 1 < n_sq:
                l_pow = _bdmm(lp_s, lp_s, m_bf16)
        t_s = _split_bf16(t_inv)
        p_mat = _bdmm(t_s, _split_bf16(a_t[rs]), m_bf16)
        q_mat = _bdmm(t_s, _split_bf16(_bdmm(_split_bf16(l_ak), v_s, m_bf16)), m_bf16)
        rhat_ref[rs, :] = r_t[rs] + _bdmm(m_rb_s, _split_bf16(p_mat), m_bf16)
        y0_ref[rs, :] = _bdmm(m_rb_s, _split_bf16(q_mat), m_bf16) + _bdmm(m_rk_s, v_s, m_bf16)
        ns = slice(j * n, (j + 1) * n)
        a_ref[ns, :] = _bd_tn(p_mat, b_h[rs], m_f32) + eye * w_c[j * chunk:j * chunk + 1, :]
        d_ref[ns, :] = _bd_tn(jnp.concatenate([q_mat, v_j], axis=0),
                              jnp.concatenate([b_h[rs], k_h[rs]], axis=0), m_f32)


def _rwkv_scan_body(rhat_ref, y0_ref, a_ref, d_ref, bonus_ref, g_ref, s0_ref, lng_ref, lnb_ref, bd_ref, mbf_ref,
                    o_ref, s_ref, s_scr, y_scr, *, chunk, groups, steps):
    n = RW_HEAD_DIM
    c = pl.program_id(1)

    @pl.when(c == 0)
    def _():
        for h in range(RW_HEADS):
            s_scr[:, h * n:(h + 1) * n] = s0_ref[0, h]

    mask = mbf_ref[...]
    s = s_scr[...]
    for j in range(groups):
        rs = slice(j * chunk, (j + 1) * chunk)
        ns = slice(j * n, (j + 1) * n)
        s_s = _split_bf16(s)
        y_scr[rs, :] = y0_ref[rs, :] + _bdmm(_split_bf16(rhat_ref[rs, :]), s_s, mask, _dot_nt, extended=True)
        s = _bdmm(s_s, _split_bf16(a_ref[ns, :]), mask, extended=True) + d_ref[ns, :]
    s_scr[...] = s

    @pl.when(c == steps - 1)
    def _():
        for h in range(RW_HEADS):
            s_ref[0, h] = s[:, h * n:(h + 1) * n]

    bd = bd_ref[...]
    y = y_scr[...]
    mean = _dot_ones_rhs(y, bd) * (1.0 / n)
    yc = y - mean
    var = _dot_ones_rhs(yc * yc, bd) * (1.0 / n)
    yn = yc * lax.rsqrt(var + RW_GN_EPS) * lng_ref[...] + lnb_ref[...]
    o_ref[...] = ((yn + bonus_ref[...]) * g_ref[...]).astype(o_ref.dtype)


def _rwkv(rw_all, batch, t, t_valid, prev, s0, lp, o_prev):
    proj = rw_all.shape[1]
    n_all = o_prev.shape[0]
    row_off = 0
    n = RW_HEAD_DIM
    chunk = RW_CHUNK
    assert chunk == n and t % chunk == 0 and (t_valid == t or t == chunk)
    n_sq = chunk.bit_length() - 2
    nc = t // chunk
    gp = min(RW_PREP_CHUNKS, nc)
    gs = min(RW_SCAN_CHUNKS, nc)
    rows_p = gp * chunk
    rows_s = gs * chunk
    n_tok = batch * t
    row = lambda x: x.reshape(1, -1).astype(F32)
    r = lax.broadcasted_iota(jnp.int32, (rows_p, rows_p), 0)
    c = lax.broadcasted_iota(jnp.int32, (rows_p, rows_p), 1)
    tri = ((c <= r) & (c // chunk == r // chunk)).astype(BF16)
    hr = lax.broadcasted_iota(jnp.int32, (RW_WIDTH, RW_WIDTH), 0) // n
    hc = lax.broadcasted_iota(jnp.int32, (RW_WIDTH, RW_WIDTH), 1) // n
    bd = (hr == hc).astype(BF16)
    group_mask = (hr == hc)[:GROUP_WIDTH, :GROUP_WIDTH]
    const = lambda shape: pl.BlockSpec(shape, lambda b, c: (0,) * len(shape))

    blk_p = row_off // rows_p
    prep_args = [rw_all, rw_all, prev.astype(F32), row(lp['rw_mu']), row(lp['rw_w0']), lp['rw_w2'].astype(F32),
                 row(lp['rw_a0']), lp['rw_a2'].astype(F32), lp['rw_g2'].astype(BF16), row(lp['rw_k_k']),
                 row(lp['rw_k_a']), row(lp['rw_r_k']), tri, bd, group_mask.astype(BF16), group_mask.astype(F32)]
    tok_spec = lambda rows: pl.BlockSpec((rows, RW_WIDTH), lambda b, c: (b * (t // rows) + c, 0))
    mat_spec = lambda g: pl.BlockSpec((g * n, RW_WIDTH), lambda b, c: (b * (nc // g) + c, 0))
    tok_shape = jax.ShapeDtypeStruct((n_tok, RW_WIDTH), F32)
    mat_shape = jax.ShapeDtypeStruct((batch * nc * n, RW_WIDTH), F32)
    rhat, y0, a_mat, d_mat, bonus, g = pl.pallas_call(
        functools.partial(_rwkv_prep_body, chunk=chunk, groups=gp, n_sq=n_sq, t_valid=t_valid),
        grid=(batch, nc // gp),
        in_specs=[
            pl.BlockSpec((rows_p, proj), lambda b, c: (blk_p + b * (nc // gp) + c, 0)),
            pl.BlockSpec((8, proj), lambda b, c: (jnp.maximum((row_off + b * t + c * rows_p) // 8 - 1, 0), 0)),
            pl.BlockSpec((1, 1, proj), lambda b, c: (b, 0, 0)),
        ] + [const(x.shape) for x in prep_args[3:]],
        out_specs=[tok_spec(rows_p), tok_spec(rows_p), mat_spec(gp), mat_spec(gp), tok_spec(rows_p),
                   tok_spec(rows_p)],
        out_shape=[tok_shape, tok_shape, mat_shape, mat_shape, tok_shape, tok_shape],
        compiler_params=_params(("parallel", "parallel")),
        name="rwkv7_prep",
    )(*prep_args)

    blk_s = row_off // rows_s
    scan_args = [rhat, y0, a_mat, d_mat, bonus, g, s0.astype(F32), row(lp['rw_ln_g']), row(lp['rw_ln_b']), bd,
                 group_mask.astype(BF16), o_prev]
    state_spec = pl.BlockSpec((1, RW_HEADS, n, n), lambda b, c: (b, 0, 0, 0))
    n_in = len(scan_args) - 1

    def scan_body(*refs):
        _rwkv_scan_body(*refs[:n_in], *refs[n_in + 1:], chunk=chunk, groups=gs, steps=nc // gs)

    o, state = pl.pallas_call(
        scan_body,
        grid=(batch, nc // gs),
        in_specs=[tok_spec(rows_s), tok_spec(rows_s), mat_spec(gs), mat_spec(gs), tok_spec(rows_s),
                  tok_spec(rows_s), state_spec, const((1, RW_WIDTH)), const((1, RW_WIDTH)), const(bd.shape),
                  const(group_mask.shape), pl.BlockSpec(memory_space=pl.ANY)],
        out_specs=[pl.BlockSpec((rows_s, RW_WIDTH), lambda b, c: (blk_s + b * (nc // gs) + c, 0)), state_spec],
        out_shape=[jax.ShapeDtypeStruct((n_all, RW_WIDTH), BF16),
                   jax.ShapeDtypeStruct((batch, RW_HEADS, n, n), F32)],
        scratch_shapes=[pltpu.VMEM((n, RW_WIDTH), F32), pltpu.VMEM((rows_s, RW_WIDTH), F32)],
        input_output_aliases={n_in: 0},
        compiler_params=_params(("parallel", "arbitrary")),
        name="rwkv7_scan",
    )(*scan_args)
    return o, state


def _merge_body(oa_ref, ob_ref, gab_ref, xa_ref, xb_ref, wa_ref, wb_ref, wo_ref, g2_ref, wr_ref, br_ref, tri_ref,
                x2_ref, h2_ref, idx_ref, gate_ref, rank_ref, cnt_ref, carry_scr, *, tiles_a):
    i = pl.program_id(0)
    d = xa_ref.shape[-1]

    @pl.when(i == 0)
    def _():
        carry_scr[...] = jnp.zeros_like(carry_scr)

    gab = gab_ref[...]
    merged = (jax.nn.sigmoid(gab[:, :d]) * _dot(oa_ref[...], wa_ref[...])
              + jax.nn.sigmoid(gab[:, d:]) * _dot(ob_ref[...], wb_ref[...]))
    x = jnp.where(i < tiles_a, xa_ref[...], xb_ref[...])
    x2 = x + _dot(merged.astype(BF16), wo_ref[...])
    x2_ref[...] = x2
    h2 = _rmsnorm(x2, g2_ref[...])
    h2_ref[...] = h2
    logits = _dot3(h2, wr_ref[...]) + br_ref[...]

    tm, ne = logits.shape
    col = lax.broadcasted_iota(jnp.int32, (tm, ne), 1)
    c4 = lax.broadcasted_iota(jnp.int32, (tm, TOP_K), 1)
    work = logits
    tops, idxs = [], []
    for _ in range(TOP_K):
        m = jnp.max(work, axis=-1, keepdims=True)
        ix = jnp.min(jnp.where(work == m, col, ne), axis=-1, keepdims=True)
        tops.append(m)
        idxs.append(ix)
        work = jnp.where(col == ix, -jnp.inf, work)
    es = [jnp.exp(m - tops[0]) for m in tops]
    denom = es[0] + es[1] + es[2] + es[3]
    onehot = jnp.zeros((tm, ne), F32)
    for ix in idxs:
        onehot = onehot + (col == ix).astype(F32)
    before = carry_scr[...] + _dot(tri_ref[...], onehot.astype(BF16))
    idx_out = jnp.zeros((tm, TOP_K), jnp.int32)
    gate_out = jnp.zeros((tm, TOP_K), F32)
    rank_out = jnp.zeros((tm, TOP_K), jnp.int32)
    for kk in range(TOP_K):
        rk = jnp.sum(jnp.where(col == idxs[kk], before, 0.0), axis=-1, keepdims=True)
        idx_out = jnp.where(c4 == kk, idxs[kk], idx_out)
        gate_out = jnp.where(c4 == kk, es[kk] / denom, gate_out)
        rank_out = jnp.where(c4 == kk, rk.astype(jnp.int32), rank_out)
    idx_ref[...] = idx_out
    gate_ref[...] = gate_out
    rank_ref[...] = rank_out
    carry = carry_scr[...] + jnp.sum(onehot, axis=0, keepdims=True)
    carry_scr[...] = carry
    cnt_ref[...] = carry.astype(jnp.int32)


def _merge_route(oa, ob, gab, xa, xb, lp, tm):
    d = xa.shape[1]
    n = xa.shape[0] + xb.shape[0]
    assert xa.shape[0] % tm == 0 and xb.shape[0] % tm == 0
    tiles_a = xa.shape[0] // tm
    ne = N_EXPERTS
    r = lax.broadcasted_iota(jnp.int32, (tm, tm), 0)
    c = lax.broadcasted_iota(jnp.int32, (tm, tm), 1)
    tri = (c < r).astype(BF16)
    const = lambda shape: pl.BlockSpec(shape, lambda i: (0,) * len(shape))
    rowblk = lambda wd: pl.BlockSpec((tm, wd), lambda i: (i, 0))
    wa = lp['w_branch_a'].astype(BF16)
    wb = lp['w_branch_b'].astype(BF16)
    wo = lp['w_out'].astype(BF16)
    return pl.pallas_call(
        functools.partial(_merge_body, tiles_a=tiles_a),
        grid=(n // tm,),
        in_specs=[rowblk(oa.shape[1]), rowblk(ob.shape[1]), rowblk(2 * d),
                  pl.BlockSpec((tm, d), lambda i: (jnp.minimum(i, tiles_a - 1), 0)),
                  pl.BlockSpec((tm, d), lambda i: (jnp.maximum(i - tiles_a, 0), 0)),
                  const(wa.shape), const(wb.shape), const(wo.shape), const((1, d)), const((d, ne)),
                  const((1, ne)), const((tm, tm))],
        out_specs=[rowblk(d), rowblk(d), rowblk(TOP_K), rowblk(TOP_K), rowblk(TOP_K), const((1, ne))],
        out_shape=[jax.ShapeDtypeStruct((n, d), F32), jax.ShapeDtypeStruct((n, d), F32),
                   jax.ShapeDtypeStruct((n, TOP_K), jnp.int32), jax.ShapeDtypeStruct((n, TOP_K), F32),
                   jax.ShapeDtypeStruct((n, TOP_K), jnp.int32), jax.ShapeDtypeStruct((1, ne), jnp.int32)],
        scratch_shapes=[pltpu.VMEM((1, ne), F32)],
        compiler_params=_params(("arbitrary",)),
        name="merge_route",
    )(oa, ob, gab, xa, xb, wa, wb, wo, lp['norm2_g'].reshape(1, d).astype(F32), lp['w_router'].astype(F32),
      lp['b_router'].reshape(1, ne).astype(F32), tri)


def _for_each_row(n_rows, fn):
    def body(g, carry):
        base = pl.multiple_of(g * SUBLANES, SUBLANES)
        for j in range(SUBLANES):
            fn(base + j)
        return carry

    lax.fori_loop(0, n_rows // SUBLANES, body, 0)


def _dispatch_body(dest_ref, end_ref, padded_ref, h_ref, xs_ref, zeros_ref, sem, zsem, *, tile, bm):
    @pl.when(pl.program_id(0) == 0)
    def _():
        zeros_ref[...] = jnp.zeros_like(zeros_ref)

        def clear(e):
            start = pl.multiple_of(end_ref[e] - bm, bm)
            return pltpu.make_async_copy(zeros_ref, xs_ref.at[pl.ds(start, bm)], zsem)

        for e in range(N_EXPERTS):
            @pl.when(padded_ref[e] > 0)
            def _():
                clear(e).start()

        for e in range(N_EXPERTS):
            @pl.when(padded_ref[e] > 0)
            def _():
                clear(e).wait()

    def copy(t, kk):
        return pltpu.make_async_copy(h_ref.at[pl.ds(t, 1)],
                                     xs_ref.at[pl.ds(dest_ref[0, 0, t * TOP_K + kk], 1)], sem)

    def issue(t):
        for kk in range(TOP_K):
            copy(t, kk).start(priority=kk % 2)

    def drain(t):
        for kk in range(TOP_K):
            copy(t, kk).wait()

    _for_each_row(tile, issue)
    _for_each_row(tile, drain)


def _dispatch(h2, dest, pad_end, padded, n_rows, tile, bm):
    n, d = h2.shape
    assert n % tile == 0
    nt = n // tile
    smem = pl.BlockSpec(memory_space=pltpu.SMEM)
    return pl.pallas_call(
        functools.partial(_dispatch_body, tile=tile, bm=bm),
        grid=(nt,),
        in_specs=[pl.BlockSpec((1, 1, tile * TOP_K), lambda i: (i, 0, 0), memory_space=pltpu.SMEM), smem, smem,
                  pl.BlockSpec((tile, d), lambda i: (i, 0))],
        out_specs=pl.BlockSpec(memory_space=pl.ANY),
        out_shape=jax.ShapeDtypeStruct((n_rows, d), F32),
        scratch_shapes=[pltpu.VMEM((bm, d), F32), pltpu.SemaphoreType.DMA, pltpu.SemaphoreType.DMA],
        compiler_params=_params(("arbitrary",)),
        name="moe_dispatch",
    )(dest.reshape(nt, 1, tile * TOP_K), pad_end.astype(jnp.int32), padded.astype(jnp.int32), h2)


def _expert_body(be_ref, used_ref, xs_ref, wgu_ref, bgu_ref, wd_ref, bd_ref, *rest):
    ys_refs, (wgu16, wd16) = rest[:-2], rest[-2:]
    i = pl.program_id(0)
    de = wd_ref.shape[1]

    @pl.when(jnp.logical_or(i == 0, be_ref[i] != be_ref[jnp.maximum(i - 1, 0)]))
    def _():
        wgu16[...] = wgu_ref[0].astype(BF16)
        wd16[...] = wd_ref[0].astype(BF16)

    @pl.when(i < used_ref[0])
    def _():
        gu = _dot(xs_ref[...].astype(BF16), wgu16[...]) + bgu_ref[0]
        g = jnp.minimum(gu[:, :de], SWIGLU_LIMIT)
        u = jnp.clip(gu[:, de:], -SWIGLU_LIMIT, SWIGLU_LIMIT)
        act = (u + 1.0) * (g * jax.nn.sigmoid(g * SWIGLU_ALPHA))
        y = _dot(act.astype(BF16), wd16[...]) + bd_ref[0]
        cw = ys_refs[0].shape[-1]
        for q, ref in enumerate(ys_refs):
            ref[...] = y[:, q * cw:(q + 1) * cw]

    @pl.when(i >= used_ref[0])
    def _():
        for ref in ys_refs:
            ref[...] = jnp.zeros_like(ref)


def _experts(xs, block_expert, n_used, wgu, bgu, wd, bd, bm):
    n_rows, d = xs.shape
    ne, _, de2 = wgu.shape
    nb = n_rows // bm
    grid_spec = pltpu.PrefetchScalarGridSpec(
        num_scalar_prefetch=2,
        grid=(nb,),
        in_specs=[
            pl.BlockSpec((bm, d), lambda i, be, nu: (i, 0)),
            pl.BlockSpec((1, d, de2), lambda i, be, nu: (be[i], 0, 0)),
            pl.BlockSpec((1, 1, de2), lambda i, be, nu: (be[i], 0, 0)),
            pl.BlockSpec((1, de2 // 2, d), lambda i, be, nu: (be[i], 0, 0)),
            pl.BlockSpec((1, 1, d), lambda i, be, nu: (be[i], 0, 0)),
        ],
        out_specs=[pl.BlockSpec((bm, SC_ROW_SLAB), lambda i, be, nu: (i, 0))] * (d // SC_ROW_SLAB),
        scratch_shapes=[pltpu.VMEM((d, de2), BF16), pltpu.VMEM((de2 // 2, d), BF16)],
    )
    return pl.pallas_call(
        _expert_body,
        grid_spec=grid_spec,
        out_shape=[jax.ShapeDtypeStruct((n_rows, SC_ROW_SLAB), F32)] * (d // SC_ROW_SLAB),
        compiler_params=_params(("arbitrary",)),
        name="moe_experts",
    )(block_expert, n_used, xs, wgu, bgu.reshape(ne, 1, de2), wd, bd.reshape(ne, 1, d))


def _combine_body(dest_ref, next_ref, ys_ref, gate_ref, x2_ref, g_ref, ya_ref, yb_ref, buf, sems, *,
                  tile, tiles_a, steps):
    i = pl.program_id(0)
    slot = i % 2

    def copy(dref, s, t, kk):
        return pltpu.make_async_copy(ys_ref.at[pl.ds(dref[0, 0, t * TOP_K + kk], 1)],
                                     buf.at[s, kk, pl.ds(t, 1)], sems.at[s])

    def issue(dref, s):
        def body(t):
            for kk in range(TOP_K):
                copy(dref, s, t, kk).start(priority=kk % 2)
        _for_each_row(tile, body)

    @pl.when(i == 0)
    def _():
        issue(dest_ref, 0)

    @pl.when(i + 1 < steps)
    def _():
        issue(next_ref, 1 - slot)

    def drain(t):
        for kk in range(TOP_K):
            copy(dest_ref, slot, t, kk).wait()

    _for_each_row(tile, drain)
    gate = gate_ref[...]
    moe = gate[:, 0:1] * buf[slot, 0]
    for kk in range(1, TOP_K):
        moe = moe + gate[:, kk:kk + 1] * buf[slot, kk]
    y = _rmsnorm(x2_ref[...] + moe, g_ref[...])

    @pl.when(i < tiles_a)
    def _():
        ya_ref[...] = y

    @pl.when(i >= tiles_a)
    def _():
        yb_ref[...] = y


def _combine(ys, dest, gate, x2, final_g, tile, n_a):
    n, d = x2.shape
    assert n % tile == 0 and n_a % tile == 0
    nt = n // tile
    tiles_a = n_a // tile
    dest3 = dest.reshape(nt, 1, tile * TOP_K)
    return pl.pallas_call(
        functools.partial(_combine_body, tile=tile, tiles_a=tiles_a, steps=nt),
        grid=(nt,),
        in_specs=[pl.BlockSpec((1, 1, tile * TOP_K), lambda i: (i, 0, 0), memory_space=pltpu.SMEM),
                  pl.BlockSpec((1, 1, tile * TOP_K), lambda i: (jnp.minimum(i + 1, nt - 1), 0, 0),
                               memory_space=pltpu.SMEM),
                  pl.BlockSpec(memory_space=pl.ANY),
                  pl.BlockSpec((tile, TOP_K), lambda i: (i, 0)),
                  pl.BlockSpec((tile, d), lambda i: (i, 0)),
                  pl.BlockSpec((1, d), lambda i: (0, 0))],
        out_specs=[pl.BlockSpec((tile, d), lambda i: (jnp.minimum(i, tiles_a - 1), 0)),
                   pl.BlockSpec((tile, d), lambda i: (jnp.maximum(i - tiles_a, 0), 0))],
        out_shape=[jax.ShapeDtypeStruct((n_a, d), F32), jax.ShapeDtypeStruct((n - n_a, d), F32)],
        scratch_shapes=[pltpu.VMEM((2, TOP_K, tile, d), F32), pltpu.SemaphoreType.DMA((2,))],
        compiler_params=_params(("arbitrary",)),
        name="moe_combine",
    )(dest3, dest3, ys, gate, x2, final_g.reshape(1, d).astype(F32))


def _sc_gather_rows(x, indices, window):
    n = indices.shape[0]
    d = x.shape[1]
    mesh = plsc.VectorSubcoreMesh(core_axis_name="core", subcore_axis_name="subcore")

    @pl.kernel(out_type=jax.ShapeDtypeStruct((n, d), x.dtype), mesh=mesh, name="sc_gather_rows")
    def gather(x_hbm, i_hbm, o_hbm):
        def body(i_vmem, o_vmem):
            pltpu.sync_copy(x_hbm.at[i_vmem.at[0]], o_vmem)

        pltpu.emit_pipeline(
            body,
            grid=(n // window,),
            in_specs=[pl.BlockSpec((1, window), lambda i: (0, i))],
            out_specs=[pl.BlockSpec((window, d), lambda i: (i, 0))],
            core_axis_name=("core", "subcore"),
            dimension_semantics=(pltpu.PARALLEL,),
        )(i_hbm, o_hbm)

    return gather(x, indices.reshape(1, n))


def _combine_stream_body(*refs, tiles_a, n_slabs):
    slab_refs = refs[:n_slabs]
    gate_ref, x2_ref, g_ref, ya_ref, yb_ref = refs[n_slabs:]
    gate = gate_ref[...]
    parts = []
    for ref in slab_refs:
        cw = ref.shape[-1] // TOP_K
        part = gate[:, 0:1] * ref[:, 0:cw]
        for kk in range(1, TOP_K):
            part = part + gate[:, kk:kk + 1] * ref[:, kk * cw:(kk + 1) * cw]
        parts.append(part)
    y = _rmsnorm(x2_ref[...] + jnp.concatenate(parts, axis=1), g_ref[...])
    i = pl.program_id(0)

    @pl.when(i < tiles_a)
    def _():
        ya_ref[...] = y

    @pl.when(i >= tiles_a)
    def _():
        yb_ref[...] = y


def _combine_stream(slabs, gate, x2, final_g, tile, n_a):
    n, d = x2.shape
    assert n % tile == 0 and n_a % tile == 0
    tiles_a = n_a // tile
    return pl.pallas_call(
        functools.partial(_combine_stream_body, tiles_a=tiles_a, n_slabs=len(slabs)),
        grid=(n // tile,),
        in_specs=[pl.BlockSpec((tile, s.shape[1]), lambda i: (i, 0)) for s in slabs] + [
                  pl.BlockSpec((tile, TOP_K), lambda i: (i, 0)),
                  pl.BlockSpec((tile, d), lambda i: (i, 0)),
                  pl.BlockSpec((1, d), lambda i: (0, 0))],
        out_specs=[pl.BlockSpec((tile, d), lambda i: (jnp.minimum(i, tiles_a - 1), 0)),
                   pl.BlockSpec((tile, d), lambda i: (jnp.maximum(i - tiles_a, 0), 0))],
        out_shape=[jax.ShapeDtypeStruct((n_a, d), F32), jax.ShapeDtypeStruct((n - n_a, d), F32)],
        compiler_params=_params(("arbitrary",)),
        name="moe_combine",
    )(*slabs, gate, x2, final_g.reshape(1, d).astype(F32))


def _moe(h2, x2, idx, gate, rank, counts, lp, final_g, n_a):
    n, d = x2.shape
    bm = EXPERT_ROWS
    counts = counts.reshape(N_EXPERTS)
    padded = (counts + bm - 1) // bm * bm
    pad_end = jnp.cumsum(padded)
    dest = ((pad_end - padded)[idx] + rank).reshape(n * TOP_K).astype(jnp.int32)
    n_blocks = -(-(n * TOP_K) // bm) + N_EXPERTS
    block_start = jnp.arange(n_blocks, dtype=jnp.int32) * bm
    block_expert = jnp.minimum(jnp.sum(pad_end[None, :] <= block_start[:, None], axis=1),
                               N_EXPERTS - 1).astype(jnp.int32)
    n_used = (pad_end[-1:] // bm).astype(jnp.int32)
    xs = _dispatch(h2, dest, pad_end, padded, n_blocks * bm, math.gcd(n, DISPATCH_TILE), bm)
    ys = _experts(xs, block_expert, n_used, lp['w_gate_up'].astype(F32), lp['b_gate_up'].astype(F32),
                  lp['w_down'].astype(F32), lp['b_down'].astype(F32), bm)
    slabs = [_sc_gather_rows(y_slab, dest, SC_GATHER_WINDOW).reshape(n, TOP_K * SC_ROW_SLAB) for y_slab in ys]
    return _combine_stream(slabs, gate, x2, final_g, math.gcd(math.gcd(n, n_a), COMBINE_TILE), n_a)


def _to_heads(t, b, s):
    return t.reshape(b, s, SB_HEADS, SB_HEAD_DIM).transpose(0, 2, 1, 3)


def _from_heads(t):
    b, h, s, dh = t.shape
    return t.transpose(0, 2, 1, 3).reshape(b, s, h * dh)


def _layer(x_p, x_s, cache_k, cache_v, state_rwkv, state_shift, lp, final_g):
    bp, tp, d = x_p.shape
    bs, ts, _ = x_s.shape
    sbw = SB_HEADS * SB_HEAD_DIM
    rw_proj = lp['rw_mu'].shape[0]
    n_p = bp * tp
    n = n_p + bs * ts
    x_s = x_s.reshape(bs * ts, d)
    scale = LOG2E * SB_HEAD_DIM ** -0.5
    qkv16, kp, vp, kv_new, rw, gab = _inproj(x_p, x_s, lp['norm1_g'].astype(F32), lp['w_in'].astype(BF16),
                                             rw_proj, scale, ROW_TILE)
    pairs = SB_HEADS // 2
    oa = _stick_breaking(qkv16, 0, 0, qkv16, pairs, 2 * pairs, bp, tp, 0, SB_BLOCK, SB_SUB,
                         jnp.zeros((n, sbw), BF16))
    past = cache_k.shape[2]
    new16 = qkv16[n_p:].reshape(bs, ts, 3 * sbw)
    kv_s = jnp.concatenate([
        jnp.concatenate([_from_heads(cache_k).astype(BF16), new16[:, :, sbw:2 * sbw]], axis=1),
        jnp.concatenate([_from_heads(cache_v).astype(BF16), new16[:, :, 2 * sbw:]], axis=1)], axis=2)
    oa = _stick_breaking(qkv16, n_p, 0, kv_s.reshape(bs * (past + ts), 2 * sbw), 0, pairs, bs, ts, past, ts,
                         min(SB_SUB, past), oa)
    ks, vs = kv_new[:, :sbw], kv_new[:, sbw:]

    s0_p = jnp.zeros((bp, RW_HEADS, RW_HEAD_DIM, RW_HEAD_DIM), F32)
    prev_p = jnp.zeros((bp, 1, rw_proj), F32)
    ob, st_p = _rwkv(rw, bp, tp, tp, prev_p, s0_p, lp, jnp.zeros((n, RW_WIDTH), BF16))
    rw_s = rw[n_p:].reshape(bs, ts, rw_proj)
    ts_pad = -(-ts // RW_CHUNK) * RW_CHUNK
    rw_s_pad = jnp.pad(rw_s, ((0, 0), (0, ts_pad - ts), (0, 0))).reshape(bs * ts_pad, rw_proj)
    ob_s, st_s = _rwkv(rw_s_pad, bs, ts_pad, ts, state_shift, state_rwkv, lp,
                       jnp.zeros((bs * ts_pad, RW_WIDTH), BF16))
    ob = lax.dynamic_update_slice(
        ob, ob_s.reshape(bs, ts_pad, RW_WIDTH)[:, :ts].reshape(bs * ts, RW_WIDTH), (n_p, 0))
    sh_p = jnp.stack([rw[(b + 1) * tp - 1:(b + 1) * tp] for b in range(bp)])
    sh_s = rw_s[:, ts - 1:]

    x2, h2, idx, gate, rank, counts = _merge_route(oa, ob, gab, x_p.reshape(n_p, d), x_s, lp, MERGE_TILE)
    y = _moe(h2, x2, idx, gate, rank, counts, lp, final_g, n_p)
    return (y, kp, vp, st_p, sh_p, _to_heads(ks, bs, ts), _to_heads(vs, bs, ts), st_s, sh_s)


def kernel(x_prompt, x_sample, cache_sb_k, cache_sb_v, state_rwkv, state_shift, norm1_g, w_in, rw_mu, rw_w0, rw_w2, rw_a0, rw_a2, rw_g2, rw_k_k, rw_k_a, rw_r_k, rw_ln_g, rw_ln_b, w_branch_a, w_branch_b, w_out, norm2_g, w_router, b_router, w_gate_up, b_gate_up, w_down, b_down, final_norm_g):
    depth = w_in.shape[0]
    assert depth == 1, "the final RMSNorm is fused into the last (only) layer"
    bp, tp, d = x_prompt.shape
    bs, ts, _ = x_sample.shape
    lp = dict(norm1_g=norm1_g[0], w_in=w_in[0], rw_mu=rw_mu[0], rw_w0=rw_w0[0], rw_w2=rw_w2[0], rw_a0=rw_a0[0],
              rw_a2=rw_a2[0], rw_g2=rw_g2[0], rw_k_k=rw_k_k[0], rw_k_a=rw_k_a[0], rw_r_k=rw_r_k[0].reshape(-1),
              rw_ln_g=rw_ln_g[0], rw_ln_b=rw_ln_b[0], w_branch_a=w_branch_a[0], w_branch_b=w_branch_b[0],
              w_out=w_out[0], norm2_g=norm2_g[0], w_router=w_router[0], b_router=b_router[0],
              w_gate_up=w_gate_up[0], b_gate_up=b_gate_up[0], w_down=w_down[0], b_down=b_down[0])
    y, kp, vp, st_p, sh_p, ks, vs, st_s, sh_s = _layer(
        x_prompt, x_sample, cache_sb_k[0], cache_sb_v[0], state_rwkv[0], state_shift[0], lp, final_norm_g)
    return (y[0].reshape(bp, tp, d), y[1].reshape(bs, ts, d),
            kp[None], vp[None], st_p[None], sh_p[None], ks[None], vs[None], st_s[None], sh_s[None])
```

```python
import functools
import math

import jax
import jax.numpy as jnp
from jax import lax
from jax.experimental import pallas as pl
from jax.experimental.pallas import tpu as pltpu
from jax.experimental.pallas import tpu_sc as plsc

F32 = jnp.float32
BF16 = jnp.bfloat16

SB_HEADS = 8
SB_HEAD_DIM = 64
RW_HEADS = 8
RW_HEAD_DIM = 64
RW_WIDTH = RW_HEADS * RW_HEAD_DIM
RW_DECAY_LORA = 64
RW_AAA_LORA = 64
RW_GATE_LORA = 128
RW_GN_EPS = 64e-5
N_EXPERTS = 32
TOP_K = 4
SWIGLU_LIMIT = 7.0
SWIGLU_ALPHA = 1.702
RMS_EPS = 1e-6

VMEM_LIMIT_BYTES = 56 * 1024 * 1024
ROW_TILE = 256
MERGE_TILE = 512
SB_BLOCK = 512
SB_SUB = 256
LOG2E = 1.4426950408889634
SB_DEAD_LOG2 = -200.0
RW_CHUNK = 64
RW_PREP_CHUNKS = 4
RW_SCAN_CHUNKS = 4
EXPERT_ROWS = 512
DISPATCH_TILE = 2560
COMBINE_TILE = 256
SUBLANES = 8
SC_GATHER_WINDOW = 128
SC_ROW_SLAB = 256


def _params(semantics):
    return pltpu.CompilerParams(dimension_semantics=semantics, vmem_limit_bytes=VMEM_LIMIT_BYTES)


def _dot(a, b):
    return jnp.dot(a, b, preferred_element_type=F32)


def _dot_nt(a, b):
    return lax.dot_general(a, b, (((1,), (1,)), ((), ())), preferred_element_type=F32)


def _softplus_parts(z):
    l = jnp.log(1.0 + jnp.exp(-jnp.abs(z)))
    sp = jnp.maximum(z, 0.0) + l
    return sp, sp - z


def _rmsnorm(x, g):
    return x * lax.rsqrt(jnp.mean(x * x, axis=-1, keepdims=True) + RMS_EPS) * g


def _inproj_body(xa_ref, xb_ref, g_ref, w_ref, qkv16_ref, ka_ref, va_ref, kvb_ref, rw_ref, gab_ref, *,
                 q_scale, tiles_a):
    i = pl.program_id(0)
    x = jnp.where(i < tiles_a, xa_ref[...], xb_ref[...])
    h = _rmsnorm(x, g_ref[...]).astype(BF16)
    sbw = kvb_ref.shape[-1] // 2
    dh = ka_ref.shape[-1]
    q = _dot(h, w_ref[:, 0:sbw])
    qkv16_ref[:, 0:sbw] = (q * q_scale).astype(BF16)
    k = _dot(h, w_ref[:, sbw:2 * sbw])
    v = _dot(h, w_ref[:, 2 * sbw:3 * sbw])
    qkv16_ref[:, sbw:2 * sbw] = k.astype(BF16)
    qkv16_ref[:, 2 * sbw:3 * sbw] = v.astype(BF16)
    off = 3 * sbw
    for ref in (rw_ref, gab_ref):
        width = ref.shape[-1]
        for c0 in range(0, width, sbw):
            cw = min(sbw, width - c0)
            ref[:, c0:c0 + cw] = _dot(h, w_ref[:, off + c0:off + c0 + cw])
        off += width

    @pl.when(i < tiles_a)
    def _():
        for hd in range(sbw // dh):
            ka_ref[0, hd] = k[:, hd * dh:(hd + 1) * dh]
            va_ref[0, hd] = v[:, hd * dh:(hd + 1) * dh]

    @pl.when(i >= tiles_a)
    def _():
        kvb_ref[:, 0:sbw] = k
        kvb_ref[:, sbw:2 * sbw] = v


def _inproj(xa, xb, g, w_bf16, rw_proj, q_scale, tm):
    ba, ta, d = xa.shape
    nb = xb.shape[0]
    n = ba * ta + nb
    sbw = SB_HEADS * SB_HEAD_DIM
    total = w_bf16.shape[1]
    assert total == 3 * sbw + rw_proj + 2 * d and ta % tm == 0 and nb % tm == 0
    tiles_a = ba * ta // tm
    per_seq = ta // tm
    last_a = tiles_a - 1
    row = lambda wd: pl.BlockSpec((tm, wd), lambda i: (i, 0))
    head_major = pl.BlockSpec(
        (1, SB_HEADS, tm, SB_HEAD_DIM),
        lambda i: (jnp.minimum(i, last_a) // per_seq, 0, jnp.minimum(i, last_a) % per_seq, 0))
    hm_shape = jax.ShapeDtypeStruct((ba, SB_HEADS, ta, SB_HEAD_DIM), F32)
    return pl.pallas_call(
        functools.partial(_inproj_body, q_scale=q_scale, tiles_a=tiles_a),
        grid=(n // tm,),
        in_specs=[
            pl.BlockSpec((tm, d), lambda i: (jnp.minimum(i, last_a), 0)),
            pl.BlockSpec((tm, d), lambda i: (jnp.maximum(i - tiles_a, 0), 0)),
            pl.BlockSpec((1, d), lambda i: (0, 0)),
            pl.BlockSpec((d, total), lambda i: (0, 0), pipeline_mode=pl.Buffered(1)),
        ],
        out_specs=[row(3 * sbw), head_major, head_major,
                   pl.BlockSpec((tm, 2 * sbw), lambda i: (jnp.maximum(i - tiles_a, 0), 0)),
                   row(rw_proj), row(2 * d)],
        out_shape=[jax.ShapeDtypeStruct((n, 3 * sbw), BF16), hm_shape, hm_shape,
                   jax.ShapeDtypeStruct((nb, 2 * sbw), F32),
                   jax.ShapeDtypeStruct((n, rw_proj), F32), jax.ShapeDtypeStruct((n, 2 * d), F32)],
        compiler_params=_params(("arbitrary",)),
        name="inproj",
    )(xa.reshape(ba * ta, d), xb, g.reshape(1, d), w_bf16)


def _sb_tile(q, kb, vb, ntri, carry, mask, sb):
    tk = kb.shape[0]
    z = _dot_nt(q, kb)
    neg_abs = lax.bitcast_convert_type(lax.bitcast_convert_type(z, jnp.uint32) | jnp.uint32(0x80000000), F32)
    sp = jnp.maximum(z, 0.0) + jnp.log(1.0 + jnp.exp2(neg_abs)) * LOG2E
    if mask is not None:
        sp = jnp.where(mask, sp, 0.0)
    hi = sp.astype(BF16)
    lo = (sp - hi.astype(F32)).astype(BF16)
    log_beta = z - sp
    wts = []
    for j in range(tk // sb - 1, -1, -1):
        cols = slice(j * sb, (j + 1) * sb)
        later = _dot(hi[:, cols], ntri) + _dot(lo[:, cols], ntri) + carry
        w = jnp.exp2(log_beta[:, cols] + later)
        if mask is not None:
            w = jnp.where(mask[:, cols], w, 0.0)
        wts.append(w.astype(BF16))
        carry = carry - jnp.sum(sp[:, cols], axis=-1, keepdims=True)
    wts = wts[0] if len(wts) == 1 else jnp.concatenate(wts[::-1], axis=1)
    return _dot(wts, vb), carry


def _sb_body(q_ref, k_ref, v_ref, triq_ref, trik_ref, o_in_ref, o_ref, *, tq, tk, past):
    del o_in_ref
    qi = pl.program_id(2)
    q = q_ref[...]
    first = lax.broadcasted_iota(jnp.int32, q.shape, 1) < SB_HEAD_DIM
    zero = jnp.zeros_like(q)
    qs = (jnp.where(first, q, zero), jnp.where(first, zero, q))
    d0 = pl.multiple_of(past + qi * tq, tq)
    row = lax.broadcasted_iota(jnp.int32, (tq, tq), 0)
    col = lax.broadcasted_iota(jnp.int32, (tq, tq), 1)
    kd = k_ref[pl.ds(d0, tq), :]
    vd = v_ref[pl.ds(d0, tq), :]
    sbq = triq_ref.shape[0]
    state = []
    for qh in qs:
        state.extend(_sb_tile(qh, kd, vd, triq_ref[...], jnp.zeros((tq, 1), F32), col < row, sbq))
    nb = (past + qi * tq) // tk
    sbk = trik_ref.shape[0]

    def live(state):
        return jnp.max(jnp.maximum(state[1], state[3])) > SB_DEAD_LOG2

    def cond(loop):
        i, alive, _ = loop
        return jnp.logical_and(i < nb, alive)

    def step(loop):
        i, _, state = loop
        s0 = pl.multiple_of((nb - 1 - i) * tk, tk)
        kb = k_ref[pl.ds(s0, tk), :]
        vb = v_ref[pl.ds(s0, tk), :]
        new = []
        for h, qh in enumerate(qs):
            out, carry = _sb_tile(qh, kb, vb, trik_ref[...], state[2 * h + 1], None, sbk)
            new.extend((state[2 * h] + out, carry))
        return i + 1, live(new), tuple(new)

    _, _, state = lax.while_loop(cond, step, (jnp.int32(0), live(state), tuple(state)))
    o_ref[...] = jnp.where(first, state[0], state[2]).astype(o_ref.dtype)


def _later_matrix(n):
    r = lax.broadcasted_iota(jnp.int32, (n, n), 0)
    c = lax.broadcasted_iota(jnp.int32, (n, n), 1)
    return jnp.where(r > c, -1.0, 0.0).astype(BF16)


def _stick_breaking(q, q_row0, q_lane0, kv, k_lane0, v_lane0, batch, t, past, tq, tk, o_prev):
    s = past + t
    pair = 2 * SB_HEAD_DIM
    sbq = min(tq, SB_SUB)
    sbk = min(tk, SB_SUB)
    nq = t // tq
    qblk = q_row0 // tq
    q_spec = lambda lane0: pl.BlockSpec((tq, pair), lambda b, h, i: (qblk + b * nq + i, lane0 + h))
    kv_spec = lambda lane0: pl.BlockSpec((s, pair), lambda b, h, i: (b, lane0 + h))
    return pl.pallas_call(
        functools.partial(_sb_body, tq=tq, tk=tk, past=past),
        grid=(batch, SB_HEADS // 2, nq),
        in_specs=[
            q_spec(q_lane0), kv_spec(k_lane0), kv_spec(v_lane0),
            pl.BlockSpec((sbq, sbq), lambda b, h, i: (0, 0)),
            pl.BlockSpec((sbk, sbk), lambda b, h, i: (0, 0)),
            pl.BlockSpec(memory_space=pl.ANY),
        ],
        out_specs=q_spec(0),
        out_shape=jax.ShapeDtypeStruct(o_prev.shape, BF16),
        input_output_aliases={5: 0},
        compiler_params=_params(("parallel", "parallel", "arbitrary")),
        name="stick_breaking",
    )(q, kv, kv, _later_matrix(sbq), _later_matrix(sbk), o_prev)


def _split_bf16(x):
    hi = x.astype(BF16)
    return hi, (x - hi.astype(F32)).astype(BF16)


def _dot3(a, b, dot=_dot):
    ah, al = _split_bf16(a)
    bh, bl = _split_bf16(b)
    return dot(ah, bh) + dot(ah, bl) + dot(al, bh)


def _dot_ones_rhs(x, ones_bf16):
    hi, lo = _split_bf16(x)
    return _dot(hi, ones_bf16) + _dot(lo, ones_bf16)


def _dot_ones_lhs(ones_bf16, x):
    hi, lo = _split_bf16(x)
    return _dot(ones_bf16, hi) + _dot(ones_bf16, lo)


HEADS_PER_MATMUL = 4
GROUP_WIDTH = HEADS_PER_MATMUL * RW_HEAD_DIM


def _block_diag(y, mask):
    return jnp.concatenate([y] * HEADS_PER_MATMUL, axis=0) * mask


def _bdmm(xs, ys, mask, dot=_dot, extended=False):
    xh, xl = xs
    yh, yl = ys
    m = xh.shape[0]
    x2 = jnp.concatenate([xh, xl], axis=0) if extended else xh
    outs = []
    for g in range(RW_WIDTH // GROUP_WIDTH):
        ls = slice(g * GROUP_WIDTH, (g + 1) * GROUP_WIDTH)
        out = dot(x2[:, ls], _block_diag(yh[:, ls], mask))
        if extended:
            out = out[:m] + out[m:] + dot(xh[:, ls], _block_diag(yl[:, ls], mask))
        outs.append(out)
    return jnp.concatenate(outs, axis=1)


def _bd_tn(x, y, mask):
    n = RW_HEAD_DIM
    outs = []
    for g in range(RW_WIDTH // GROUP_WIDTH):
        ls = slice(g * GROUP_WIDTH, (g + 1) * GROUP_WIDTH)
        z = _dot(x[:, ls].T.astype(BF16), y[:, ls].astype(BF16)) * mask
        outs.append(z[0:n] + z[n:2 * n] + z[2 * n:3 * n] + z[3 * n:4 * n])
    return jnp.concatenate(outs, axis=1)


def _rwkv_prep_body(p_ref, pb_ref, prev_ref, mu_ref, w0_ref, w2_ref, a0_ref, a2_ref, g2_ref, kk_ref, ka_ref,
                    rk_ref, tri_ref, bd_ref, mbf_ref, mf32_ref,
                    rhat_ref, y0_ref, a_ref, d_ref, bonus_ref, g_ref, *, chunk, groups, n_sq, t_valid):
    n = RW_HEAD_DIM
    w_ = RW_WIDTH
    p = p_ref[...]
    rows = lax.broadcasted_iota(jnp.int32, p.shape, 0)
    prev_row = jnp.where(pl.program_id(1) == 0, prev_ref[0], pb_ref[7:8, :])
    p_shift = jnp.where(rows == 0, prev_row, pltpu.roll(p, 1, 0))

    xs = p + (p_shift - p) * mu_ref[...]
    r = xs[:, 0:w_]
    k = xs[:, w_:2 * w_]
    v = xs[:, 2 * w_:3 * w_]
    o1 = 3 * w_
    dw = xs[:, o1:o1 + RW_DECAY_LORA]
    da = xs[:, o1 + RW_DECAY_LORA:o1 + RW_DECAY_LORA + RW_AAA_LORA]
    dg = xs[:, o1 + RW_DECAY_LORA + RW_AAA_LORA:]

    sp_neg = _softplus_parts(w0_ref[...] + _dot3(jnp.tanh(dw), w2_ref[...]))[1]
    log_decay = -jnp.exp(-sp_neg - 0.5)
    a = jax.nn.sigmoid(a0_ref[...] + _dot3(da, a2_ref[...]))
    g_ref[...] = _dot(jax.nn.sigmoid(dg).astype(BF16), g2_ref[...])

    bd = bd_ref[...]
    kk = k * kk_ref[...]
    kk = kk / jnp.maximum(jnp.sqrt(_dot_ones_rhs(kk * kk, bd)), 1e-12)
    k2 = k * (1.0 + (a - 1.0) * ka_ref[...])
    a_vec = -kk
    b_vec = kk * a
    bonus_ref[...] = _dot((r * k2 * rk_ref[...]).astype(BF16), bd) * v
    if t_valid < chunk:
        live = lax.broadcasted_iota(jnp.int32, r.shape, 0) < t_valid
        keep = lambda x: jnp.where(live, x, 0.0)
        log_decay, r, k2, v, a_vec, b_vec = (keep(x) for x in (log_decay, r, k2, v, a_vec, b_vec))

    cw = _dot_ones_lhs(tri_ref[...], log_decay)
    cw_last = jnp.concatenate(
        [jnp.broadcast_to(cw[(j + 1) * chunk - 1:(j + 1) * chunk, :], (chunk, w_)) for j in range(groups)], axis=0)
    r_t = r * jnp.exp(cw)
    a_t = a_vec * jnp.exp(cw - log_decay)
    e_neg = jnp.exp(-cw)
    b_t = b_vec * e_neg
    k_t = k2 * e_neg
    e_rel = jnp.exp(cw_last - cw)
    b_h = b_vec * e_rel
    k_h = k2 * e_rel
    w_c = jnp.exp(cw_last)

    rr = lax.broadcasted_iota(jnp.int32, (chunk, w_), 0)
    ss = lax.broadcasted_iota(jnp.int32, (chunk, w_), 1) % n
    strict = ss < rr
    incl = ss <= rr
    eye = (ss == rr).astype(F32)
    m_bf16 = mbf_ref[...]
    m_f32 = mf32_ref[...]

    for j in range(groups):
        rs = slice(j * chunk, (j + 1) * chunk)
        v_j = v[rs]
        v_s = _split_bf16(v_j)
        ar_s = _split_bf16(jnp.concatenate([a_t[rs], r_t[rs]], axis=0))
        gram_b = _bdmm(ar_s, _split_bf16(b_t[rs]), m_bf16, _dot_nt)
        gram_k = _bdmm(ar_s, _split_bf16(k_t[rs]), m_bf16, _dot_nt)
        l_ab = jnp.where(strict, gram_b[:chunk], 0.0)
        l_ak = jnp.where(strict, gram_k[:chunk], 0.0)
        m_rb_s = _split_bf16(jnp.where(incl, gram_b[chunk:], 0.0))
        m_rk_s = _split_bf16(jnp.where(incl, gram_k[chunk:], 0.0))
        l_s = _split_bf16(l_ab)
        t_inv = eye + l_ab
        l_pow = _bdmm(l_s, l_s, m_bf16)
        for i in range(n_sq):
            lp_s = _split_bf16(l_pow)
            t_inv = t_inv + _bdmm(lp_s, _split_bf16(t_inv), m_bf16)
            if i + 1 < n_sq:
                l_pow = _bdmm(lp_s, lp_s, m_bf16)
        t_s = _split_bf16(t_inv)
        p_mat = _bdmm(t_s, _split_bf16(a_t[rs]), m_bf16)
        q_mat = _bdmm(t_s, _split_bf16(_bdmm(_split_bf16(l_ak), v_s, m_bf16)), m_bf16)
        rhat_ref[rs, :] = r_t[rs] + _bdmm(m_rb_s, _split_bf16(p_mat), m_bf16)
        y0_ref[rs, :] = _bdmm(m_rb_s, _split_bf16(q_mat), m_bf16) + _bdmm(m_rk_s, v_s, m_bf16)
        ns = slice(j * n, (j + 1) * n)
        a_ref[ns, :] = _bd_tn(p_mat, b_h[rs], m_f32) + eye * w_c[j * chunk:j * chunk + 1, :]
        d_ref[ns, :] = _bd_tn(jnp.concatenate([q_mat, v_j], axis=0),
                              jnp.concatenate([b_h[rs], k_h[rs]], axis=0), m_f32)


def _rwkv_scan_body(rhat_ref, y0_ref, a_ref, d_ref, bonus_ref, g_ref, s0_ref, lng_ref, lnb_ref, bd_ref, mbf_ref,
                    o_ref, s_ref, s_scr, y_scr, *, chunk, groups, steps):
    n = RW_HEAD_DIM
    c = pl.program_id(1)

    @pl.when(c == 0)
    def _():
        for h in range(RW_HEADS):
            s_scr[:, h * n:(h + 1) * n] = s0_ref[0, h]

    mask = mbf_ref[...]
    s = s_scr[...]
    for j in range(groups):
        rs = slice(j * chunk, (j + 1) * chunk)
        ns = slice(j * n, (j + 1) * n)
        s_s = _split_bf16(s)
        y_scr[rs, :] = y0_ref[rs, :] + _bdmm(_split_bf16(rhat_ref[rs, :]), s_s, mask, _dot_nt, extended=True)
        s = _bdmm(s_s, _split_bf16(a_ref[ns, :]), mask, extended=True) + d_ref[ns, :]
    s_scr[...] = s

    @pl.when(c == steps - 1)
    def _():
        for h in range(RW_HEADS):
            s_ref[0, h] = s[:, h * n:(h + 1) * n]

    bd = bd_ref[...]
    y = y_scr[...]
    mean = _dot_ones_rhs(y, bd) * (1.0 / n)
    yc = y - mean
    var = _dot_ones_rhs(yc * yc, bd) * (1.0 / n)
    yn = yc * lax.rsqrt(var + RW_GN_EPS) * lng_ref[...] + lnb_ref[...]
    o_ref[...] = ((yn + bonus_ref[...]) * g_ref[...]).astype(o_ref.dtype)


def _rwkv(rw_all, batch, t, t_valid, prev, s0, lp, o_prev):
    proj = rw_all.shape[1]
    n_all = o_prev.shape[0]
    row_off = 0
    n = RW_HEAD_DIM
    chunk = RW_CHUNK
    assert chunk == n and t % chunk == 0 and (t_valid == t or t == chunk)
    n_sq = chunk.bit_length() - 2
    nc = t // chunk
    gp = min(RW_PREP_CHUNKS, nc)
    gs = min(RW_SCAN_CHUNKS, nc)
    rows_p = gp * chunk
    rows_s = gs * chunk
    n_tok = batch * t
    row = lambda x: x.reshape(1, -1).astype(F32)
    r = lax.broadcasted_iota(jnp.int32, (rows_p, rows_p), 0)
    c = lax.broadcasted_iota(jnp.int32, (rows_p, rows_p), 1)
    tri = ((c <= r) & (c // chunk == r // chunk)).astype(BF16)
    hr = lax.broadcasted_iota(jnp.int32, (RW_WIDTH, RW_WIDTH), 0) // n
    hc = lax.broadcasted_iota(jnp.int32, (RW_WIDTH, RW_WIDTH), 1) // n
    bd = (hr == hc).astype(BF16)
    group_mask = (hr == hc)[:GROUP_WIDTH, :GROUP_WIDTH]
    const = lambda shape: pl.BlockSpec(shape, lambda b, c: (0,) * len(shape))

    blk_p = row_off // rows_p
    prep_args = [rw_all, rw_all, prev.astype(F32), row(lp['rw_mu']), row(lp['rw_w0']), lp['rw_w2'].astype(F32),
                 row(lp['rw_a0']), lp['rw_a2'].astype(F32), lp['rw_g2'].astype(BF16), row(lp['rw_k_k']),
                 row(lp['rw_k_a']), row(lp['rw_r_k']), tri, bd, group_mask.astype(BF16), group_mask.astype(F32)]
    tok_spec = lambda rows: pl.BlockSpec((rows, RW_WIDTH), lambda b, c: (b * (t // rows) + c, 0))
    mat_spec = lambda g: pl.BlockSpec((g * n, RW_WIDTH), lambda b, c: (b * (nc // g) + c, 0))
    tok_shape = jax.ShapeDtypeStruct((n_tok, RW_WIDTH), F32)
    mat_shape = jax.ShapeDtypeStruct((batch * nc * n, RW_WIDTH), F32)
    rhat, y0, a_mat, d_mat, bonus, g = pl.pallas_call(
        functools.partial(_rwkv_prep_body, chunk=chunk, groups=gp, n_sq=n_sq, t_valid=t_valid),
        grid=(batch, nc // gp),
        in_specs=[
            pl.BlockSpec((rows_p, proj), lambda b, c: (blk_p + b * (nc // gp) + c, 0)),
            pl.BlockSpec((8, proj), lambda b, c: (jnp.maximum((row_off + b * t + c * rows_p) // 8 - 1, 0), 0)),
            pl.BlockSpec((1, 1, proj), lambda b, c: (b, 0, 0)),
        ] + [const(x.shape) for x in prep_args[3:]],
        out_specs=[tok_spec(rows_p), tok_spec(rows_p), mat_spec(gp), mat_spec(gp), tok_spec(rows_p),
                   tok_spec(rows_p)],
        out_shape=[tok_shape, tok_shape, mat_shape, mat_shape, tok_shape, tok_shape],
        compiler_params=_params(("parallel", "parallel")),
        name="rwkv7_prep",
    )(*prep_args)

    blk_s = row_off // rows_s
    scan_args = [rhat, y0, a_mat, d_mat, bonus, g, s0.astype(F32), row(lp['rw_ln_g']), row(lp['rw_ln_b']), bd,
                 group_mask.astype(BF16), o_prev]
    state_spec = pl.BlockSpec((1, RW_HEADS, n, n), lambda b, c: (b, 0, 0, 0))
    n_in = len(scan_args) - 1

    def scan_body(*refs):
        _rwkv_scan_body(*refs[:n_in], *refs[n_in + 1:], chunk=chunk, groups=gs, steps=nc // gs)

    o, state = pl.pallas_call(
        scan_body,
        grid=(batch, nc // gs),
        in_specs=[tok_spec(rows_s), tok_spec(rows_s), mat_spec(gs), mat_spec(gs), tok_spec(rows_s),
                  tok_spec(rows_s), state_spec, const((1, RW_WIDTH)), const((1, RW_WIDTH)), const(bd.shape),
                  const(group_mask.shape), pl.BlockSpec(memory_space=pl.ANY)],
        out_specs=[pl.BlockSpec((rows_s, RW_WIDTH), lambda b, c: (blk_s + b * (nc // gs) + c, 0)), state_spec],
        out_shape=[jax.ShapeDtypeStruct((n_all, RW_WIDTH), BF16),
                   jax.ShapeDtypeStruct((batch, RW_HEADS, n, n), F32)],
        scratch_shapes=[pltpu.VMEM((n, RW_WIDTH), F32), pltpu.VMEM((rows_s, RW_WIDTH), F32)],
        input_output_aliases={n_in: 0},
        compiler_params=_params(("parallel", "arbitrary")),
        name="rwkv7_scan",
    )(*scan_args)
    return o, state


def _merge_body(oa_ref, ob_ref, gab_ref, xa_ref, xb_ref, wa_ref, wb_ref, wo_ref, g2_ref, wr_ref, br_ref, tri_ref,
                x2_ref, h2_ref, idx_ref, gate_ref, rank_ref, cnt_ref, carry_scr, *, tiles_a):
    i = pl.program_id(0)
    d = xa_ref.shape[-1]

    @pl.when(i == 0)
    def _():
        carry_scr[...] = jnp.zeros_like(carry_scr)

    gab = gab_ref[...]
    merged = (jax.nn.sigmoid(gab[:, :d]) * _dot(oa_ref[...], wa_ref[...])
              + jax.nn.sigmoid(gab[:, d:]) * _dot(ob_ref[...], wb_ref[...]))
    x = jnp.where(i < tiles_a, xa_ref[...], xb_ref[...])
    x2 = x + _dot(merged.astype(BF16), wo_ref[...])
    x2_ref[...] = x2
    h2 = _rmsnorm(x2, g2_ref[...])
    h2_ref[...] = h2
    logits = _dot3(h2, wr_ref[...]) + br_ref[...]

    tm, ne = logits.shape
    col = lax.broadcasted_iota(jnp.int32, (tm, ne), 1)
    c4 = lax.broadcasted_iota(jnp.int32, (tm, TOP_K), 1)
    work = logits
    tops, idxs = [], []
    for _ in range(TOP_K):
        m = jnp.max(work, axis=-1, keepdims=True)
        ix = jnp.min(jnp.where(work == m, col, ne), axis=-1, keepdims=True)
        tops.append(m)
        idxs.append(ix)
        work = jnp.where(col == ix, -jnp.inf, work)
    es = [jnp.exp(m - tops[0]) for m in tops]
    denom = es[0] + es[1] + es[2] + es[3]
    onehot = jnp.zeros((tm, ne), F32)
    for ix in idxs:
        onehot = onehot + (col == ix).astype(F32)
    before = carry_scr[...] + _dot(tri_ref[...], onehot.astype(BF16))
    idx_out = jnp.zeros((tm, TOP_K), jnp.int32)
    gate_out = jnp.zeros((tm, TOP_K), F32)
    rank_out = jnp.zeros((tm, TOP_K), jnp.int32)
    for kk in range(TOP_K):
        rk = jnp.sum(jnp.where(col == idxs[kk], before, 0.0), axis=-1, keepdims=True)
        idx_out = jnp.where(c4 == kk, idxs[kk], idx_out)
        gate_out = jnp.where(c4 == kk, es[kk] / denom, gate_out)
        rank_out = jnp.where(c4 == kk, rk.astype(jnp.int32), rank_out)
    idx_ref[...] = idx_out
    gate_ref[...] = gate_out
    rank_ref[...] = rank_out
    carry = carry_scr[...] + jnp.sum(onehot, axis=0, keepdims=True)
    carry_scr[...] = carry
    cnt_ref[...] = carry.astype(jnp.int32)


def _merge_route(oa, ob, gab, xa, xb, lp, tm):
    d = xa.shape[1]
    n = xa.shape[0] + xb.shape[0]
    assert xa.shape[0] % tm == 0 and xb.shape[0] % tm == 0
    tiles_a = xa.shape[0] // tm
    ne = N_EXPERTS
    r = lax.broadcasted_iota(jnp.int32, (tm, tm), 0)
    c = lax.broadcasted_iota(jnp.int32, (tm, tm), 1)
    tri = (c < r).astype(BF16)
    const = lambda shape: pl.BlockSpec(shape, lambda i: (0,) * len(shape))
    rowblk = lambda wd: pl.BlockSpec((tm, wd), lambda i: (i, 0))
    wa = lp['w_branch_a'].astype(BF16)
    wb = lp['w_branch_b'].astype(BF16)
    wo = lp['w_out'].astype(BF16)
    return pl.pallas_call(
        functools.partial(_merge_body, tiles_a=tiles_a),
        grid=(n // tm,),
        in_specs=[rowblk(oa.shape[1]), rowblk(ob.shape[1]), rowblk(2 * d),
                  pl.BlockSpec((tm, d), lambda i: (jnp.minimum(i, tiles_a - 1), 0)),
                  pl.BlockSpec((tm, d), lambda i: (jnp.maximum(i - tiles_a, 0), 0)),
                  const(wa.shape), const(wb.shape), const(wo.shape), const((1, d)), const((d, ne)),
                  const((1, ne)), const((tm, tm))],
        out_specs=[rowblk(d), rowblk(d), rowblk(TOP_K), rowblk(TOP_K), rowblk(TOP_K), const((1, ne))],
        out_shape=[jax.ShapeDtypeStruct((n, d), F32), jax.ShapeDtypeStruct((n, d), F32),
                   jax.ShapeDtypeStruct((n, TOP_K), jnp.int32), jax.ShapeDtypeStruct((n, TOP_K), F32),
                   jax.ShapeDtypeStruct((n, TOP_K), jnp.int32), jax.ShapeDtypeStruct((1, ne), jnp.int32)],
        scratch_shapes=[pltpu.VMEM((1, ne), F32)],
        compiler_params=_params(("arbitrary",)),
        name="merge_route",
    )(oa, ob, gab, xa, xb, wa, wb, wo, lp['norm2_g'].reshape(1, d).astype(F32), lp['w_router'].astype(F32),
      lp['b_router'].reshape(1, ne).astype(F32), tri)


def _for_each_row(n_rows, fn):
    def body(g, carry):
        base = pl.multiple_of(g * SUBLANES, SUBLANES)
        for j in range(SUBLANES):
            fn(base + j)
        return carry

    lax.fori_loop(0, n_rows // SUBLANES, body, 0)


def _dispatch_body(dest_ref, end_ref, padded_ref, h_ref, xs_ref, zeros_ref, sem, zsem, *, tile, bm):
    @pl.when(pl.program_id(0) == 0)
    def _():
        zeros_ref[...] = jnp.zeros_like(zeros_ref)

        def clear(e):
            start = pl.multiple_of(end_ref[e] - bm, bm)
            return pltpu.make_async_copy(zeros_ref, xs_ref.at[pl.ds(start, bm)], zsem)

        for e in range(N_EXPERTS):
            @pl.when(padded_ref[e] > 0)
            def _():
                clear(e).start()

        for e in range(N_EXPERTS):
            @pl.when(padded_ref[e] > 0)
            def _():
                clear(e).wait()

    def copy(t, kk):
        return pltpu.make_async_copy(h_ref.at[pl.ds(t, 1)],
                                     xs_ref.at[pl.ds(dest_ref[0, 0, t * TOP_K + kk], 1)], sem)

    def issue(t):
        for kk in range(TOP_K):
            copy(t, kk).start(priority=kk % 2)

    def drain(t):
        for kk in range(TOP_K):
            copy(t, kk).wait()

    _for_each_row(tile, issue)
    _for_each_row(tile, drain)


def _dispatch(h2, dest, pad_end, padded, n_rows, tile, bm):
    n, d = h2.shape
    assert n % tile == 0
    nt = n // tile
    smem = pl.BlockSpec(memory_space=pltpu.SMEM)
    return pl.pallas_call(
        functools.partial(_dispatch_body, tile=tile, bm=bm),
        grid=(nt,),
        in_specs=[pl.BlockSpec((1, 1, tile * TOP_K), lambda i: (i, 0, 0), memory_space=pltpu.SMEM), smem, smem,
                  pl.BlockSpec((tile, d), lambda i: (i, 0))],
        out_specs=pl.BlockSpec(memory_space=pl.ANY),
        out_shape=jax.ShapeDtypeStruct((n_rows, d), F32),
        scratch_shapes=[pltpu.VMEM((bm, d), F32), pltpu.SemaphoreType.DMA, pltpu.SemaphoreType.DMA],
        compiler_params=_params(("arbitrary",)),
        name="moe_dispatch",
    )(dest.reshape(nt, 1, tile * TOP_K), pad_end.astype(jnp.int32), padded.astype(jnp.int32), h2)


def _expert_body(be_ref, used_ref, xs_ref, wgu_ref, bgu_ref, wd_ref, bd_ref, *rest):
    ys_refs, (wgu16, wd16) = rest[:-2], rest[-2:]
    i = pl.program_id(0)
    de = wd_ref.shape[1]

    @pl.when(jnp.logical_or(i == 0, be_ref[i] != be_ref[jnp.maximum(i - 1, 0)]))
    def _():
        wgu16[...] = wgu_ref[0].astype(BF16)
        wd16[...] = wd_ref[0].astype(BF16)

    @pl.when(i < used_ref[0])
    def _():
        gu = _dot(xs_ref[...].astype(BF16), wgu16[...]) + bgu_ref[0]
        g = jnp.minimum(gu[:, :de], SWIGLU_LIMIT)
        u = jnp.clip(gu[:, de:], -SWIGLU_LIMIT, SWIGLU_LIMIT)
        act = (u + 1.0) * (g * jax.nn.sigmoid(g * SWIGLU_ALPHA))
        y = _dot(act.astype(BF16), wd16[...]) + bd_ref[0]
        cw = ys_refs[0].shape[-1]
        for q, ref in enumerate(ys_refs):
            ref[...] = y[:, q * cw:(q + 1) * cw]

    @pl.when(i >= used_ref[0])
    def _():
        for ref in ys_refs:
            ref[...] = jnp.zeros_like(ref)


def _experts(xs, block_expert, n_used, wgu, bgu, wd, bd, bm):
    n_rows, d = xs.shape
    ne, _, de2 = wgu.shape
    nb = n_rows // bm
    grid_spec = pltpu.PrefetchScalarGridSpec(
        num_scalar_prefetch=2,
        grid=(nb,),
        in_specs=[
            pl.BlockSpec((bm, d), lambda i, be, nu: (i, 0)),
            pl.BlockSpec((1, d, de2), lambda i, be, nu: (be[i], 0, 0)),
            pl.BlockSpec((1, 1, de2), lambda i, be, nu: (be[i], 0, 0)),
            pl.BlockSpec((1, de2 // 2, d), lambda i, be, nu: (be[i], 0, 0)),
            pl.BlockSpec((1, 1, d), lambda i, be, nu: (be[i], 0, 0)),
        ],
        out_specs=[pl.BlockSpec((bm, SC_ROW_SLAB), lambda i, be, nu: (i, 0))] * (d // SC_ROW_SLAB),
        scratch_shapes=[pltpu.VMEM((d, de2), BF16), pltpu.VMEM((de2 // 2, d), BF16)],
    )
    return pl.pallas_call(
        _expert_body,
        grid_spec=grid_spec,
        out_shape=[jax.ShapeDtypeStruct((n_rows, SC_ROW_SLAB), F32)] * (d // SC_ROW_SLAB),
        compiler_params=_params(("arbitrary",)),
        name="moe_experts",
    )(block_expert, n_used, xs, wgu, bgu.reshape(ne, 1, de2), wd, bd.reshape(ne, 1, d))


def _combine_body(dest_ref, next_ref, ys_ref, gate_ref, x2_ref, g_ref, ya_ref, yb_ref, buf, sems, *,
                  tile, tiles_a, steps):
    i = pl.program_id(0)
    slot = i % 2

    def copy(dref, s, t, kk):
        return pltpu.make_async_copy(ys_ref.at[pl.ds(dref[0, 0, t * TOP_K + kk], 1)],
                                     buf.at[s, kk, pl.ds(t, 1)], sems.at[s])

    def issue(dref, s):
        def body(t):
            for kk in range(TOP_K):
                copy(dref, s, t, kk).start(priority=kk % 2)
        _for_each_row(tile, body)

    @pl.when(i == 0)
    def _():
        issue(dest_ref, 0)

    @pl.when(i + 1 < steps)
    def _():
        issue(next_ref, 1 - slot)

    def drain(t):
        for kk in range(TOP_K):
            copy(dest_ref, slot, t, kk).wait()

    _for_each_row(tile, drain)
    gate = gate_ref[...]
    moe = gate[:, 0:1] * buf[slot, 0]
    for kk in range(1, TOP_K):
        moe = moe + gate[:, kk:kk + 1] * buf[slot, kk]
    y = _rmsnorm(x2_ref[...] + moe, g_ref[...])

    @pl.when(i < tiles_a)
    def _():
        ya_ref[...] = y

    @pl.when(i >= tiles_a)
    def _():
        yb_ref[...] = y


def _combine(ys, dest, gate, x2, final_g, tile, n_a):
    n, d = x2.shape
    assert n % tile == 0 and n_a % tile == 0
    nt = n // tile
    tiles_a = n_a // tile
    dest3 = dest.reshape(nt, 1, tile * TOP_K)
    return pl.pallas_call(
        functools.partial(_combine_body, tile=tile, tiles_a=tiles_a, steps=nt),
        grid=(nt,),
        in_specs=[pl.BlockSpec((1, 1, tile * TOP_K), lambda i: (i, 0, 0), memory_space=pltpu.SMEM),
                  pl.BlockSpec((1, 1, tile * TOP_K), lambda i: (jnp.minimum(i + 1, nt - 1), 0, 0),
                               memory_space=pltpu.SMEM),
                  pl.BlockSpec(memory_space=pl.ANY),
                  pl.BlockSpec((tile, TOP_K), lambda i: (i, 0)),
                  pl.BlockSpec((tile, d), lambda i: (i, 0)),
                  pl.BlockSpec((1, d), lambda i: (0, 0))],
        out_specs=[pl.BlockSpec((tile, d), lambda i: (jnp.minimum(i, tiles_a - 1), 0)),
                   pl.BlockSpec((tile, d), lambda i: (jnp.maximum(i - tiles_a, 0), 0))],
        out_shape=[jax.ShapeDtypeStruct((n_a, d), F32), jax.ShapeDtypeStruct((n - n_a, d), F32)],
        scratch_shapes=[pltpu.VMEM((2, TOP_K, tile, d), F32), pltpu.SemaphoreType.DMA((2,))],
        compiler_params=_params(("arbitrary",)),
        name="moe_combine",
    )(dest3, dest3, ys, gate, x2, final_g.reshape(1, d).astype(F32))


def _sc_gather_rows(x, indices, window):
    n = indices.shape[0]
    d = x.shape[1]
    mesh = plsc.VectorSubcoreMesh(core_axis_name="core", subcore_axis_name="subcore")

    @pl.kernel(out_type=jax.ShapeDtypeStruct((n, d), x.dtype), mesh=mesh, name="sc_gather_rows")
    def gather(x_hbm, i_hbm, o_hbm):
        def body(i_vmem, o_vmem):
            pltpu.sync_copy(x_hbm.at[i_vmem.at[0]], o_vmem)

        pltpu.emit_pipeline(
            body,
            grid=(n // window,),
            in_specs=[pl.BlockSpec((1, window), lambda i: (0, i))],
            out_specs=[pl.BlockSpec((window, d), lambda i: (i, 0))],
            core_axis_name=("core", "subcore"),
            dimension_semantics=(pltpu.PARALLEL,),
        )(i_hbm, o_hbm)

    return gather(x, indices.reshape(1, n))


def _combine_stream_body(*refs, tiles_a, n_slabs):
    row_refs = refs[:n_slabs * TOP_K]
    gate_ref, x2_ref, g_ref, ya_ref, yb_ref = refs[n_slabs * TOP_K:]
    gate = gate_ref[...]
    parts = []
    for s in range(n_slabs):
        part = gate[:, 0:1] * row_refs[s * TOP_K][...]
        for kk in range(1, TOP_K):
            part = part + gate[:, kk:kk + 1] * row_refs[s * TOP_K + kk][...]
        parts.append(part)
    y = _rmsnorm(x2_ref[...] + jnp.concatenate(parts, axis=1), g_ref[...])
    i = pl.program_id(0)

    @pl.when(i < tiles_a)
    def _():
        ya_ref[...] = y

    @pl.when(i >= tiles_a)
    def _():
        yb_ref[...] = y


def _combine_stream(slabs, gate, x2, final_g, tile, n_a):
    n, d = x2.shape
    assert n % tile == 0 and n_a % tile == 0
    tiles_a = n_a // tile
    nt = n // tile
    choice = lambda s, kk: pl.BlockSpec((tile, s.shape[1]), lambda i: (kk * nt + i, 0))
    return pl.pallas_call(
        functools.partial(_combine_stream_body, tiles_a=tiles_a, n_slabs=len(slabs)),
        grid=(nt,),
        in_specs=[choice(s, kk) for s in slabs for kk in range(TOP_K)] + [
                  pl.BlockSpec((tile, TOP_K), lambda i: (i, 0)),
                  pl.BlockSpec((tile, d), lambda i: (i, 0)),
                  pl.BlockSpec((1, d), lambda i: (0, 0))],
        out_specs=[pl.BlockSpec((tile, d), lambda i: (jnp.minimum(i, tiles_a - 1), 0)),
                   pl.BlockSpec((tile, d), lambda i: (jnp.maximum(i - tiles_a, 0), 0))],
        out_shape=[jax.ShapeDtypeStruct((n_a, d), F32), jax.ShapeDtypeStruct((n - n_a, d), F32)],
        compiler_params=_params(("arbitrary",)),
        name="moe_combine",
    )(*[s for s in slabs for _ in range(TOP_K)], gate, x2, final_g.reshape(1, d).astype(F32))


def _moe(h2, x2, idx, gate, rank, counts, lp, final_g, n_a):
    n, d = x2.shape
    bm = EXPERT_ROWS
    counts = counts.reshape(N_EXPERTS)
    padded = (counts + bm - 1) // bm * bm
    pad_end = jnp.cumsum(padded)
    dest = ((pad_end - padded)[idx] + rank).reshape(n * TOP_K).astype(jnp.int32)
    n_blocks = -(-(n * TOP_K) // bm) + N_EXPERTS
    block_start = jnp.arange(n_blocks, dtype=jnp.int32) * bm
    block_expert = jnp.minimum(jnp.sum(pad_end[None, :] <= block_start[:, None], axis=1),
                               N_EXPERTS - 1).astype(jnp.int32)
    n_used = (pad_end[-1:] // bm).astype(jnp.int32)
    xs = _dispatch(h2, dest, pad_end, padded, n_blocks * bm, math.gcd(n, DISPATCH_TILE), bm)
    ys = _experts(xs, block_expert, n_used, lp['w_gate_up'].astype(F32), lp['b_gate_up'].astype(F32),
                  lp['w_down'].astype(F32), lp['b_down'].astype(F32), bm)
    by_choice = dest.reshape(n, TOP_K).T.reshape(n * TOP_K)
    slabs = [_sc_gather_rows(y_slab, by_choice, SC_GATHER_WINDOW) for y_slab in ys]
    return _combine_stream(slabs, gate, x2, final_g, math.gcd(math.gcd(n, n_a), COMBINE_TILE), n_a)


def _to_heads(t, b, s):
    return t.reshape(b, s, SB_HEADS, SB_HEAD_DIM).transpose(0, 2, 1, 3)


def _from_heads(t):
    b, h, s, dh = t.shape
    return t.transpose(0, 2, 1, 3).reshape(b, s, h * dh)


def _layer(x_p, x_s, cache_k, cache_v, state_rwkv, state_shift, lp, final_g):
    bp, tp, d = x_p.shape
    bs, ts, _ = x_s.shape
    sbw = SB_HEADS * SB_HEAD_DIM
    rw_proj = lp['rw_mu'].shape[0]
    n_p = bp * tp
    n = n_p + bs * ts
    x_s = x_s.reshape(bs * ts, d)
    scale = LOG2E * SB_HEAD_DIM ** -0.5
    qkv16, kp, vp, kv_new, rw, gab = _inproj(x_p, x_s, lp['norm1_g'].astype(F32), lp['w_in'].astype(BF16),
                                             rw_proj, scale, ROW_TILE)
    pairs = SB_HEADS // 2
    oa = _stick_breaking(qkv16, 0, 0, qkv16, pairs, 2 * pairs, bp, tp, 0, SB_BLOCK, SB_SUB,
                         jnp.zeros((n, sbw), BF16))
    past = cache_k.shape[2]
    new16 = qkv16[n_p:].reshape(bs, ts, 3 * sbw)
    kv_s = jnp.concatenate([
        jnp.concatenate([_from_heads(cache_k).astype(BF16), new16[:, :, sbw:2 * sbw]], axis=1),
        jnp.concatenate([_from_heads(cache_v).astype(BF16), new16[:, :, 2 * sbw:]], axis=1)], axis=2)
    oa = _stick_breaking(qkv16, n_p, 0, kv_s.reshape(bs * (past + ts), 2 * sbw), 0, pairs, bs, ts, past, ts,
                         min(SB_SUB, past), oa)
    ks, vs = kv_new[:, :sbw], kv_new[:, sbw:]

    s0_p = jnp.zeros((bp, RW_HEADS, RW_HEAD_DIM, RW_HEAD_DIM), F32)
    prev_p = jnp.zeros((bp, 1, rw_proj), F32)
    ob, st_p = _rwkv(rw, bp, tp, tp, prev_p, s0_p, lp, jnp.zeros((n, RW_WIDTH), BF16))
    rw_s = rw[n_p:].reshape(bs, ts, rw_proj)
    ts_pad = -(-ts // RW_CHUNK) * RW_CHUNK
    rw_s_pad = jnp.pad(rw_s, ((0, 0), (0, ts_pad - ts), (0, 0))).reshape(bs * ts_pad, rw_proj)
    ob_s, st_s = _rwkv(rw_s_pad, bs, ts_pad, ts, state_shift, state_rwkv, lp,
                       jnp.zeros((bs * ts_pad, RW_WIDTH), BF16))
    ob = lax.dynamic_update_slice(
        ob, ob_s.reshape(bs, ts_pad, RW_WIDTH)[:, :ts].reshape(bs * ts, RW_WIDTH), (n_p, 0))
    sh_p = jnp.stack([rw[(b + 1) * tp - 1:(b + 1) * tp] for b in range(bp)])
    sh_s = rw_s[:, ts - 1:]

    x2, h2, idx, gate, rank, counts = _merge_route(oa, ob, gab, x_p.reshape(n_p, d), x_s, lp, MERGE_TILE)
    y = _moe(h2, x2, idx, gate, rank, counts, lp, final_g, n_p)
    return (y, kp, vp, st_p, sh_p, _to_heads(ks, bs, ts), _to_heads(vs, bs, ts), st_s, sh_s)


def kernel(x_prompt, x_sample, cache_sb_k, cache_sb_v, state_rwkv, state_shift, norm1_g, w_in, rw_mu, rw_w0, rw_w2, rw_a0, rw_a2, rw_g2, rw_k_k, rw_k_a, rw_r_k, rw_ln_g, rw_ln_b, w_branch_a, w_branch_b, w_out, norm2_g, w_router, b_router, w_gate_up, b_gate_up, w_down, b_down, final_norm_g):
    depth = w_in.shape[0]
    assert depth == 1, "the final RMSNorm is fused into the last (only) layer"
    bp, tp, d = x_prompt.shape
    bs, ts, _ = x_sample.shape
    lp = dict(norm1_g=norm1_g[0], w_in=w_in[0], rw_mu=rw_mu[0], rw_w0=rw_w0[0], rw_w2=rw_w2[0], rw_a0=rw_a0[0],
              rw_a2=rw_a2[0], rw_g2=rw_g2[0], rw_k_k=rw_k_k[0], rw_k_a=rw_k_a[0], rw_r_k=rw_r_k[0].reshape(-1),
              rw_ln_g=rw_ln_g[0], rw_ln_b=rw_ln_b[0], w_branch_a=w_branch_a[0], w_branch_b=w_branch_b[0],
              w_out=w_out[0], norm2_g=norm2_g[0], w_router=w_router[0], b_router=b_router[0],
              w_gate_up=w_gate_up[0], b_gate_up=b_gate_up[0], w_down=w_down[0], b_down=b_down[0])
    y, kp, vp, st_p, sh_p, ks, vs, st_s, sh_s = _layer(
        x_prompt, x_sample, cache_sb_k[0], cache_sb_v[0], state_rwkv[0], state_shift[0], lp, final_norm_g)
    return (y[0].reshape(bp, tp, d), y[1].reshape(bs, ts, d),
            kp[None], vp[None], st_p[None], sh_p[None], ks[None], vs[None], st_s[None], sh_s[None])
```

```python
import functools
import math

import jax
import jax.numpy as jnp
from jax import lax
from jax.experimental import pallas as pl
from jax.experimental.pallas import tpu as pltpu
from jax.experimental.pallas import tpu_sc as plsc

F32 = jnp.float32
BF16 = jnp.bfloat16

SB_HEADS = 8
SB_HEAD_DIM = 64
RW_HEADS = 8
RW_HEAD_DIM = 64
RW_WIDTH = RW_HEADS * RW_HEAD_DIM
RW_DECAY_LORA = 64
RW_AAA_LORA = 64
RW_GATE_LORA = 128
RW_GN_EPS = 64e-5
N_EXPERTS = 32
TOP_K = 4
SWIGLU_LIMIT = 7.0
SWIGLU_ALPHA = 1.702
RMS_EPS = 1e-6

VMEM_LIMIT_BYTES = 56 * 1024 * 1024
ROW_TILE = 256
MERGE_TILE = 512
SB_BLOCK = 512
SB_SUB = 256
LOG2E = 1.4426950408889634
SB_DEAD_LOG2 = -200.0
RW_CHUNK = 64
RW_PREP_CHUNKS = 4
RW_SCAN_CHUNKS = 4
EXPERT_ROWS = 512
DISPATCH_TILE = 2560
COMBINE_TILE = 512
SUBLANES = 8
SC_GATHER_WINDOW = 128
SC_ROW_SLAB = 256


def _params(semantics):
    return pltpu.CompilerParams(dimension_semantics=semantics, vmem_limit_bytes=VMEM_LIMIT_BYTES)


def _dot(a, b):
    return jnp.dot(a, b, preferred_element_type=F32)


def _dot_nt(a, b):
    return lax.dot_general(a, b, (((1,), (1,)), ((), ())), preferred_element_type=F32)


def _softplus_parts(z):
    l = jnp.log(1.0 + jnp.exp(-jnp.abs(z)))
    sp = jnp.maximum(z, 0.0) + l
    return sp, sp - z


def _rmsnorm(x, g):
    return x * lax.rsqrt(jnp.mean(x * x, axis=-1, keepdims=True) + RMS_EPS) * g


def _inproj_body(xa_ref, xb_ref, g_ref, w_ref, qkv16_ref, ka_ref, va_ref, kvb_ref, rw_ref, gab_ref, *,
                 q_scale, tiles_a):
    i = pl.program_id(0)
    x = jnp.where(i < tiles_a, xa_ref[...], xb_ref[...])
    h = _rmsnorm(x, g_ref[...]).astype(BF16)
    sbw = kvb_ref.shape[-1] // 2
    dh = ka_ref.shape[-1]
    q = _dot(h, w_ref[:, 0:sbw])
    qkv16_ref[:, 0:sbw] = (q * q_scale).astype(BF16)
    k = _dot(h, w_ref[:, sbw:2 * sbw])
    v = _dot(h, w_ref[:, 2 * sbw:3 * sbw])
    qkv16_ref[:, sbw:2 * sbw] = k.astype(BF16)
    qkv16_ref[:, 2 * sbw:3 * sbw] = v.astype(BF16)
    off = 3 * sbw
    for ref in (rw_ref, gab_ref):
        width = ref.shape[-1]
        for c0 in range(0, width, sbw):
            cw = min(sbw, width - c0)
            ref[:, c0:c0 + cw] = _dot(h, w_ref[:, off + c0:off + c0 + cw])
        off += width

    @pl.when(i < tiles_a)
    def _():
        for hd in range(sbw // dh):
            ka_ref[0, hd] = k[:, hd * dh:(hd + 1) * dh]
            va_ref[0, hd] = v[:, hd * dh:(hd + 1) * dh]

    @pl.when(i >= tiles_a)
    def _():
        kvb_ref[:, 0:sbw] = k
        kvb_ref[:, sbw:2 * sbw] = v


def _inproj(xa, xb, g, w_bf16, rw_proj, q_scale, tm):
    ba, ta, d = xa.shape
    nb = xb.shape[0]
    n = ba * ta + nb
    sbw = SB_HEADS * SB_HEAD_DIM
    total = w_bf16.shape[1]
    assert total == 3 * sbw + rw_proj + 2 * d and ta % tm == 0 and nb % tm == 0
    tiles_a = ba * ta // tm
    per_seq = ta // tm
    last_a = tiles_a - 1
    row = lambda wd: pl.BlockSpec((tm, wd), lambda i: (i, 0))
    head_major = pl.BlockSpec(
        (1, SB_HEADS, tm, SB_HEAD_DIM),
        lambda i: (jnp.minimum(i, last_a) // per_seq, 0, jnp.minimum(i, last_a) % per_seq, 0))
    hm_shape = jax.ShapeDtypeStruct((ba, SB_HEADS, ta, SB_HEAD_DIM), F32)
    return pl.pallas_call(
        functools.partial(_inproj_body, q_scale=q_scale, tiles_a=tiles_a),
        grid=(n // tm,),
        in_specs=[
            pl.BlockSpec((tm, d), lambda i: (jnp.minimum(i, last_a), 0)),
            pl.BlockSpec((tm, d), lambda i: (jnp.maximum(i - tiles_a, 0), 0)),
            pl.BlockSpec((1, d), lambda i: (0, 0)),
            pl.BlockSpec((d, total), lambda i: (0, 0), pipeline_mode=pl.Buffered(1)),
        ],
        out_specs=[row(3 * sbw), head_major, head_major,
                   pl.BlockSpec((tm, 2 * sbw), lambda i: (jnp.maximum(i - tiles_a, 0), 0)),
                   row(rw_proj), row(2 * d)],
        out_shape=[jax.ShapeDtypeStruct((n, 3 * sbw), BF16), hm_shape, hm_shape,
                   jax.ShapeDtypeStruct((nb, 2 * sbw), F32),
                   jax.ShapeDtypeStruct((n, rw_proj), F32), jax.ShapeDtypeStruct((n, 2 * d), F32)],
        compiler_params=_params(("arbitrary",)),
        name="inproj",
    )(xa.reshape(ba * ta, d), xb, g.reshape(1, d), w_bf16)


def _sb_tile(q, kb, vb, ntri, carry, mask, sb):
    tk = kb.shape[0]
    z = _dot_nt(q, kb)
    neg_abs = lax.bitcast_convert_type(lax.bitcast_convert_type(z, jnp.uint32) | jnp.uint32(0x80000000), F32)
    sp = jnp.maximum(z, 0.0) + jnp.log(1.0 + jnp.exp2(neg_abs)) * LOG2E
    if mask is not None:
        sp = jnp.where(mask, sp, 0.0)
    hi = sp.astype(BF16)
    lo = (sp - hi.astype(F32)).astype(BF16)
    log_beta = z - sp
    wts = []
    for j in range(tk // sb - 1, -1, -1):
        cols = slice(j * sb, (j + 1) * sb)
        later = _dot(hi[:, cols], ntri) + _dot(lo[:, cols], ntri) + carry
        w = jnp.exp2(log_beta[:, cols] + later)
        if mask is not None:
            w = jnp.where(mask[:, cols], w, 0.0)
        wts.append(w.astype(BF16))
        carry = carry - jnp.sum(sp[:, cols], axis=-1, keepdims=True)
    wts = wts[0] if len(wts) == 1 else jnp.concatenate(wts[::-1], axis=1)
    return _dot(wts, vb), carry


def _sb_body(q_ref, k_ref, v_ref, triq_ref, trik_ref, o_in_ref, o_ref, *, tq, tk, past):
    del o_in_ref
    qi = pl.program_id(2)
    q = q_ref[...]
    first = lax.broadcasted_iota(jnp.int32, q.shape, 1) < SB_HEAD_DIM
    zero = jnp.zeros_like(q)
    qs = (jnp.where(first, q, zero), jnp.where(first, zero, q))
    d0 = pl.multiple_of(past + qi * tq, tq)
    row = lax.broadcasted_iota(jnp.int32, (tq, tq), 0)
    col = lax.broadcasted_iota(jnp.int32, (tq, tq), 1)
    kd = k_ref[pl.ds(d0, tq), :]
    vd = v_ref[pl.ds(d0, tq), :]
    sbq = triq_ref.shape[0]
    state = []
    for qh in qs:
        state.extend(_sb_tile(qh, kd, vd, triq_ref[...], jnp.zeros((tq, 1), F32), col < row, sbq))
    nb = (past + qi * tq) // tk
    sbk = trik_ref.shape[0]

    def live(state):
        return jnp.max(jnp.maximum(state[1], state[3])) > SB_DEAD_LOG2

    def cond(loop):
        i, alive, _ = loop
        return jnp.logical_and(i < nb, alive)

    def step(loop):
        i, _, state = loop
        s0 = pl.multiple_of((nb - 1 - i) * tk, tk)
        kb = k_ref[pl.ds(s0, tk), :]
        vb = v_ref[pl.ds(s0, tk), :]
        new = []
        for h, qh in enumerate(qs):
            out, carry = _sb_tile(qh, kb, vb, trik_ref[...], state[2 * h + 1], None, sbk)
            new.extend((state[2 * h] + out, carry))
        return i + 1, live(new), tuple(new)

    _, _, state = lax.while_loop(cond, step, (jnp.int32(0), live(state), tuple(state)))
    o_ref[...] = jnp.where(first, state[0], state[2]).astype(o_ref.dtype)


def _later_matrix(n):
    r = lax.broadcasted_iota(jnp.int32, (n, n), 0)
    c = lax.broadcasted_iota(jnp.int32, (n, n), 1)
    return jnp.where(r > c, -1.0, 0.0).astype(BF16)


def _stick_breaking(q, q_row0, q_lane0, kv, k_lane0, v_lane0, batch, t, past, tq, tk, o_prev):
    s = past + t
    pair = 2 * SB_HEAD_DIM
    sbq = min(tq, SB_SUB)
    sbk = min(tk, SB_SUB)
    nq = t // tq
    qblk = q_row0 // tq
    q_spec = lambda lane0: pl.BlockSpec((tq, pair), lambda b, h, i: (qblk + b * nq + i, lane0 + h))
    kv_spec = lambda lane0: pl.BlockSpec((s, pair), lambda b, h, i: (b, lane0 + h))
    return pl.pallas_call(
        functools.partial(_sb_body, tq=tq, tk=tk, past=past),
        grid=(batch, SB_HEADS // 2, nq),
        in_specs=[
            q_spec(q_lane0), kv_spec(k_lane0), kv_spec(v_lane0),
            pl.BlockSpec((sbq, sbq), lambda b, h, i: (0, 0)),
            pl.BlockSpec((sbk, sbk), lambda b, h, i: (0, 0)),
            pl.BlockSpec(memory_space=pl.ANY),
        ],
        out_specs=q_spec(0),
        out_shape=jax.ShapeDtypeStruct(o_prev.shape, BF16),
        input_output_aliases={5: 0},
        compiler_params=_params(("parallel", "parallel", "arbitrary")),
        name="stick_breaking",
    )(q, kv, kv, _later_matrix(sbq), _later_matrix(sbk), o_prev)


def _split_bf16(x):
    hi = x.astype(BF16)
    return hi, (x - hi.astype(F32)).astype(BF16)


def _dot3(a, b, dot=_dot):
    ah, al = _split_bf16(a)
    bh, bl = _split_bf16(b)
    return dot(ah, bh) + dot(ah, bl) + dot(al, bh)


def _dot_ones_rhs(x, ones_bf16):
    hi, lo = _split_bf16(x)
    return _dot(hi, ones_bf16) + _dot(lo, ones_bf16)


def _dot_ones_lhs(ones_bf16, x):
    hi, lo = _split_bf16(x)
    return _dot(ones_bf16, hi) + _dot(ones_bf16, lo)


HEADS_PER_MATMUL = 4
GROUP_WIDTH = HEADS_PER_MATMUL * RW_HEAD_DIM


def _block_diag(y, mask):
    return jnp.concatenate([y] * HEADS_PER_MATMUL, axis=0) * mask


def _bdmm(xs, ys, mask, dot=_dot, extended=False):
    xh, xl = xs
    yh, yl = ys
    m = xh.shape[0]
    x2 = jnp.concatenate([xh, xl], axis=0) if extended else xh
    outs = []
    for g in range(RW_WIDTH // GROUP_WIDTH):
        ls = slice(g * GROUP_WIDTH, (g + 1) * GROUP_WIDTH)
        out = dot(x2[:, ls], _block_diag(yh[:, ls], mask))
        if extended:
            out = out[:m] + out[m:] + dot(xh[:, ls], _block_diag(yl[:, ls], mask))
        outs.append(out)
    return jnp.concatenate(outs, axis=1)


def _bd_tn(x, y, mask):
    n = RW_HEAD_DIM
    outs = []
    for g in range(RW_WIDTH // GROUP_WIDTH):
        ls = slice(g * GROUP_WIDTH, (g + 1) * GROUP_WIDTH)
        z = _dot(x[:, ls].T.astype(BF16), y[:, ls].astype(BF16)) * mask
        outs.append(z[0:n] + z[n:2 * n] + z[2 * n:3 * n] + z[3 * n:4 * n])
    return jnp.concatenate(outs, axis=1)


def _rwkv_prep_body(p_ref, pb_ref, prev_ref, mu_ref, w0_ref, w2_ref, a0_ref, a2_ref, g2_ref, kk_ref, ka_ref,
                    rk_ref, tri_ref, bd_ref, mbf_ref, mf32_ref,
                    rhat_ref, y0_ref, a_ref, d_ref, bonus_ref, g_ref, *, chunk, groups, n_sq, t_valid):
    n = RW_HEAD_DIM
    w_ = RW_WIDTH
    p = p_ref[...]
    rows = lax.broadcasted_iota(jnp.int32, p.shape, 0)
    prev_row = jnp.where(pl.program_id(1) == 0, prev_ref[0], pb_ref[7:8, :])
    p_shift = jnp.where(rows == 0, prev_row, pltpu.roll(p, 1, 0))

    xs = p + (p_shift - p) * mu_ref[...]
    r = xs[:, 0:w_]
    k = xs[:, w_:2 * w_]
    v = xs[:, 2 * w_:3 * w_]
    o1 = 3 * w_
    dw = xs[:, o1:o1 + RW_DECAY_LORA]
    da = xs[:, o1 + RW_DECAY_LORA:o1 + RW_DECAY_LORA + RW_AAA_LORA]
    dg = xs[:, o1 + RW_DECAY_LORA + RW_AAA_LORA:]

    sp_neg = _softplus_parts(w0_ref[...] + _dot3(jnp.tanh(dw), w2_ref[...]))[1]
    log_decay = -jnp.exp(-sp_neg - 0.5)
    a = jax.nn.sigmoid(a0_ref[...] + _dot3(da, a2_ref[...]))
    g_ref[...] = _dot(jax.nn.sigmoid(dg).astype(BF16), g2_ref[...])

    bd = bd_ref[...]
    kk = k * kk_ref[...]
    kk = kk / jnp.maximum(jnp.sqrt(_dot_ones_rhs(kk * kk, bd)), 1e-12)
    k2 = k * (1.0 + (a - 1.0) * ka_ref[...])
    a_vec = -kk
    b_vec = kk * a
    bonus_ref[...] = _dot((r * k2 * rk_ref[...]).astype(BF16), bd) * v
    if t_valid < chunk:
        live = lax.broadcasted_iota(jnp.int32, r.shape, 0) < t_valid
        keep = lambda x: jnp.where(live, x, 0.0)
        log_decay, r, k2, v, a_vec, b_vec = (keep(x) for x in (log_decay, r, k2, v, a_vec, b_vec))

    cw = _dot_ones_lhs(tri_ref[...], log_decay)
    cw_last = jnp.concatenate(
        [jnp.broadcast_to(cw[(j + 1) * chunk - 1:(j + 1) * chunk, :], (chunk, w_)) for j in range(groups)], axis=0)
    r_t = r * jnp.exp(cw)
    a_t = a_vec * jnp.exp(cw - log_decay)
    e_neg = jnp.exp(-cw)
    b_t = b_vec * e_neg
    k_t = k2 * e_neg
    e_rel = jnp.exp(cw_last - cw)
    b_h = b_vec * e_rel
    k_h = k2 * e_rel
    w_c = jnp.exp(cw_last)

    rr = lax.broadcasted_iota(jnp.int32, (chunk, w_), 0)
    ss = lax.broadcasted_iota(jnp.int32, (chunk, w_), 1) % n
    strict = ss < rr
    incl = ss <= rr
    eye = (ss == rr).astype(F32)
    m_bf16 = mbf_ref[...]
    m_f32 = mf32_ref[...]

    for j in range(groups):
        rs = slice(j * chunk, (j + 1) * chunk)
        v_j = v[rs]
        v_s = _split_bf16(v_j)
        ar_s = _split_bf16(jnp.concatenate([a_t[rs], r_t[rs]], axis=0))
        gram_b = _bdmm(ar_s, _split_bf16(b_t[rs]), m_bf16, _dot_nt)
        gram_k = _bdmm(ar_s, _split_bf16(k_t[rs]), m_bf16, _dot_nt)
        l_ab = jnp.where(strict, gram_b[:chunk], 0.0)
        l_ak = jnp.where(strict, gram_k[:chunk], 0.0)
        m_rb_s = _split_bf16(jnp.where(incl, gram_b[chunk:], 0.0))
        m_rk_s = _split_bf16(jnp.where(incl, gram_k[chunk:], 0.0))
        l_s = _split_bf16(l_ab)
        t_inv = eye + l_ab
        l_pow = _bdmm(l_s, l_s, m_bf16)
        for i in range(n_sq):
            lp_s = _split_bf16(l_pow)
            t_inv = t_inv + _bdmm(lp_s, _split_bf16(t_inv), m_bf16)
            if i + 1 < n_sq:
                l_pow = _bdmm(lp_s, lp_s, m_bf16)
        t_s = _split_bf16(t_inv)
        p_mat = _bdmm(t_s, _split_bf16(a_t[rs]), m_bf16)
        q_mat = _bdmm(t_s, _split_bf16(_bdmm(_split_bf16(l_ak), v_s, m_bf16)), m_bf16)
        rhat_ref[rs, :] = r_t[rs] + _bdmm(m_rb_s, _split_bf16(p_mat), m_bf16)
        y0_ref[rs, :] = _bdmm(m_rb_s, _split_bf16(q_mat), m_bf16) + _bdmm(m_rk_s, v_s, m_bf16)
        ns = slice(j * n, (j + 1) * n)
        a_ref[ns, :] = _bd_tn(p_mat, b_h[rs], m_f32) + eye * w_c[j * chunk:j * chunk + 1, :]
        d_ref[ns, :] = _bd_tn(jnp.concatenate([q_mat, v_j], axis=0),
                              jnp.concatenate([b_h[rs], k_h[rs]], axis=0), m_f32)


def _rwkv_scan_body(rhat_ref, y0_ref, a_ref, d_ref, bonus_ref, g_ref, s0_ref, lng_ref, lnb_ref, bd_ref, mbf_ref,
                    o_ref, s_ref, s_scr, y_scr, *, chunk, groups, steps):
    n = RW_HEAD_DIM
    c = pl.program_id(1)

    @pl.when(c == 0)
    def _():
        for h in range(RW_HEADS):
            s_scr[:, h * n:(h + 1) * n] = s0_ref[0, h]

    mask = mbf_ref[...]
    s = s_scr[...]
    for j in range(groups):
        rs = slice(j * chunk, (j + 1) * chunk)
        ns = slice(j * n, (j + 1) * n)
        s_s = _split_bf16(s)
        y_scr[rs, :] = y0_ref[rs, :] + _bdmm(_split_bf16(rhat_ref[rs, :]), s_s, mask, _dot_nt, extended=True)
        s = _bdmm(s_s, _split_bf16(a_ref[ns, :]), mask, extended=True) + d_ref[ns, :]
    s_scr[...] = s

    @pl.when(c == steps - 1)
    def _():
        for h in range(RW_HEADS):
            s_ref[0, h] = s[:, h * n:(h + 1) * n]

    bd = bd_ref[...]
    y = y_scr[...]
    mean = _dot_ones_rhs(y, bd) * (1.0 / n)
    yc = y - mean
    var = _dot_ones_rhs(yc * yc, bd) * (1.0 / n)
    yn = yc * lax.rsqrt(var + RW_GN_EPS) * lng_ref[...] + lnb_ref[...]
    o_ref[...] = ((yn + bonus_ref[...]) * g_ref[...]).astype(o_ref.dtype)


def _rwkv(rw_all, batch, t, t_valid, prev, s0, lp, o_prev):
    proj = rw_all.shape[1]
    n_all = o_prev.shape[0]
    row_off = 0
    n = RW_HEAD_DIM
    chunk = RW_CHUNK
    assert chunk == n and t % chunk == 0 and (t_valid == t or t == chunk)
    n_sq = chunk.bit_length() - 2
    nc = t // chunk
    gp = min(RW_PREP_CHUNKS, nc)
    gs = min(RW_SCAN_CHUNKS, nc)
    rows_p = gp * chunk
    rows_s = gs * chunk
    n_tok = batch * t
    row = lambda x: x.reshape(1, -1).astype(F32)
    r = lax.broadcasted_iota(jnp.int32, (rows_p, rows_p), 0)
    c = lax.broadcasted_iota(jnp.int32, (rows_p, rows_p), 1)
    tri = ((c <= r) & (c // chunk == r // chunk)).astype(BF16)
    hr = lax.broadcasted_iota(jnp.int32, (RW_WIDTH, RW_WIDTH), 0) // n
    hc = lax.broadcasted_iota(jnp.int32, (RW_WIDTH, RW_WIDTH), 1) // n
    bd = (hr == hc).astype(BF16)
    group_mask = (hr == hc)[:GROUP_WIDTH, :GROUP_WIDTH]
    const = lambda shape: pl.BlockSpec(shape, lambda b, c: (0,) * len(shape))

    blk_p = row_off // rows_p
    prep_args = [rw_all, rw_all, prev.astype(F32), row(lp['rw_mu']), row(lp['rw_w0']), lp['rw_w2'].astype(F32),
                 row(lp['rw_a0']), lp['rw_a2'].astype(F32), lp['rw_g2'].astype(BF16), row(lp['rw_k_k']),
                 row(lp['rw_k_a']), row(lp['rw_r_k']), tri, bd, group_mask.astype(BF16), group_mask.astype(F32)]
    tok_spec = lambda rows: pl.BlockSpec((rows, RW_WIDTH), lambda b, c: (b * (t // rows) + c, 0))
    mat_spec = lambda g: pl.BlockSpec((g * n, RW_WIDTH), lambda b, c: (b * (nc // g) + c, 0))
    tok_shape = jax.ShapeDtypeStruct((n_tok, RW_WIDTH), F32)
    mat_shape = jax.ShapeDtypeStruct((batch * nc * n, RW_WIDTH), F32)
    rhat, y0, a_mat, d_mat, bonus, g = pl.pallas_call(
        functools.partial(_rwkv_prep_body, chunk=chunk, groups=gp, n_sq=n_sq, t_valid=t_valid),
        grid=(batch, nc // gp),
        in_specs=[
            pl.BlockSpec((rows_p, proj), lambda b, c: (blk_p + b * (nc // gp) + c, 0)),
            pl.BlockSpec((8, proj), lambda b, c: (jnp.maximum((row_off + b * t + c * rows_p) // 8 - 1, 0), 0)),
            pl.BlockSpec((1, 1, proj), lambda b, c: (b, 0, 0)),
        ] + [const(x.shape) for x in prep_args[3:]],
        out_specs=[tok_spec(rows_p), tok_spec(rows_p), mat_spec(gp), mat_spec(gp), tok_spec(rows_p),
                   tok_spec(rows_p)],
        out_shape=[tok_shape, tok_shape, mat_shape, mat_shape, tok_shape, tok_shape],
        compiler_params=_params(("parallel", "parallel")),
        name="rwkv7_prep",
    )(*prep_args)

    blk_s = row_off // rows_s
    scan_args = [rhat, y0, a_mat, d_mat, bonus, g, s0.astype(F32), row(lp['rw_ln_g']), row(lp['rw_ln_b']), bd,
                 group_mask.astype(BF16), o_prev]
    state_spec = pl.BlockSpec((1, RW_HEADS, n, n), lambda b, c: (b, 0, 0, 0))
    n_in = len(scan_args) - 1

    def scan_body(*refs):
        _rwkv_scan_body(*refs[:n_in], *refs[n_in + 1:], chunk=chunk, groups=gs, steps=nc // gs)

    o, state = pl.pallas_call(
        scan_body,
        grid=(batch, nc // gs),
        in_specs=[tok_spec(rows_s), tok_spec(rows_s), mat_spec(gs), mat_spec(gs), tok_spec(rows_s),
                  tok_spec(rows_s), state_spec, const((1, RW_WIDTH)), const((1, RW_WIDTH)), const(bd.shape),
                  const(group_mask.shape), pl.BlockSpec(memory_space=pl.ANY)],
        out_specs=[pl.BlockSpec((rows_s, RW_WIDTH), lambda b, c: (blk_s + b * (nc // gs) + c, 0)), state_spec],
        out_shape=[jax.ShapeDtypeStruct((n_all, RW_WIDTH), BF16),
                   jax.ShapeDtypeStruct((batch, RW_HEADS, n, n), F32)],
        scratch_shapes=[pltpu.VMEM((n, RW_WIDTH), F32), pltpu.VMEM((rows_s, RW_WIDTH), F32)],
        input_output_aliases={n_in: 0},
        compiler_params=_params(("parallel", "arbitrary")),
        name="rwkv7_scan",
    )(*scan_args)
    return o, state


def _merge_body(oa_ref, ob_ref, gab_ref, xa_ref, xb_ref, wa_ref, wb_ref, wo_ref, g2_ref, wr_ref, br_ref, tri_ref,
                x2_ref, h2_ref, idx_ref, gate_ref, rank_ref, cnt_ref, carry_scr, *, tiles_a):
    i = pl.program_id(0)
    d = xa_ref.shape[-1]

    @pl.when(i == 0)
    def _():
        carry_scr[...] = jnp.zeros_like(carry_scr)

    gab = gab_ref[...]
    merged = (jax.nn.sigmoid(gab[:, :d]) * _dot(oa_ref[...], wa_ref[...])
              + jax.nn.sigmoid(gab[:, d:]) * _dot(ob_ref[...], wb_ref[...]))
    x = jnp.where(i < tiles_a, xa_ref[...], xb_ref[...])
    x2 = x + _dot(merged.astype(BF16), wo_ref[...])
    x2_ref[...] = x2
    h2 = _rmsnorm(x2, g2_ref[...])
    h2_ref[...] = h2
    logits = _dot3(h2, wr_ref[...]) + br_ref[...]

    tm, ne = logits.shape
    col = lax.broadcasted_iota(jnp.int32, (tm, ne), 1)
    c4 = lax.broadcasted_iota(jnp.int32, (tm, TOP_K), 1)
    work = logits
    tops, idxs = [], []
    for _ in range(TOP_K):
        m = jnp.max(work, axis=-1, keepdims=True)
        ix = jnp.min(jnp.where(work == m, col, ne), axis=-1, keepdims=True)
        tops.append(m)
        idxs.append(ix)
        work = jnp.where(col == ix, -jnp.inf, work)
    es = [jnp.exp(m - tops[0]) for m in tops]
    denom = es[0] + es[1] + es[2] + es[3]
    onehot = jnp.zeros((tm, ne), F32)
    for ix in idxs:
        onehot = onehot + (col == ix).astype(F32)
    before = carry_scr[...] + _dot(tri_ref[...], onehot.astype(BF16))
    idx_out = jnp.zeros((tm, TOP_K), jnp.int32)
    gate_out = jnp.zeros((tm, TOP_K), F32)
    rank_out = jnp.zeros((tm, TOP_K), jnp.int32)
    for kk in range(TOP_K):
        rk = jnp.sum(jnp.where(col == idxs[kk], before, 0.0), axis=-1, keepdims=True)
        idx_out = jnp.where(c4 == kk, idxs[kk], idx_out)
        gate_out = jnp.where(c4 == kk, es[kk] / denom, gate_out)
        rank_out = jnp.where(c4 == kk, rk.astype(jnp.int32), rank_out)
    idx_ref[...] = idx_out
    gate_ref[...] = gate_out
    rank_ref[...] = rank_out
    carry = carry_scr[...] + jnp.sum(onehot, axis=0, keepdims=True)
    carry_scr[...] = carry
    cnt_ref[...] = carry.astype(jnp.int32)


def _merge_route(oa, ob, gab, xa, xb, lp, tm):
    d = xa.shape[1]
    n = xa.shape[0] + xb.shape[0]
    assert xa.shape[0] % tm == 0 and xb.shape[0] % tm == 0
    tiles_a = xa.shape[0] // tm
    ne = N_EXPERTS
    r = lax.broadcasted_iota(jnp.int32, (tm, tm), 0)
    c = lax.broadcasted_iota(jnp.int32, (tm, tm), 1)
    tri = (c < r).astype(BF16)
    const = lambda shape: pl.BlockSpec(shape, lambda i: (0,) * len(shape))
    rowblk = lambda wd: pl.BlockSpec((tm, wd), lambda i: (i, 0))
    wa = lp['w_branch_a'].astype(BF16)
    wb = lp['w_branch_b'].astype(BF16)
    wo = lp['w_out'].astype(BF16)
    return pl.pallas_call(
        functools.partial(_merge_body, tiles_a=tiles_a),
        grid=(n // tm,),
        in_specs=[rowblk(oa.shape[1]), rowblk(ob.shape[1]), rowblk(2 * d),
                  pl.BlockSpec((tm, d), lambda i: (jnp.minimum(i, tiles_a - 1), 0)),
                  pl.BlockSpec((tm, d), lambda i: (jnp.maximum(i - tiles_a, 0), 0)),
                  const(wa.shape), const(wb.shape), const(wo.shape), const((1, d)), const((d, ne)),
                  const((1, ne)), const((tm, tm))],
        out_specs=[rowblk(d), rowblk(d), rowblk(TOP_K), rowblk(TOP_K), rowblk(TOP_K), const((1, ne))],
        out_shape=[jax.ShapeDtypeStruct((n, d), F32), jax.ShapeDtypeStruct((n, d), F32),
                   jax.ShapeDtypeStruct((n, TOP_K), jnp.int32), jax.ShapeDtypeStruct((n, TOP_K), F32),
                   jax.ShapeDtypeStruct((n, TOP_K), jnp.int32), jax.ShapeDtypeStruct((1, ne), jnp.int32)],
        scratch_shapes=[pltpu.VMEM((1, ne), F32)],
        compiler_params=_params(("arbitrary",)),
        name="merge_route",
    )(oa, ob, gab, xa, xb, wa, wb, wo, lp['norm2_g'].reshape(1, d).astype(F32), lp['w_router'].astype(F32),
      lp['b_router'].reshape(1, ne).astype(F32), tri)


def _for_each_row(n_rows, fn):
    def body(g, carry):
        base = pl.multiple_of(g * SUBLANES, SUBLANES)
        for j in range(SUBLANES):
            fn(base + j)
        return carry

    lax.fori_loop(0, n_rows // SUBLANES, body, 0)


def _dispatch_body(dest_ref, end_ref, padded_ref, h_ref, xs_ref, zeros_ref, sem, zsem, *, tile, bm):
    @pl.when(pl.program_id(0) == 0)
    def _():
        zeros_ref[...] = jnp.zeros_like(zeros_ref)

        def clear(e):
            start = pl.multiple_of(end_ref[e] - bm, bm)
            return pltpu.make_async_copy(zeros_ref, xs_ref.at[pl.ds(start, bm)], zsem)

        for e in range(N_EXPERTS):
            @pl.when(padded_ref[e] > 0)
            def _():
                clear(e).start()

        for e in range(N_EXPERTS):
            @pl.when(padded_ref[e] > 0)
            def _():
                clear(e).wait()

    def copy(t, kk):
        return pltpu.make_async_copy(h_ref.at[pl.ds(t, 1)],
                                     xs_ref.at[pl.ds(dest_ref[0, 0, t * TOP_K + kk], 1)], sem)

    def issue(t):
        for kk in range(TOP_K):
            copy(t, kk).start(priority=kk % 2)

    def drain(t):
        for kk in range(TOP_K):
            copy(t, kk).wait()

    _for_each_row(tile, issue)
    _for_each_row(tile, drain)


def _dispatch(h2, dest, pad_end, padded, n_rows, tile, bm):
    n, d = h2.shape
    assert n % tile == 0
    nt = n // tile
    smem = pl.BlockSpec(memory_space=pltpu.SMEM)
    return pl.pallas_call(
        functools.partial(_dispatch_body, tile=tile, bm=bm),
        grid=(nt,),
        in_specs=[pl.BlockSpec((1, 1, tile * TOP_K), lambda i: (i, 0, 0), memory_space=pltpu.SMEM), smem, smem,
                  pl.BlockSpec((tile, d), lambda i: (i, 0))],
        out_specs=pl.BlockSpec(memory_space=pl.ANY),
        out_shape=jax.ShapeDtypeStruct((n_rows, d), F32),
        scratch_shapes=[pltpu.VMEM((bm, d), F32), pltpu.SemaphoreType.DMA, pltpu.SemaphoreType.DMA],
        compiler_params=_params(("arbitrary",)),
        name="moe_dispatch",
    )(dest.reshape(nt, 1, tile * TOP_K), pad_end.astype(jnp.int32), padded.astype(jnp.int32), h2)


def _expert_body(be_ref, used_ref, xs_ref, wgu_ref, bgu_ref, wd_ref, bd_ref, *rest):
    ys_refs, (wgu16, wd16) = rest[:-2], rest[-2:]
    i = pl.program_id(0)
    de = wd_ref.shape[1]

    @pl.when(jnp.logical_or(i == 0, be_ref[i] != be_ref[jnp.maximum(i - 1, 0)]))
    def _():
        wgu16[...] = wgu_ref[0].astype(BF16)
        wd16[...] = wd_ref[0].astype(BF16)

    @pl.when(i < used_ref[0])
    def _():
        gu = _dot(xs_ref[...].astype(BF16), wgu16[...]) + bgu_ref[0]
        g = jnp.minimum(gu[:, :de], SWIGLU_LIMIT)
        u = jnp.clip(gu[:, de:], -SWIGLU_LIMIT, SWIGLU_LIMIT)
        act = (u + 1.0) * (g * jax.nn.sigmoid(g * SWIGLU_ALPHA))
        y = _dot(act.astype(BF16), wd16[...]) + bd_ref[0]
        cw = ys_refs[0].shape[-1]
        for q, ref in enumerate(ys_refs):
            ref[...] = y[:, q * cw:(q + 1) * cw]

    @pl.when(i >= used_ref[0])
    def _():
        for ref in ys_refs:
            ref[...] = jnp.zeros_like(ref)


def _experts(xs, block_expert, n_used, wgu, bgu, wd, bd, bm):
    n_rows, d = xs.shape
    ne, _, de2 = wgu.shape
    nb = n_rows // bm
    grid_spec = pltpu.PrefetchScalarGridSpec(
        num_scalar_prefetch=2,
        grid=(nb,),
        in_specs=[
            pl.BlockSpec((bm, d), lambda i, be, nu: (i, 0)),
            pl.BlockSpec((1, d, de2), lambda i, be, nu: (be[i], 0, 0)),
            pl.BlockSpec((1, 1, de2), lambda i, be, nu: (be[i], 0, 0)),
            pl.BlockSpec((1, de2 // 2, d), lambda i, be, nu: (be[i], 0, 0)),
            pl.BlockSpec((1, 1, d), lambda i, be, nu: (be[i], 0, 0)),
        ],
        out_specs=[pl.BlockSpec((bm, SC_ROW_SLAB), lambda i, be, nu: (i, 0))] * (d // SC_ROW_SLAB),
        scratch_shapes=[pltpu.VMEM((d, de2), BF16), pltpu.VMEM((de2 // 2, d), BF16)],
    )
    return pl.pallas_call(
        _expert_body,
        grid_spec=grid_spec,
        out_shape=[jax.ShapeDtypeStruct((n_rows, SC_ROW_SLAB), F32)] * (d // SC_ROW_SLAB),
        compiler_params=_params(("arbitrary",)),
        name="moe_experts",
    )(block_expert, n_used, xs, wgu, bgu.reshape(ne, 1, de2), wd, bd.reshape(ne, 1, d))


def _sc_gather_rows(x, indices, window):
    n = indices.shape[0]
    d = x.shape[1]
    mesh = plsc.VectorSubcoreMesh(core_axis_name="core", subcore_axis_name="subcore")

    @pl.kernel(out_type=jax.ShapeDtypeStruct((n, d), x.dtype), mesh=mesh, name="sc_gather_rows")
    def gather(x_hbm, i_hbm, o_hbm):
        def body(i_vmem, o_vmem):
            pltpu.sync_copy(x_hbm.at[i_vmem.at[0]], o_vmem)

        pltpu.emit_pipeline(
            body,
            grid=(n // window,),
            in_specs=[pl.BlockSpec((1, window), lambda i: (0, i))],
            out_specs=[pl.BlockSpec((window, d), lambda i: (i, 0))],
            core_axis_name=("core", "subcore"),
            dimension_semantics=(pltpu.PARALLEL,),
        )(i_hbm, o_hbm)

    return gather(x, indices.reshape(1, n))


def _combine_stream_body(*refs, tiles_a, n_slabs):
    row_refs = refs[:n_slabs * TOP_K]
    gate_ref, x2_ref, g_ref, ya_ref, yb_ref = refs[n_slabs * TOP_K:]
    gate = gate_ref[...]
    parts = []
    for s in range(n_slabs):
        part = gate[:, 0:1] * row_refs[s * TOP_K][...]
        for kk in range(1, TOP_K):
            part = part + gate[:, kk:kk + 1] * row_refs[s * TOP_K + kk][...]
        parts.append(part)
    y = _rmsnorm(x2_ref[...] + jnp.concatenate(parts, axis=1), g_ref[...])
    i = pl.program_id(0)

    @pl.when(i < tiles_a)
    def _():
        ya_ref[...] = y

    @pl.when(i >= tiles_a)
    def _():
        yb_ref[...] = y


def _combine_stream(slabs, gate, x2, final_g, tile, n_a):
    n, d = x2.shape
    assert n % tile == 0 and n_a % tile == 0
    tiles_a = n_a // tile
    nt = n // tile
    choice = lambda s, kk: pl.BlockSpec((tile, s.shape[1]), lambda i: (kk * nt + i, 0))
    return pl.pallas_call(
        functools.partial(_combine_stream_body, tiles_a=tiles_a, n_slabs=len(slabs)),
        grid=(nt,),
        in_specs=[choice(s, kk) for s in slabs for kk in range(TOP_K)] + [
                  pl.BlockSpec((tile, TOP_K), lambda i: (i, 0)),
                  pl.BlockSpec((tile, d), lambda i: (i, 0)),
                  pl.BlockSpec((1, d), lambda i: (0, 0))],
        out_specs=[pl.BlockSpec((tile, d), lambda i: (jnp.minimum(i, tiles_a - 1), 0)),
                   pl.BlockSpec((tile, d), lambda i: (jnp.maximum(i - tiles_a, 0), 0))],
        out_shape=[jax.ShapeDtypeStruct((n_a, d), F32), jax.ShapeDtypeStruct((n - n_a, d), F32)],
        compiler_params=_params(("arbitrary",)),
        name="moe_combine",
    )(*[s for s in slabs for _ in range(TOP_K)], gate, x2, final_g.reshape(1, d).astype(F32))


def _moe(h2, x2, idx, gate, rank, counts, lp, final_g, n_a):
    n, d = x2.shape
    bm = EXPERT_ROWS
    counts = counts.reshape(N_EXPERTS)
    padded = (counts + bm - 1) // bm * bm
    pad_end = jnp.cumsum(padded)
    dest = ((pad_end - padded)[idx] + rank).reshape(n * TOP_K).astype(jnp.int32)
    n_blocks = -(-(n * TOP_K) // bm) + N_EXPERTS
    block_start = jnp.arange(n_blocks, dtype=jnp.int32) * bm
    block_expert = jnp.minimum(jnp.sum(pad_end[None, :] <= block_start[:, None], axis=1),
                               N_EXPERTS - 1).astype(jnp.int32)
    n_used = (pad_end[-1:] // bm).astype(jnp.int32)
    xs = _dispatch(h2, dest, pad_end, padded, n_blocks * bm, math.gcd(n, DISPATCH_TILE), bm)
    ys = _experts(xs, block_expert, n_used, lp['w_gate_up'].astype(F32), lp['b_gate_up'].astype(F32),
                  lp['w_down'].astype(F32), lp['b_down'].astype(F32), bm)
    by_choice = dest.reshape(n, TOP_K).T.reshape(n * TOP_K)
    slabs = [_sc_gather_rows(y_slab, by_choice, SC_GATHER_WINDOW) for y_slab in ys]
    return _combine_stream(slabs, gate, x2, final_g, math.gcd(math.gcd(n, n_a), COMBINE_TILE), n_a)


def _to_heads(t, b, s):
    return t.reshape(b, s, SB_HEADS, SB_HEAD_DIM).transpose(0, 2, 1, 3)


def _from_heads(t):
    b, h, s, dh = t.shape
    return t.transpose(0, 2, 1, 3).reshape(b, s, h * dh)


def _layer(x_p, x_s, cache_k, cache_v, state_rwkv, state_shift, lp, final_g):
    bp, tp, d = x_p.shape
    bs, ts, _ = x_s.shape
    sbw = SB_HEADS * SB_HEAD_DIM
    rw_proj = lp['rw_mu'].shape[0]
    n_p = bp * tp
    n = n_p + bs * ts
    x_s = x_s.reshape(bs * ts, d)
    scale = LOG2E * SB_HEAD_DIM ** -0.5
    qkv16, kp, vp, kv_new, rw, gab = _inproj(x_p, x_s, lp['norm1_g'].astype(F32), lp['w_in'].astype(BF16),
                                             rw_proj, scale, ROW_TILE)
    pairs = SB_HEADS // 2
    oa = _stick_breaking(qkv16, 0, 0, qkv16, pairs, 2 * pairs, bp, tp, 0, SB_BLOCK, SB_SUB,
                         jnp.zeros((n, sbw), BF16))
    past = cache_k.shape[2]
    new16 = qkv16[n_p:].reshape(bs, ts, 3 * sbw)
    kv_s = jnp.concatenate([
        jnp.concatenate([_from_heads(cache_k).astype(BF16), new16[:, :, sbw:2 * sbw]], axis=1),
        jnp.concatenate([_from_heads(cache_v).astype(BF16), new16[:, :, 2 * sbw:]], axis=1)], axis=2)
    oa = _stick_breaking(qkv16, n_p, 0, kv_s.reshape(bs * (past + ts), 2 * sbw), 0, pairs, bs, ts, past, ts,
                         min(SB_SUB, past), oa)
    ks, vs = kv_new[:, :sbw], kv_new[:, sbw:]

    s0_p = jnp.zeros((bp, RW_HEADS, RW_HEAD_DIM, RW_HEAD_DIM), F32)
    prev_p = jnp.zeros((bp, 1, rw_proj), F32)
    ob, st_p = _rwkv(rw, bp, tp, tp, prev_p, s0_p, lp, jnp.zeros((n, RW_WIDTH), BF16))
    rw_s = rw[n_p:].reshape(bs, ts, rw_proj)
    ts_pad = -(-ts // RW_CHUNK) * RW_CHUNK
    rw_s_pad = jnp.pad(rw_s, ((0, 0), (0, ts_pad - ts), (0, 0))).reshape(bs * ts_pad, rw_proj)
    ob_s, st_s = _rwkv(rw_s_pad, bs, ts_pad, ts, state_shift, state_rwkv, lp,
                       jnp.zeros((bs * ts_pad, RW_WIDTH), BF16))
    ob = lax.dynamic_update_slice(
        ob, ob_s.reshape(bs, ts_pad, RW_WIDTH)[:, :ts].reshape(bs * ts, RW_WIDTH), (n_p, 0))
    sh_p = jnp.stack([rw[(b + 1) * tp - 1:(b + 1) * tp] for b in range(bp)])
    sh_s = rw_s[:, ts - 1:]

    x2, h2, idx, gate, rank, counts = _merge_route(oa, ob, gab, x_p.reshape(n_p, d), x_s, lp, MERGE_TILE)
    y = _moe(h2, x2, idx, gate, rank, counts, lp, final_g, n_p)
    return (y, kp, vp, st_p, sh_p, _to_heads(ks, bs, ts), _to_heads(vs, bs, ts), st_s, sh_s)


def kernel(x_prompt, x_sample, cache_sb_k, cache_sb_v, state_rwkv, state_shift, norm1_g, w_in, rw_mu, rw_w0, rw_w2, rw_a0, rw_a2, rw_g2, rw_k_k, rw_k_a, rw_r_k, rw_ln_g, rw_ln_b, w_branch_a, w_branch_b, w_out, norm2_g, w_router, b_router, w_gate_up, b_gate_up, w_down, b_down, final_norm_g):
    depth = w_in.shape[0]
    assert depth == 1, "the final RMSNorm is fused into the last (only) layer"
    bp, tp, d = x_prompt.shape
    bs, ts, _ = x_sample.shape
    lp = dict(norm1_g=norm1_g[0], w_in=w_in[0], rw_mu=rw_mu[0], rw_w0=rw_w0[0], rw_w2=rw_w2[0], rw_a0=rw_a0[0],
              rw_a2=rw_a2[0], rw_g2=rw_g2[0], rw_k_k=rw_k_k[0], rw_k_a=rw_k_a[0], rw_r_k=rw_r_k[0].reshape(-1),
              rw_ln_g=rw_ln_g[0], rw_ln_b=rw_ln_b[0], w_branch_a=w_branch_a[0], w_branch_b=w_branch_b[0],
              w_out=w_out[0], norm2_g=norm2_g[0], w_router=w_router[0], b_router=b_router[0],
              w_gate_up=w_gate_up[0], b_gate_up=b_gate_up[0], w_down=w_down[0], b_down=b_down[0])
    y, kp, vp, st_p, sh_p, ks, vs, st_s, sh_s = _layer(
        x_prompt, x_sample, cache_sb_k[0], cache_sb_v[0], state_rwkv[0], state_shift[0], lp, final_norm_g)
    return (y[0].reshape(bp, tp, d), y[1].reshape(bs, ts, d),
            kp[None], vp[None], st_p[None], sh_p[None], ks[None], vs[None], st_s[None], sh_s[None])
```

```python
import functools
import math

import jax
import jax.numpy as jnp
from jax import lax
from jax.experimental import pallas as pl
from jax.experimental.pallas import tpu as pltpu
from jax.experimental.pallas import tpu_sc as plsc

F32 = jnp.float32
BF16 = jnp.bfloat16

SB_HEADS = 8
SB_HEAD_DIM = 64
RW_HEADS = 8
RW_HEAD_DIM = 64
RW_WIDTH = RW_HEADS * RW_HEAD_DIM
RW_DECAY_LORA = 64
RW_AAA_LORA = 64
RW_GATE_LORA = 128
RW_GN_EPS = 64e-5
N_EXPERTS = 32
TOP_K = 4
SWIGLU_LIMIT = 7.0
SWIGLU_ALPHA = 1.702
RMS_EPS = 1e-6

VMEM_LIMIT_BYTES = 56 * 1024 * 1024
ROW_TILE = 256
MERGE_TILE = 512
SB_BLOCK = 512
SB_SUB = 256
LOG2E = 1.4426950408889634
SB_DEAD_LOG2 = -200.0
RW_CHUNK = 64
RW_PREP_CHUNKS = 4
RW_SCAN_CHUNKS = 4
EXPERT_ROWS = 512
COMBINE_TILE = 512
SC_GATHER_WINDOW = 128
SC_ROW_SLAB = 256


def _params(semantics):
    return pltpu.CompilerParams(dimension_semantics=semantics, vmem_limit_bytes=VMEM_LIMIT_BYTES)


def _dot(a, b):
    return jnp.dot(a, b, preferred_element_type=F32)


def _dot_nt(a, b):
    return lax.dot_general(a, b, (((1,), (1,)), ((), ())), preferred_element_type=F32)


def _softplus_parts(z):
    l = jnp.log(1.0 + jnp.exp(-jnp.abs(z)))
    sp = jnp.maximum(z, 0.0) + l
    return sp, sp - z


def _rmsnorm(x, g):
    return x * lax.rsqrt(jnp.mean(x * x, axis=-1, keepdims=True) + RMS_EPS) * g


def _inproj_body(xa_ref, xb_ref, g_ref, w_ref, qkv16_ref, ka_ref, va_ref, kvb_ref, rw_ref, gab_ref, *,
                 q_scale, tiles_a):
    i = pl.program_id(0)
    x = jnp.where(i < tiles_a, xa_ref[...], xb_ref[...])
    h = _rmsnorm(x, g_ref[...]).astype(BF16)
    sbw = kvb_ref.shape[-1] // 2
    dh = ka_ref.shape[-1]
    q = _dot(h, w_ref[:, 0:sbw])
    qkv16_ref[:, 0:sbw] = (q * q_scale).astype(BF16)
    k = _dot(h, w_ref[:, sbw:2 * sbw])
    v = _dot(h, w_ref[:, 2 * sbw:3 * sbw])
    qkv16_ref[:, sbw:2 * sbw] = k.astype(BF16)
    qkv16_ref[:, 2 * sbw:3 * sbw] = v.astype(BF16)
    off = 3 * sbw
    for ref in (rw_ref, gab_ref):
        width = ref.shape[-1]
        for c0 in range(0, width, sbw):
            cw = min(sbw, width - c0)
            ref[:, c0:c0 + cw] = _dot(h, w_ref[:, off + c0:off + c0 + cw])
        off += width

    @pl.when(i < tiles_a)
    def _():
        for hd in range(sbw // dh):
            ka_ref[0, hd] = k[:, hd * dh:(hd + 1) * dh]
            va_ref[0, hd] = v[:, hd * dh:(hd + 1) * dh]

    @pl.when(i >= tiles_a)
    def _():
        kvb_ref[:, 0:sbw] = k
        kvb_ref[:, sbw:2 * sbw] = v


def _inproj(xa, xb, g, w_bf16, rw_proj, q_scale, tm):
    ba, ta, d = xa.shape
    nb = xb.shape[0]
    n = ba * ta + nb
    sbw = SB_HEADS * SB_HEAD_DIM
    total = w_bf16.shape[1]
    assert total == 3 * sbw + rw_proj + 2 * d and ta % tm == 0 and nb % tm == 0
    tiles_a = ba * ta // tm
    per_seq = ta // tm
    last_a = tiles_a - 1
    row = lambda wd: pl.BlockSpec((tm, wd), lambda i: (i, 0))
    head_major = pl.BlockSpec(
        (1, SB_HEADS, tm, SB_HEAD_DIM),
        lambda i: (jnp.minimum(i, last_a) // per_seq, 0, jnp.minimum(i, last_a) % per_seq, 0))
    hm_shape = jax.ShapeDtypeStruct((ba, SB_HEADS, ta, SB_HEAD_DIM), F32)
    return pl.pallas_call(
        functools.partial(_inproj_body, q_scale=q_scale, tiles_a=tiles_a),
        grid=(n // tm,),
        in_specs=[
            pl.BlockSpec((tm, d), lambda i: (jnp.minimum(i, last_a), 0)),
            pl.BlockSpec((tm, d), lambda i: (jnp.maximum(i - tiles_a, 0), 0)),
            pl.BlockSpec((1, d), lambda i: (0, 0)),
            pl.BlockSpec((d, total), lambda i: (0, 0), pipeline_mode=pl.Buffered(1)),
        ],
        out_specs=[row(3 * sbw), head_major, head_major,
                   pl.BlockSpec((tm, 2 * sbw), lambda i: (jnp.maximum(i - tiles_a, 0), 0)),
                   row(rw_proj), row(2 * d)],
        out_shape=[jax.ShapeDtypeStruct((n, 3 * sbw), BF16), hm_shape, hm_shape,
                   jax.ShapeDtypeStruct((nb, 2 * sbw), F32),
                   jax.ShapeDtypeStruct((n, rw_proj), F32), jax.ShapeDtypeStruct((n, 2 * d), F32)],
        compiler_params=_params(("arbitrary",)),
        name="inproj",
    )(xa.reshape(ba * ta, d), xb, g.reshape(1, d), w_bf16)


def _sb_tile(q, kb, vb, ntri, carry, mask, sb):
    tk = kb.shape[0]
    z = _dot_nt(q, kb)
    neg_abs = lax.bitcast_convert_type(lax.bitcast_convert_type(z, jnp.uint32) | jnp.uint32(0x80000000), F32)
    sp = jnp.maximum(z, 0.0) + jnp.log(1.0 + jnp.exp2(neg_abs)) * LOG2E
    if mask is not None:
        sp = jnp.where(mask, sp, 0.0)
    hi = sp.astype(BF16)
    lo = (sp - hi.astype(F32)).astype(BF16)
    log_beta = z - sp
    wts = []
    for j in range(tk // sb - 1, -1, -1):
        cols = slice(j * sb, (j + 1) * sb)
        later = _dot(hi[:, cols], ntri) + _dot(lo[:, cols], ntri) + carry
        w = jnp.exp2(log_beta[:, cols] + later)
        if mask is not None:
            w = jnp.where(mask[:, cols], w, 0.0)
        wts.append(w.astype(BF16))
        carry = carry - jnp.sum(sp[:, cols], axis=-1, keepdims=True)
    wts = wts[0] if len(wts) == 1 else jnp.concatenate(wts[::-1], axis=1)
    return _dot(wts, vb), carry


def _sb_body(q_ref, k_ref, v_ref, triq_ref, trik_ref, o_in_ref, o_ref, *, tq, tk, past):
    del o_in_ref
    qi = pl.program_id(2)
    q = q_ref[...]
    first = lax.broadcasted_iota(jnp.int32, q.shape, 1) < SB_HEAD_DIM
    zero = jnp.zeros_like(q)
    qs = (jnp.where(first, q, zero), jnp.where(first, zero, q))
    d0 = pl.multiple_of(past + qi * tq, tq)
    row = lax.broadcasted_iota(jnp.int32, (tq, tq), 0)
    col = lax.broadcasted_iota(jnp.int32, (tq, tq), 1)
    kd = k_ref[pl.ds(d0, tq), :]
    vd = v_ref[pl.ds(d0, tq), :]
    sbq = triq_ref.shape[0]
    state = []
    for qh in qs:
        state.extend(_sb_tile(qh, kd, vd, triq_ref[...], jnp.zeros((tq, 1), F32), col < row, sbq))
    nb = (past + qi * tq) // tk
    sbk = trik_ref.shape[0]

    def live(state):
        return jnp.max(jnp.maximum(state[1], state[3])) > SB_DEAD_LOG2

    def cond(loop):
        i, alive, _ = loop
        return jnp.logical_and(i < nb, alive)

    def step(loop):
        i, _, state = loop
        s0 = pl.multiple_of((nb - 1 - i) * tk, tk)
        kb = k_ref[pl.ds(s0, tk), :]
        vb = v_ref[pl.ds(s0, tk), :]
        new = []
        for h, qh in enumerate(qs):
            out, carry = _sb_tile(qh, kb, vb, trik_ref[...], state[2 * h + 1], None, sbk)
            new.extend((state[2 * h] + out, carry))
        return i + 1, live(new), tuple(new)

    _, _, state = lax.while_loop(cond, step, (jnp.int32(0), live(state), tuple(state)))
    o_ref[...] = jnp.where(first, state[0], state[2]).astype(o_ref.dtype)


def _later_matrix(n):
    r = lax.broadcasted_iota(jnp.int32, (n, n), 0)
    c = lax.broadcasted_iota(jnp.int32, (n, n), 1)
    return jnp.where(r > c, -1.0, 0.0).astype(BF16)


def _stick_breaking(q, q_row0, q_lane0, kv, k_lane0, v_lane0, batch, t, past, tq, tk, o_prev):
    s = past + t
    pair = 2 * SB_HEAD_DIM
    sbq = min(tq, SB_SUB)
    sbk = min(tk, SB_SUB)
    nq = t // tq
    qblk = q_row0 // tq
    q_spec = lambda lane0: pl.BlockSpec((tq, pair), lambda b, h, i: (qblk + b * nq + i, lane0 + h))
    kv_spec = lambda lane0: pl.BlockSpec((s, pair), lambda b, h, i: (b, lane0 + h))
    return pl.pallas_call(
        functools.partial(_sb_body, tq=tq, tk=tk, past=past),
        grid=(batch, SB_HEADS // 2, nq),
        in_specs=[
            q_spec(q_lane0), kv_spec(k_lane0), kv_spec(v_lane0),
            pl.BlockSpec((sbq, sbq), lambda b, h, i: (0, 0)),
            pl.BlockSpec((sbk, sbk), lambda b, h, i: (0, 0)),
            pl.BlockSpec(memory_space=pl.ANY),
        ],
        out_specs=q_spec(0),
        out_shape=jax.ShapeDtypeStruct(o_prev.shape, BF16),
        input_output_aliases={5: 0},
        compiler_params=_params(("parallel", "parallel", "arbitrary")),
        name="stick_breaking",
    )(q, kv, kv, _later_matrix(sbq), _later_matrix(sbk), o_prev)


def _split_bf16(x):
    hi = x.astype(BF16)
    return hi, (x - hi.astype(F32)).astype(BF16)


def _dot3(a, b, dot=_dot):
    ah, al = _split_bf16(a)
    bh, bl = _split_bf16(b)
    return dot(ah, bh) + dot(ah, bl) + dot(al, bh)


def _dot_ones_rhs(x, ones_bf16):
    hi, lo = _split_bf16(x)
    return _dot(hi, ones_bf16) + _dot(lo, ones_bf16)


def _dot_ones_lhs(ones_bf16, x):
    hi, lo = _split_bf16(x)
    return _dot(ones_bf16, hi) + _dot(ones_bf16, lo)


HEADS_PER_MATMUL = 4
GROUP_WIDTH = HEADS_PER_MATMUL * RW_HEAD_DIM


def _block_diag(y, mask):
    return jnp.concatenate([y] * HEADS_PER_MATMUL, axis=0) * mask


def _bdmm(xs, ys, mask, dot=_dot, extended=False):
    xh, xl = xs
    yh, yl = ys
    m = xh.shape[0]
    x2 = jnp.concatenate([xh, xl], axis=0) if extended else xh
    outs = []
    for g in range(RW_WIDTH // GROUP_WIDTH):
        ls = slice(g * GROUP_WIDTH, (g + 1) * GROUP_WIDTH)
        out = dot(x2[:, ls], _block_diag(yh[:, ls], mask))
        if extended:
            out = out[:m] + out[m:] + dot(xh[:, ls], _block_diag(yl[:, ls], mask))
        outs.append(out)
    return jnp.concatenate(outs, axis=1)


def _bd_tn(x, y, mask):
    n = RW_HEAD_DIM
    outs = []
    for g in range(RW_WIDTH // GROUP_WIDTH):
        ls = slice(g * GROUP_WIDTH, (g + 1) * GROUP_WIDTH)
        z = _dot(x[:, ls].T.astype(BF16), y[:, ls].astype(BF16)) * mask
        outs.append(z[0:n] + z[n:2 * n] + z[2 * n:3 * n] + z[3 * n:4 * n])
    return jnp.concatenate(outs, axis=1)


def _rwkv_prep_body(p_ref, pb_ref, prev_ref, mu_ref, w0_ref, w2_ref, a0_ref, a2_ref, g2_ref, kk_ref, ka_ref,
                    rk_ref, tri_ref, bd_ref, mbf_ref, mf32_ref,
                    rhat_ref, y0_ref, a_ref, d_ref, bonus_ref, g_ref, *, chunk, groups, n_sq, t_valid):
    n = RW_HEAD_DIM
    w_ = RW_WIDTH
    p = p_ref[...]
    rows = lax.broadcasted_iota(jnp.int32, p.shape, 0)
    prev_row = jnp.where(pl.program_id(1) == 0, prev_ref[0], pb_ref[7:8, :])
    p_shift = jnp.where(rows == 0, prev_row, pltpu.roll(p, 1, 0))

    xs = p + (p_shift - p) * mu_ref[...]
    r = xs[:, 0:w_]
    k = xs[:, w_:2 * w_]
    v = xs[:, 2 * w_:3 * w_]
    o1 = 3 * w_
    dw = xs[:, o1:o1 + RW_DECAY_LORA]
    da = xs[:, o1 + RW_DECAY_LORA:o1 + RW_DECAY_LORA + RW_AAA_LORA]
    dg = xs[:, o1 + RW_DECAY_LORA + RW_AAA_LORA:]

    sp_neg = _softplus_parts(w0_ref[...] + _dot3(jnp.tanh(dw), w2_ref[...]))[1]
    log_decay = -jnp.exp(-sp_neg - 0.5)
    a = jax.nn.sigmoid(a0_ref[...] + _dot3(da, a2_ref[...]))
    g_ref[...] = _dot(jax.nn.sigmoid(dg).astype(BF16), g2_ref[...])

    bd = bd_ref[...]
    kk = k * kk_ref[...]
    kk = kk / jnp.maximum(jnp.sqrt(_dot_ones_rhs(kk * kk, bd)), 1e-12)
    k2 = k * (1.0 + (a - 1.0) * ka_ref[...])
    a_vec = -kk
    b_vec = kk * a
    bonus_ref[...] = _dot((r * k2 * rk_ref[...]).astype(BF16), bd) * v
    if t_valid < chunk:
        live = lax.broadcasted_iota(jnp.int32, r.shape, 0) < t_valid
        keep = lambda x: jnp.where(live, x, 0.0)
        log_decay, r, k2, v, a_vec, b_vec = (keep(x) for x in (log_decay, r, k2, v, a_vec, b_vec))

    cw = _dot_ones_lhs(tri_ref[...], log_decay)
    cw_last = jnp.concatenate(
        [jnp.broadcast_to(cw[(j + 1) * chunk - 1:(j + 1) * chunk, :], (chunk, w_)) for j in range(groups)], axis=0)
    r_t = r * jnp.exp(cw)
    a_t = a_vec * jnp.exp(cw - log_decay)
    e_neg = jnp.exp(-cw)
    b_t = b_vec * e_neg
    k_t = k2 * e_neg
    e_rel = jnp.exp(cw_last - cw)
    b_h = b_vec * e_rel
    k_h = k2 * e_rel
    w_c = jnp.exp(cw_last)

    rr = lax.broadcasted_iota(jnp.int32, (chunk, w_), 0)
    ss = lax.broadcasted_iota(jnp.int32, (chunk, w_), 1) % n
    strict = ss < rr
    incl = ss <= rr
    eye = (ss == rr).astype(F32)
    m_bf16 = mbf_ref[...]
    m_f32 = mf32_ref[...]

    for j in range(groups):
        rs = slice(j * chunk, (j + 1) * chunk)
        v_j = v[rs]
        v_s = _split_bf16(v_j)
        ar_s = _split_bf16(jnp.concatenate([a_t[rs], r_t[rs]], axis=0))
        gram_b = _bdmm(ar_s, _split_bf16(b_t[rs]), m_bf16, _dot_nt)
        gram_k = _bdmm(ar_s, _split_bf16(k_t[rs]), m_bf16, _dot_nt)
        l_ab = jnp.where(strict, gram_b[:chunk], 0.0)
        l_ak = jnp.where(strict, gram_k[:chunk], 0.0)
        m_rb_s = _split_bf16(jnp.where(incl, gram_b[chunk:], 0.0))
        m_rk_s = _split_bf16(jnp.where(incl, gram_k[chunk:], 0.0))
        l_s = _split_bf16(l_ab)
        t_inv = eye + l_ab
        l_pow = _bdmm(l_s, l_s, m_bf16)
        for i in range(n_sq):
            lp_s = _split_bf16(l_pow)
            t_inv = t_inv + _bdmm(lp_s, _split_bf16(t_inv), m_bf16)
            if i + 1 < n_sq:
                l_pow = _bdmm(lp_s, lp_s, m_bf16)
        t_s = _split_bf16(t_inv)
        p_mat = _bdmm(t_s, _split_bf16(a_t[rs]), m_bf16)
        q_mat = _bdmm(t_s, _split_bf16(_bdmm(_split_bf16(l_ak), v_s, m_bf16)), m_bf16)
        rhat_ref[rs, :] = r_t[rs] + _bdmm(m_rb_s, _split_bf16(p_mat), m_bf16)
        y0_ref[rs, :] = _bdmm(m_rb_s, _split_bf16(q_mat), m_bf16) + _bdmm(m_rk_s, v_s, m_bf16)
        ns = slice(j * n, (j + 1) * n)
        a_ref[ns, :] = _bd_tn(p_mat, b_h[rs], m_f32) + eye * w_c[j * chunk:j * chunk + 1, :]
        d_ref[ns, :] = _bd_tn(jnp.concatenate([q_mat, v_j], axis=0),
                              jnp.concatenate([b_h[rs], k_h[rs]], axis=0), m_f32)


def _rwkv_scan_body(rhat_ref, y0_ref, a_ref, d_ref, bonus_ref, g_ref, s0_ref, lng_ref, lnb_ref, bd_ref, mbf_ref,
                    o_ref, s_ref, s_scr, y_scr, *, chunk, groups, steps):
    n = RW_HEAD_DIM
    c = pl.program_id(1)

    @pl.when(c == 0)
    def _():
        for h in range(RW_HEADS):
            s_scr[:, h * n:(h + 1) * n] = s0_ref[0, h]

    mask = mbf_ref[...]
    s = s_scr[...]
    for j in range(groups):
        rs = slice(j * chunk, (j + 1) * chunk)
        ns = slice(j * n, (j + 1) * n)
        s_s = _split_bf16(s)
        y_scr[rs, :] = y0_ref[rs, :] + _bdmm(_split_bf16(rhat_ref[rs, :]), s_s, mask, _dot_nt, extended=True)
        s = _bdmm(s_s, _split_bf16(a_ref[ns, :]), mask, extended=True) + d_ref[ns, :]
    s_scr[...] = s

    @pl.when(c == steps - 1)
    def _():
        for h in range(RW_HEADS):
            s_ref[0, h] = s[:, h * n:(h + 1) * n]

    bd = bd_ref[...]
    y = y_scr[...]
    mean = _dot_ones_rhs(y, bd) * (1.0 / n)
    yc = y - mean
    var = _dot_ones_rhs(yc * yc, bd) * (1.0 / n)
    yn = yc * lax.rsqrt(var + RW_GN_EPS) * lng_ref[...] + lnb_ref[...]
    o_ref[...] = ((yn + bonus_ref[...]) * g_ref[...]).astype(o_ref.dtype)


def _rwkv(rw_all, batch, t, t_valid, prev, s0, lp, o_prev):
    proj = rw_all.shape[1]
    n_all = o_prev.shape[0]
    row_off = 0
    n = RW_HEAD_DIM
    chunk = RW_CHUNK
    assert chunk == n and t % chunk == 0 and (t_valid == t or t == chunk)
    n_sq = chunk.bit_length() - 2
    nc = t // chunk
    gp = min(RW_PREP_CHUNKS, nc)
    gs = min(RW_SCAN_CHUNKS, nc)
    rows_p = gp * chunk
    rows_s = gs * chunk
    n_tok = batch * t
    row = lambda x: x.reshape(1, -1).astype(F32)
    r = lax.broadcasted_iota(jnp.int32, (rows_p, rows_p), 0)
    c = lax.broadcasted_iota(jnp.int32, (rows_p, rows_p), 1)
    tri = ((c <= r) & (c // chunk == r // chunk)).astype(BF16)
    hr = lax.broadcasted_iota(jnp.int32, (RW_WIDTH, RW_WIDTH), 0) // n
    hc = lax.broadcasted_iota(jnp.int32, (RW_WIDTH, RW_WIDTH), 1) // n
    bd = (hr == hc).astype(BF16)
    group_mask = (hr == hc)[:GROUP_WIDTH, :GROUP_WIDTH]
    const = lambda shape: pl.BlockSpec(shape, lambda b, c: (0,) * len(shape))

    blk_p = row_off // rows_p
    prep_args = [rw_all, rw_all, prev.astype(F32), row(lp['rw_mu']), row(lp['rw_w0']), lp['rw_w2'].astype(F32),
                 row(lp['rw_a0']), lp['rw_a2'].astype(F32), lp['rw_g2'].astype(BF16), row(lp['rw_k_k']),
                 row(lp['rw_k_a']), row(lp['rw_r_k']), tri, bd, group_mask.astype(BF16), group_mask.astype(F32)]
    tok_spec = lambda rows: pl.BlockSpec((rows, RW_WIDTH), lambda b, c: (b * (t // rows) + c, 0))
    mat_spec = lambda g: pl.BlockSpec((g * n, RW_WIDTH), lambda b, c: (b * (nc // g) + c, 0))
    tok_shape = jax.ShapeDtypeStruct((n_tok, RW_WIDTH), F32)
    mat_shape = jax.ShapeDtypeStruct((batch * nc * n, RW_WIDTH), F32)
    rhat, y0, a_mat, d_mat, bonus, g = pl.pallas_call(
        functools.partial(_rwkv_prep_body, chunk=chunk, groups=gp, n_sq=n_sq, t_valid=t_valid),
        grid=(batch, nc // gp),
        in_specs=[
            pl.BlockSpec((rows_p, proj), lambda b, c: (blk_p + b * (nc // gp) + c, 0)),
            pl.BlockSpec((8, proj), lambda b, c: (jnp.maximum((row_off + b * t + c * rows_p) // 8 - 1, 0), 0)),
            pl.BlockSpec((1, 1, proj), lambda b, c: (b, 0, 0)),
        ] + [const(x.shape) for x in prep_args[3:]],
        out_specs=[tok_spec(rows_p), tok_spec(rows_p), mat_spec(gp), mat_spec(gp), tok_spec(rows_p),
                   tok_spec(rows_p)],
        out_shape=[tok_shape, tok_shape, mat_shape, mat_shape, tok_shape, tok_shape],
        compiler_params=_params(("parallel", "parallel")),
        name="rwkv7_prep",
    )(*prep_args)

    blk_s = row_off // rows_s
    scan_args = [rhat, y0, a_mat, d_mat, bonus, g, s0.astype(F32), row(lp['rw_ln_g']), row(lp['rw_ln_b']), bd,
                 group_mask.astype(BF16), o_prev]
    state_spec = pl.BlockSpec((1, RW_HEADS, n, n), lambda b, c: (b, 0, 0, 0))
    n_in = len(scan_args) - 1

    def scan_body(*refs):
        _rwkv_scan_body(*refs[:n_in], *refs[n_in + 1:], chunk=chunk, groups=gs, steps=nc // gs)

    o, state = pl.pallas_call(
        scan_body,
        grid=(batch, nc // gs),
        in_specs=[tok_spec(rows_s), tok_spec(rows_s), mat_spec(gs), mat_spec(gs), tok_spec(rows_s),
                  tok_spec(rows_s), state_spec, const((1, RW_WIDTH)), const((1, RW_WIDTH)), const(bd.shape),
                  const(group_mask.shape), pl.BlockSpec(memory_space=pl.ANY)],
        out_specs=[pl.BlockSpec((rows_s, RW_WIDTH), lambda b, c: (blk_s + b * (nc // gs) + c, 0)), state_spec],
        out_shape=[jax.ShapeDtypeStruct((n_all, RW_WIDTH), BF16),
                   jax.ShapeDtypeStruct((batch, RW_HEADS, n, n), F32)],
        scratch_shapes=[pltpu.VMEM((n, RW_WIDTH), F32), pltpu.VMEM((rows_s, RW_WIDTH), F32)],
        input_output_aliases={n_in: 0},
        compiler_params=_params(("parallel", "arbitrary")),
        name="rwkv7_scan",
    )(*scan_args)
    return o, state


def _merge_body(oa_ref, ob_ref, gab_ref, xa_ref, xb_ref, wa_ref, wb_ref, wo_ref, g2_ref, wr_ref, br_ref, tri_ref,
                x2_ref, *rest, tiles_a):
    h2_refs, (idx_ref, gate_ref, rank_ref, cnt_ref, carry_scr) = rest[:-5], rest[-5:]
    i = pl.program_id(0)
    d = xa_ref.shape[-1]

    @pl.when(i == 0)
    def _():
        carry_scr[...] = jnp.zeros_like(carry_scr)

    gab = gab_ref[...]
    merged = (jax.nn.sigmoid(gab[:, :d]) * _dot(oa_ref[...], wa_ref[...])
              + jax.nn.sigmoid(gab[:, d:]) * _dot(ob_ref[...], wb_ref[...]))
    x = jnp.where(i < tiles_a, xa_ref[...], xb_ref[...])
    x2 = x + _dot(merged.astype(BF16), wo_ref[...])
    x2_ref[...] = x2
    h2 = _rmsnorm(x2, g2_ref[...])
    for q, ref in enumerate(h2_refs):
        ref[...] = h2[:, q * SC_ROW_SLAB:(q + 1) * SC_ROW_SLAB]
    logits = _dot3(h2, wr_ref[...]) + br_ref[...]

    tm, ne = logits.shape
    col = lax.broadcasted_iota(jnp.int32, (tm, ne), 1)
    c4 = lax.broadcasted_iota(jnp.int32, (tm, TOP_K), 1)
    work = logits
    tops, idxs = [], []
    for _ in range(TOP_K):
        m = jnp.max(work, axis=-1, keepdims=True)
        ix = jnp.min(jnp.where(work == m, col, ne), axis=-1, keepdims=True)
        tops.append(m)
        idxs.append(ix)
        work = jnp.where(col == ix, -jnp.inf, work)
    es = [jnp.exp(m - tops[0]) for m in tops]
    denom = es[0] + es[1] + es[2] + es[3]
    onehot = jnp.zeros((tm, ne), F32)
    for ix in idxs:
        onehot = onehot + (col == ix).astype(F32)
    before = carry_scr[...] + _dot(tri_ref[...], onehot.astype(BF16))
    idx_out = jnp.zeros((tm, TOP_K), jnp.int32)
    gate_out = jnp.zeros((tm, TOP_K), F32)
    rank_out = jnp.zeros((tm, TOP_K), jnp.int32)
    for kk in range(TOP_K):
        rk = jnp.sum(jnp.where(col == idxs[kk], before, 0.0), axis=-1, keepdims=True)
        idx_out = jnp.where(c4 == kk, idxs[kk], idx_out)
        gate_out = jnp.where(c4 == kk, es[kk] / denom, gate_out)
        rank_out = jnp.where(c4 == kk, rk.astype(jnp.int32), rank_out)
    idx_ref[...] = idx_out
    gate_ref[...] = gate_out
    rank_ref[...] = rank_out
    carry = carry_scr[...] + jnp.sum(onehot, axis=0, keepdims=True)
    carry_scr[...] = carry
    cnt_ref[...] = carry.astype(jnp.int32)


def _merge_route(oa, ob, gab, xa, xb, lp, tm):
    d = xa.shape[1]
    n = xa.shape[0] + xb.shape[0]
    assert xa.shape[0] % tm == 0 and xb.shape[0] % tm == 0
    tiles_a = xa.shape[0] // tm
    ne = N_EXPERTS
    r = lax.broadcasted_iota(jnp.int32, (tm, tm), 0)
    c = lax.broadcasted_iota(jnp.int32, (tm, tm), 1)
    tri = (c < r).astype(BF16)
    const = lambda shape: pl.BlockSpec(shape, lambda i: (0,) * len(shape))
    rowblk = lambda wd: pl.BlockSpec((tm, wd), lambda i: (i, 0))
    wa = lp['w_branch_a'].astype(BF16)
    wb = lp['w_branch_b'].astype(BF16)
    wo = lp['w_out'].astype(BF16)
    return pl.pallas_call(
        functools.partial(_merge_body, tiles_a=tiles_a),
        grid=(n // tm,),
        in_specs=[rowblk(oa.shape[1]), rowblk(ob.shape[1]), rowblk(2 * d),
                  pl.BlockSpec((tm, d), lambda i: (jnp.minimum(i, tiles_a - 1), 0)),
                  pl.BlockSpec((tm, d), lambda i: (jnp.maximum(i - tiles_a, 0), 0)),
                  const(wa.shape), const(wb.shape), const(wo.shape), const((1, d)), const((d, ne)),
                  const((1, ne)), const((tm, tm))],
        out_specs=[rowblk(d)] + [rowblk(SC_ROW_SLAB)] * (d // SC_ROW_SLAB) + [
                   rowblk(TOP_K), rowblk(TOP_K), rowblk(TOP_K), const((1, ne))],
        out_shape=[jax.ShapeDtypeStruct((n, d), F32)] + [jax.ShapeDtypeStruct((n, SC_ROW_SLAB), F32)] * (
                   d // SC_ROW_SLAB) + [
                   jax.ShapeDtypeStruct((n, TOP_K), jnp.int32), jax.ShapeDtypeStruct((n, TOP_K), F32),
                   jax.ShapeDtypeStruct((n, TOP_K), jnp.int32), jax.ShapeDtypeStruct((1, ne), jnp.int32)],
        scratch_shapes=[pltpu.VMEM((1, ne), F32)],
        compiler_params=_params(("arbitrary",)),
        name="merge_route",
    )(oa, ob, gab, xa, xb, wa, wb, wo, lp['norm2_g'].reshape(1, d).astype(F32), lp['w_router'].astype(F32),
      lp['b_router'].reshape(1, ne).astype(F32), tri)


def _expert_body(be_ref, used_ref, *refs, n_slabs):
    xs_refs, (wgu_ref, bgu_ref, wd_ref, bd_ref) = refs[:n_slabs], refs[n_slabs:n_slabs + 4]
    ys_refs, (wgu16, wd16) = refs[n_slabs + 4:-2], refs[-2:]
    i = pl.program_id(0)
    de = wd_ref.shape[1]

    @pl.when(jnp.logical_or(i == 0, be_ref[i] != be_ref[jnp.maximum(i - 1, 0)]))
    def _():
        wgu16[...] = wgu_ref[0].astype(BF16)
        wd16[...] = wd_ref[0].astype(BF16)

    @pl.when(i < used_ref[0])
    def _():
        x = jnp.concatenate([ref[...].astype(BF16) for ref in xs_refs], axis=1)
        gu = _dot(x, wgu16[...]) + bgu_ref[0]
        g = jnp.minimum(gu[:, :de], SWIGLU_LIMIT)
        u = jnp.clip(gu[:, de:], -SWIGLU_LIMIT, SWIGLU_LIMIT)
        act = (u + 1.0) * (g * jax.nn.sigmoid(g * SWIGLU_ALPHA))
        y = _dot(act.astype(BF16), wd16[...]) + bd_ref[0]
        cw = ys_refs[0].shape[-1]
        for q, ref in enumerate(ys_refs):
            ref[...] = y[:, q * cw:(q + 1) * cw]

    @pl.when(i >= used_ref[0])
    def _():
        for ref in ys_refs:
            ref[...] = jnp.zeros_like(ref)


def _experts(xs, block_expert, n_used, wgu, bgu, wd, bd, bm):
    n_rows = xs[0].shape[0]
    ne, d, de2 = wgu.shape
    n_slabs = len(xs)
    nb = n_rows // bm
    slab = pl.BlockSpec((bm, SC_ROW_SLAB), lambda i, be, nu: (i, 0))
    grid_spec = pltpu.PrefetchScalarGridSpec(
        num_scalar_prefetch=2,
        grid=(nb,),
        in_specs=[slab] * n_slabs + [
            pl.BlockSpec((1, d, de2), lambda i, be, nu: (be[i], 0, 0)),
            pl.BlockSpec((1, 1, de2), lambda i, be, nu: (be[i], 0, 0)),
            pl.BlockSpec((1, de2 // 2, d), lambda i, be, nu: (be[i], 0, 0)),
            pl.BlockSpec((1, 1, d), lambda i, be, nu: (be[i], 0, 0)),
        ],
        out_specs=[slab] * n_slabs,
        scratch_shapes=[pltpu.VMEM((d, de2), BF16), pltpu.VMEM((de2 // 2, d), BF16)],
    )
    return pl.pallas_call(
        functools.partial(_expert_body, n_slabs=n_slabs),
        grid_spec=grid_spec,
        out_shape=[jax.ShapeDtypeStruct((n_rows, SC_ROW_SLAB), F32)] * n_slabs,
        compiler_params=_params(("arbitrary",)),
        name="moe_experts",
    )(block_expert, n_used, *xs, wgu, bgu.reshape(ne, 1, de2), wd, bd.reshape(ne, 1, d))


def _sc_gather_rows(x, indices, window):
    n = indices.shape[0]
    d = x.shape[1]
    mesh = plsc.VectorSubcoreMesh(core_axis_name="core", subcore_axis_name="subcore")

    @pl.kernel(out_type=jax.ShapeDtypeStruct((n, d), x.dtype), mesh=mesh, name="sc_gather_rows")
    def gather(x_hbm, i_hbm, o_hbm):
        def body(i_vmem, o_vmem):
            pltpu.sync_copy(x_hbm.at[i_vmem.at[0]], o_vmem)

        pltpu.emit_pipeline(
            body,
            grid=(n // window,),
            in_specs=[pl.BlockSpec((1, window), lambda i: (0, i))],
            out_specs=[pl.BlockSpec((window, d), lambda i: (i, 0))],
            core_axis_name=("core", "subcore"),
            dimension_semantics=(pltpu.PARALLEL,),
        )(i_hbm, o_hbm)

    return gather(x, indices.reshape(1, n))


def _combine_stream_body(*refs, tiles_a, n_slabs):
    row_refs = refs[:n_slabs * TOP_K]
    gate_ref, x2_ref, g_ref, ya_ref, yb_ref = refs[n_slabs * TOP_K:]
    gate = gate_ref[...]
    parts = []
    for s in range(n_slabs):
        part = gate[:, 0:1] * row_refs[s * TOP_K][...]
        for kk in range(1, TOP_K):
            part = part + gate[:, kk:kk + 1] * row_refs[s * TOP_K + kk][...]
        parts.append(part)
    y = _rmsnorm(x2_ref[...] + jnp.concatenate(parts, axis=1), g_ref[...])
    i = pl.program_id(0)

    @pl.when(i < tiles_a)
    def _():
        ya_ref[...] = y

    @pl.when(i >= tiles_a)
    def _():
        yb_ref[...] = y


def _combine_stream(slabs, gate, x2, final_g, tile, n_a):
    n, d = x2.shape
    assert n % tile == 0 and n_a % tile == 0
    tiles_a = n_a // tile
    nt = n // tile
    choice = lambda s, kk: pl.BlockSpec((tile, s.shape[1]), lambda i: (kk * nt + i, 0))
    return pl.pallas_call(
        functools.partial(_combine_stream_body, tiles_a=tiles_a, n_slabs=len(slabs)),
        grid=(nt,),
        in_specs=[choice(s, kk) for s in slabs for kk in range(TOP_K)] + [
                  pl.BlockSpec((tile, TOP_K), lambda i: (i, 0)),
                  pl.BlockSpec((tile, d), lambda i: (i, 0)),
                  pl.BlockSpec((1, d), lambda i: (0, 0))],
        out_specs=[pl.BlockSpec((tile, d), lambda i: (jnp.minimum(i, tiles_a - 1), 0)),
                   pl.BlockSpec((tile, d), lambda i: (jnp.maximum(i - tiles_a, 0), 0))],
        out_shape=[jax.ShapeDtypeStruct((n_a, d), F32), jax.ShapeDtypeStruct((n - n_a, d), F32)],
        compiler_params=_params(("arbitrary",)),
        name="moe_combine",
    )(*[s for s in slabs for _ in range(TOP_K)], gate, x2, final_g.reshape(1, d).astype(F32))


def _moe(h2, x2, idx, gate, rank, counts, lp, final_g, n_a):
    n, d = x2.shape
    bm = EXPERT_ROWS
    counts = counts.reshape(N_EXPERTS)
    padded = (counts + bm - 1) // bm * bm
    pad_end = jnp.cumsum(padded)
    dest = ((pad_end - padded)[idx] + rank).reshape(n * TOP_K).astype(jnp.int32)
    n_blocks = -(-(n * TOP_K) // bm) + N_EXPERTS
    block_start = jnp.arange(n_blocks, dtype=jnp.int32) * bm
    block_expert = jnp.minimum(jnp.sum(pad_end[None, :] <= block_start[:, None], axis=1),
                               N_EXPERTS - 1).astype(jnp.int32)
    n_used = (pad_end[-1:] // bm).astype(jnp.int32)
    tok = jnp.arange(n * TOP_K, dtype=jnp.int32) // TOP_K
    row_tok = jnp.zeros((n_blocks * bm,), jnp.int32).at[dest].set(tok, unique_indices=True)
    xs = [_sc_gather_rows(h_slab, row_tok, SC_GATHER_WINDOW) for h_slab in h2]
    ys = _experts(xs, block_expert, n_used, lp['w_gate_up'].astype(F32), lp['b_gate_up'].astype(F32),
                  lp['w_down'].astype(F32), lp['b_down'].astype(F32), bm)
    by_choice = dest.reshape(n, TOP_K).T.reshape(n * TOP_K)
    slabs = [_sc_gather_rows(y_slab, by_choice, SC_GATHER_WINDOW) for y_slab in ys]
    return _combine_stream(slabs, gate, x2, final_g, math.gcd(math.gcd(n, n_a), COMBINE_TILE), n_a)


def _to_heads(t, b, s):
    return t.reshape(b, s, SB_HEADS, SB_HEAD_DIM).transpose(0, 2, 1, 3)


def _from_heads(t):
    b, h, s, dh = t.shape
    return t.transpose(0, 2, 1, 3).reshape(b, s, h * dh)


def _layer(x_p, x_s, cache_k, cache_v, state_rwkv, state_shift, lp, final_g):
    bp, tp, d = x_p.shape
    bs, ts, _ = x_s.shape
    sbw = SB_HEADS * SB_HEAD_DIM
    rw_proj = lp['rw_mu'].shape[0]
    n_p = bp * tp
    n = n_p + bs * ts
    x_s = x_s.reshape(bs * ts, d)
    scale = LOG2E * SB_HEAD_DIM ** -0.5
    qkv16, kp, vp, kv_new, rw, gab = _inproj(x_p, x_s, lp['norm1_g'].astype(F32), lp['w_in'].astype(BF16),
                                             rw_proj, scale, ROW_TILE)
    pairs = SB_HEADS // 2
    oa = _stick_breaking(qkv16, 0, 0, qkv16, pairs, 2 * pairs, bp, tp, 0, SB_BLOCK, SB_SUB,
                         jnp.zeros((n, sbw), BF16))
    past = cache_k.shape[2]
    new16 = qkv16[n_p:].reshape(bs, ts, 3 * sbw)
    kv_s = jnp.concatenate([
        jnp.concatenate([_from_heads(cache_k).astype(BF16), new16[:, :, sbw:2 * sbw]], axis=1),
        jnp.concatenate([_from_heads(cache_v).astype(BF16), new16[:, :, 2 * sbw:]], axis=1)], axis=2)
    oa = _stick_breaking(qkv16, n_p, 0, kv_s.reshape(bs * (past + ts), 2 * sbw), 0, pairs, bs, ts, past, ts,
                         min(SB_SUB, past), oa)
    ks, vs = kv_new[:, :sbw], kv_new[:, sbw:]

    s0_p = jnp.zeros((bp, RW_HEADS, RW_HEAD_DIM, RW_HEAD_DIM), F32)
    prev_p = jnp.zeros((bp, 1, rw_proj), F32)
    ob, st_p = _rwkv(rw, bp, tp, tp, prev_p, s0_p, lp, jnp.zeros((n, RW_WIDTH), BF16))
    rw_s = rw[n_p:].reshape(bs, ts, rw_proj)
    ts_pad = -(-ts // RW_CHUNK) * RW_CHUNK
    rw_s_pad = jnp.pad(rw_s, ((0, 0), (0, ts_pad - ts), (0, 0))).reshape(bs * ts_pad, rw_proj)
    ob_s, st_s = _rwkv(rw_s_pad, bs, ts_pad, ts, state_shift, state_rwkv, lp,
                       jnp.zeros((bs * ts_pad, RW_WIDTH), BF16))
    ob = lax.dynamic_update_slice(
        ob, ob_s.reshape(bs, ts_pad, RW_WIDTH)[:, :ts].reshape(bs * ts, RW_WIDTH), (n_p, 0))
    sh_p = jnp.stack([rw[(b + 1) * tp - 1:(b + 1) * tp] for b in range(bp)])
    sh_s = rw_s[:, ts - 1:]

    x2, *h2, idx, gate, rank, counts = _merge_route(oa, ob, gab, x_p.reshape(n_p, d), x_s, lp, MERGE_TILE)
    y = _moe(h2, x2, idx, gate, rank, counts, lp, final_g, n_p)
    return (y, kp, vp, st_p, sh_p, _to_heads(ks, bs, ts), _to_heads(vs, bs, ts), st_s, sh_s)


def kernel(x_prompt, x_sample, cache_sb_k, cache_sb_v, state_rwkv, state_shift, norm1_g, w_in, rw_mu, rw_w0, rw_w2, rw_a0, rw_a2, rw_g2, rw_k_k, rw_k_a, rw_r_k, rw_ln_g, rw_ln_b, w_branch_a, w_branch_b, w_out, norm2_g, w_router, b_router, w_gate_up, b_gate_up, w_down, b_down, final_norm_g):
    depth = w_in.shape[0]
    assert depth == 1, "the final RMSNorm is fused into the last (only) layer"
    bp, tp, d = x_prompt.shape
    bs, ts, _ = x_sample.shape
    lp = dict(norm1_g=norm1_g[0], w_in=w_in[0], rw_mu=rw_mu[0], rw_w0=rw_w0[0], rw_w2=rw_w2[0], rw_a0=rw_a0[0],
              rw_a2=rw_a2[0], rw_g2=rw_g2[0], rw_k_k=rw_k_k[0], rw_k_a=rw_k_a[0], rw_r_k=rw_r_k[0].reshape(-1),
              rw_ln_g=rw_ln_g[0], rw_ln_b=rw_ln_b[0], w_branch_a=w_branch_a[0], w_branch_b=w_branch_b[0],
              w_out=w_out[0], norm2_g=norm2_g[0], w_router=w_router[0], b_router=b_router[0],
              w_gate_up=w_gate_up[0], b_gate_up=b_gate_up[0], w_down=w_down[0], b_down=b_down[0])
    y, kp, vp, st_p, sh_p, ks, vs, st_s, sh_s = _layer(
        x_prompt, x_sample, cache_sb_k[0], cache_sb_v[0], state_rwkv[0], state_shift[0], lp, final_norm_g)
    return (y[0].reshape(bp, tp, d), y[1].reshape(bs, ts, d),
            kp[None], vp[None], st_p[None], sh_p[None], ks[None], vs[None], st_s[None], sh_s[None])
```
